```python
import jax
import jax.numpy as jnp
from jax import lax
import numpy as np


D_MODEL = 2048
BATCH = 8
SEQ = 2048
DEPTH = 2

GRID_W = 64
CTX_LEN = 256
N_FOURIER_GROUPS = 4
FOURIER_GROUP_CH = D_MODEL // 8
FOURIER_W = N_FOURIER_GROUPS * FOURIER_GROUP_CH
NA_HEADS = 16
NA_HEAD_DIM = 64
NA_W = NA_HEADS * NA_HEAD_DIM
NA_ROWS = 8
NA_COLS = 16
CONV_W = D_MODEL
CONV_K = 31
N_EXPERTS = 16
N_GROUPS = 4
EXPERTS_PER_GROUP = N_EXPERTS // N_GROUPS
TOP_K = 2
F_EXPERT = 512
N_MOD = 6
EPS = 1e-6

kernel_name = 'hybrid_fourier_natten_conformer_groupmoe_dit'


def rms_norm(x, g):
    xf = x.astype(jnp.float32)
    y = xf * lax.rsqrt(jnp.mean(xf * xf, axis=-1, keepdims=True) + EPS)
    return (y * g.astype(jnp.float32)).astype(x.dtype)


def modulate(h, shift, scale):
    return h * (1 + scale) + shift


def split_heads(t):
    b, l, _ = t.shape
    return t.reshape(b, l, NA_HEADS, NA_HEAD_DIM)


def fourier_mix(u):
    b, l, _ = u.shape
    ug = u.astype(jnp.float32).reshape(b, l, N_FOURIER_GROUPS, FOURIER_GROUP_CH)
    f = jnp.fft.fft2(ug, axes=(1, 3), norm='ortho').real
    return f.reshape(b, l, FOURIER_W).astype(u.dtype)


def context_attention(q, k, v):
    scale = q.shape[-1] ** -0.5
    s = jnp.einsum('bqhd,bkhd->bhqk', q, k).astype(jnp.float32) * scale
    p = jax.nn.softmax(s, axis=-1).astype(v.dtype)
    return jnp.einsum('bhqk,bkhd->bqhd', p, v)


def neighbourhood_attention(q, k, v, kc, vc, rpb):
    b, s, h, dh = q.shape
    rows = s // GRID_W
    kr = min(NA_ROWS, rows)
    scale = dh ** -0.5
    qg = q.reshape(b, rows, GRID_W, h, dh)
    kg = k.reshape(b, rows, GRID_W, h, dh)
    vg = v.reshape(b, rows, GRID_W, h, dh)
    col = jnp.arange(GRID_W)
    col_start = jnp.clip(col - NA_COLS // 2, 0, GRID_W - NA_COLS)
    col_mask = (col[None, :] >= col_start[:, None]) & (col[None, :] < col_start[:, None] + NA_COLS)
    dc_idx = jnp.clip(col[None, :] - col[:, None] + NA_COLS - 1, 0, 2 * NA_COLS - 2)
    rpb_f = rpb.astype(jnp.float32)

    def row_block(r):
        rs = jnp.clip(r - kr // 2, 0, rows - kr)
        q_r = lax.dynamic_index_in_dim(qg, r, axis=1, keepdims=False)
        k_r = lax.dynamic_slice_in_dim(kg, rs, kr, axis=1)
        v_r = lax.dynamic_slice_in_dim(vg, rs, kr, axis=1)
        s_loc = jnp.einsum('bqhd,bjkhd->bhqjk', q_r, k_r).astype(jnp.float32) * scale
        dr_idx = rs + jnp.arange(kr) - r + NA_ROWS - 1
        bias = rpb_f[:, dr_idx[:, None, None], dc_idx[None, :, :]]
        bias = bias.transpose(0, 2, 1, 3)
        s_loc = jnp.where(col_mask[:, None, :], s_loc + bias[None], -jnp.inf)
        s_loc = s_loc.reshape(b, h, GRID_W, kr * GRID_W)
        s_ctx = jnp.einsum('bqhd,bchd->bhqc', q_r, kc).astype(jnp.float32) * scale
        p = jax.nn.softmax(jnp.concatenate([s_loc, s_ctx], axis=-1), axis=-1).astype(v.dtype)
        p_loc = p[..., :kr * GRID_W].reshape(b, h, GRID_W, kr, GRID_W)
        p_ctx = p[..., kr * GRID_W:]
        return (jnp.einsum('bhqjk,bjkhd->bqhd', p_loc, v_r)
                + jnp.einsum('bhqc,bchd->bqhd', p_ctx, vc))

    o = lax.map(row_block, jnp.arange(rows))
    return jnp.moveaxis(o, 0, 1).reshape(b, s, h * dh)


def fourier_na_mixer(h, hc, w_in, rpb, w_out, need_ctx_out):
    a_end = FOURIER_W
    q_end = a_end + NA_W
    k_end = q_end + NA_W
    p = h @ w_in
    kvc = hc @ w_in[:, q_end:]
    kc = split_heads(kvc[..., :NA_W])
    vc = split_heads(kvc[..., NA_W:])
    q = split_heads(p[..., a_end:q_end])
    k = split_heads(p[..., q_end:k_end])
    v = split_heads(p[..., k_end:])
    y = jnp.concatenate([fourier_mix(p[..., :a_end]),
                         neighbourhood_attention(q, k, v, kc, vc, rpb)], axis=-1) @ w_out
    if not need_ctx_out:
        return y, None
    b, lc, _ = hc.shape
    pc = hc @ w_in[:, :q_end]
    oc = context_attention(split_heads(pc[..., a_end:]), kc, vc).reshape(b, lc, NA_W)
    yc = jnp.concatenate([fourier_mix(pc[..., :a_end]), oc], axis=-1) @ w_out
    return y, yc


def conformer_conv(h, w_in, b_in, dw_w, dw_b, ln_g, ln_b, w_out, b_out):
    p = h @ w_in + b_in
    u = p[..., :CONV_W] * jax.nn.sigmoid(p[..., CONV_W:])
    u = lax.conv_general_dilated(u, dw_w[:, None, :].astype(u.dtype), window_strides=(1,),
                                 padding=[(CONV_K // 2, CONV_K // 2)],
                                 dimension_numbers=('NWC', 'WIO', 'NWC'),
                                 feature_group_count=CONV_W) + dw_b
    uf = u.astype(jnp.float32)
    mu = jnp.mean(uf, axis=-1, keepdims=True)
    var = jnp.mean(jnp.square(uf - mu), axis=-1, keepdims=True)
    un = (uf - mu) * lax.rsqrt(var + EPS) * ln_g.astype(jnp.float32) + ln_b.astype(jnp.float32)
    un = jax.nn.silu(un).astype(h.dtype)
    return un @ w_out + b_out


def grouped_moe(h, router_w, router_b, w_gate, w_up, w_down):
    shape = h.shape
    t = h.reshape(-1, shape[-1])
    n = t.shape[0]
    scores = jax.nn.sigmoid((t @ router_w).astype(jnp.float32))
    sel = scores + router_b.astype(jnp.float32)
    grp_score = lax.top_k(sel.reshape(n, N_GROUPS, EXPERTS_PER_GROUP), 2)[0].sum(-1)
    top_g = jnp.argmax(grp_score, axis=-1)
    in_grp = (jnp.arange(N_EXPERTS) // EXPERTS_PER_GROUP)[None, :] == top_g[:, None]
    _, idx = lax.top_k(jnp.where(in_grp, sel, -jnp.inf), TOP_K)
    w = jnp.take_along_axis(scores, idx, axis=-1)
    w = w / jnp.sum(w, axis=-1, keepdims=True)
    combine = jnp.sum(jax.nn.one_hot(idx, N_EXPERTS, dtype=jnp.float32) * w[..., None], axis=1).astype(t.dtype)
    out = jnp.zeros_like(t)
    for e in range(N_EXPERTS):
        y = (jax.nn.silu(t @ w_gate[e]) * (t @ w_up[e])) @ w_down[e]
        out = out + combine[:, e:e + 1] * y
    return out.reshape(shape)


def setup_inputs(seed: int = 0) -> dict:
    key = jax.random.key(seed)
    ks = iter(jax.random.split(key, 40))
    n_even = (DEPTH + 1) // 2
    n_odd = DEPTH // 2
    f32 = jnp.float32

    def nrm(shape, scale):
        return jax.random.normal(next(ks), shape, f32) * scale

    mix_w = FOURIER_W + NA_W
    return {
        'x': nrm((BATCH, SEQ, D_MODEL), 1.0),
        'c': nrm((BATCH, D_MODEL), 1.0),
        'ctx': nrm((BATCH, CTX_LEN, D_MODEL), 1.0),
        'c_ctx': nrm((D_MODEL,), 1.0),
        'ada_w': nrm((DEPTH, D_MODEL, N_MOD * D_MODEL), 0.5 * D_MODEL ** -0.5),
        'ada_b': nrm((DEPTH, N_MOD * D_MODEL), 0.02),
        'mix_norm_g': 1.0 + nrm((DEPTH, D_MODEL), 0.02),
        'ffn_norm_g': 1.0 + nrm((DEPTH, D_MODEL), 0.02),
        'ev_w_in': nrm((n_even, D_MODEL, FOURIER_W + 3 * NA_W), D_MODEL ** -0.5),
        'ev_rpb': nrm((n_even, NA_HEADS, 2 * NA_ROWS - 1, 2 * NA_COLS - 1), 0.1),
        'ev_w_out': nrm((n_even, mix_w, D_MODEL), mix_w ** -0.5),
        'od_w_in': nrm((n_odd, D_MODEL, 2 * CONV_W), D_MODEL ** -0.5),
        'od_b_in': nrm((n_odd, 2 * CONV_W), 0.02),
        'od_dw_w': nrm((n_odd, CONV_K, CONV_W), CONV_K ** -0.5),
        'od_dw_b': nrm((n_odd, CONV_W), 0.02),
        'od_ln_g': 1.0 + nrm((n_odd, CONV_W), 0.02),
        'od_ln_b': nrm((n_odd, CONV_W), 0.02),
        'od_w_out': nrm((n_odd, CONV_W, D_MODEL), CONV_W ** -0.5),
        'od_b_out': nrm((n_odd, D_MODEL), 0.02),
        'router_w': nrm((D_MODEL, N_EXPERTS), D_MODEL ** -0.5),
        'router_b': nrm((N_EXPERTS,), 0.01),
        'moe_w_gate': nrm((DEPTH, N_EXPERTS, D_MODEL, F_EXPERT), D_MODEL ** -0.5),
        'moe_w_up': nrm((DEPTH, N_EXPERTS, D_MODEL, F_EXPERT), D_MODEL ** -0.5),
        'moe_w_down': nrm((DEPTH, N_EXPERTS, F_EXPERT, D_MODEL), F_EXPERT ** -0.5),
        'final_norm_g': 1.0 + nrm((D_MODEL,), 0.02),
    }


def reference(x, c, ctx, c_ctx, ada_w, ada_b, mix_norm_g, ffn_norm_g, ev_w_in, ev_rpb, ev_w_out,
              od_w_in, od_b_in, od_dw_w, od_dw_b, od_ln_g, od_ln_b, od_w_out, od_b_out,
              router_w, router_b, moe_w_gate, moe_w_up, moe_w_down, final_norm_g):
    b = x.shape[0]
    n_ctx = ctx.shape[1]
    silu_c = jax.nn.silu(c)
    silu_cc = jax.nn.silu(c_ctx)
    for i in range(DEPTH):
        is_even = (i % 2 == 0)
        ctx_needed_later = any(j % 2 == 0 for j in range(i + 1, DEPTH))
        mod = (silu_c @ ada_w[i] + ada_b[i]).reshape(b, N_MOD, 1, D_MODEL)
        mod_c = (silu_cc @ ada_w[i] + ada_b[i]).reshape(N_MOD, D_MODEL)
        shift1, scale1, gate1, shift2, scale2, gate2 = (mod[:, j] for j in range(N_MOD))
        h = modulate(rms_norm(x, mix_norm_g[i]), shift1, scale1)
        if is_even or ctx_needed_later:
            hc = modulate(rms_norm(ctx, mix_norm_g[i]), mod_c[0], mod_c[1])
        if is_even:
            e = i // 2
            y, yc = fourier_na_mixer(h, hc, ev_w_in[e], ev_rpb[e], ev_w_out[e], ctx_needed_later)
        else:
            o = i // 2
            conv_args = (od_w_in[o], od_b_in[o], od_dw_w[o], od_dw_b[o], od_ln_g[o], od_ln_b[o],
                         od_w_out[o], od_b_out[o])
            y = conformer_conv(h, *conv_args)
            yc = conformer_conv(hc, *conv_args) if ctx_needed_later else None
        x = x + gate1 * y
        h2 = modulate(rms_norm(x, ffn_norm_g[i]), shift2, scale2)
        if ctx_needed_later:
            ctx = ctx + mod_c[2] * yc
            hc2 = modulate(rms_norm(ctx, ffn_norm_g[i]), mod_c[3], mod_c[4])
            both = grouped_moe(jnp.concatenate([hc2, h2], axis=1), router_w, router_b,
                               moe_w_gate[i], moe_w_up[i], moe_w_down[i])
            ctx = ctx + mod_c[5] * both[:, :n_ctx]
            x = x + gate2 * both[:, n_ctx:]
        else:
            x = x + gate2 * grouped_moe(h2, router_w, router_b, moe_w_gate[i], moe_w_up[i], moe_w_down[i])
    return rms_norm(x, final_norm_g)
```

```python
import functools
import math
from typing import NamedTuple

import jax
import jax.numpy as jnp
from jax import lax
from jax.experimental import pallas as pl
from jax.experimental.pallas import tpu as pltpu

F32 = jnp.float32
BF16 = jnp.bfloat16
I32 = jnp.int32

LANES = 128
NEG_BIG = -1e30
EPS = 1e-6

N_MOD = 6
N_FOURIER_GROUPS = 4
NA_HEADS = 16
NA_HEAD_DIM = 64
NA_ROWS = 8
NA_COLS = 16
GRID_W = 64
CONV_K = 31
N_EXPERTS = 16
N_GROUPS = 4
EXPERTS_PER_GROUP = 4
PAIRS = ((0, 1), (0, 2), (0, 3), (1, 3), (1, 2), (2, 3))
N_CLASSES = N_GROUPS * len(PAIRS)
CLASS_PAD = 32
MOE_TM = 256
SORT_BLK = 512
GATHER_TM = 256
SCALAR_UNROLL = 8


class Dims(NamedTuple):
    batch: int
    seq: int
    d: int
    ctx: int
    fw: int
    naw: int
    fe: int


def _cparams(sem, vmem_mb):
    return pltpu.CompilerParams(dimension_semantics=sem, vmem_limit_bytes=vmem_mb << 20)


def _sigmoid(x):
    return 1.0 / (1.0 + jnp.exp(-x))


def _rms_mod(x, g, scale, shift):
    ms = jnp.mean(x * x, axis=-1, keepdims=True)
    return (x * lax.rsqrt(ms + EPS) * g) * (1.0 + scale) + shift


def _mod_kernel(c_ref, w_ref, b_ref, o_ref):
    c = c_ref[...]
    s = (c * _sigmoid(c)).astype(BF16)
    o_ref[0] = jnp.dot(s, w_ref[0].astype(BF16), preferred_element_type=F32) + b_ref[0]


def adaln_mod(cvec, ada_w, ada_b, tn):
    depth, d, n = ada_w.shape
    rows = cvec.shape[0]
    return pl.pallas_call(
        _mod_kernel,
        out_shape=jax.ShapeDtypeStruct((depth, rows, n), F32),
        grid=(depth, n // tn),
        in_specs=[
            pl.BlockSpec((rows, d), lambda l, j: (0, 0)),
            pl.BlockSpec((1, d, tn), lambda l, j: (l, 0, j)),
            pl.BlockSpec((1, 1, tn), lambda l, j: (l, 0, j)),
        ],
        out_specs=pl.BlockSpec((1, rows, tn), lambda l, j: (l, 0, j)),
        compiler_params=_cparams(("parallel", "parallel"), 40),
        name="adaln_mod",
    )(cvec, ada_w, ada_b.reshape(depth, 1, n))


def _inproj_kernel(x_ref, g_ref, mod_ref, w_ref, o_ref, h_scr, *, shift_idx, scale_idx):
    @pl.when(pl.program_id(1) == 0)
    def _():
        h = _rms_mod(x_ref[...], g_ref[...], mod_ref[0, scale_idx:scale_idx + 1, :],
                     mod_ref[0, shift_idx:shift_idx + 1, :])
        h_scr[...] = h.astype(BF16)

    o_ref[...] = jnp.dot(h_scr[...], w_ref[...], preferred_element_type=F32).astype(o_ref.dtype)


def norm_mod_matmul(x2d, g, mod, w, rows_per_mod, shift_idx, scale_idx, tm, tn, col0_blk=0):
    m, d = x2d.shape
    n = w.shape[1] - col0_blk * tn
    kern = functools.partial(_inproj_kernel, shift_idx=shift_idx, scale_idx=scale_idx)
    return pl.pallas_call(
        kern,
        out_shape=jax.ShapeDtypeStruct((m, n), BF16),
        grid=(m // tm, n // tn),
        in_specs=[
            pl.BlockSpec((tm, d), lambda i, j: (i, 0)),
            pl.BlockSpec((1, d), lambda i, j: (0, 0)),
            pl.BlockSpec((1, N_MOD, d), lambda i, j: ((i * tm) // rows_per_mod, 0, 0)),
            pl.BlockSpec((d, tn), lambda i, j: (0, j + col0_blk)),
        ],
        out_specs=pl.BlockSpec((tm, tn), lambda i, j: (i, j)),
        scratch_shapes=[pltpu.VMEM((tm, d), BF16)],
        compiler_params=_cparams(("parallel", "arbitrary"), 48),
        name="norm_mod_matmul",
    )(x2d, g.reshape(1, d), mod, w)


def _glu_kernel(x_ref, g_ref, mod_ref, wa_ref, wg_ref, ba_ref, bg_ref, o_ref, h_scr):
    @pl.when(pl.program_id(1) == 0)
    def _():
        h = _rms_mod(x_ref[...], g_ref[...], mod_ref[0, 1:2, :], mod_ref[0, 0:1, :])
        h_scr[...] = h.astype(BF16)

    h = h_scr[...]
    a = jnp.dot(h, wa_ref[...], preferred_element_type=F32) + ba_ref[...]
    gt = jnp.dot(h, wg_ref[...], preferred_element_type=F32) + bg_ref[...]
    o_ref[...] = (a * _sigmoid(gt)).astype(o_ref.dtype)


def norm_mod_glu(x2d, g, mod, w, b, rows_per_mod, tm, tn):
    m, d = x2d.shape
    half = w.shape[1] // 2
    nblk = half // tn
    b2 = b.reshape(1, 2 * half)
    return pl.pallas_call(
        _glu_kernel,
        out_shape=jax.ShapeDtypeStruct((m, half), BF16),
        grid=(m // tm, nblk),
        in_specs=[
            pl.BlockSpec((tm, d), lambda i, j: (i, 0)),
            pl.BlockSpec((1, d), lambda i, j: (0, 0)),
            pl.BlockSpec((1, N_MOD, d), lambda i, j: ((i * tm) // rows_per_mod, 0, 0)),
            pl.BlockSpec((d, tn), lambda i, j: (0, j)),
            pl.BlockSpec((d, tn), lambda i, j: (0, j + nblk)),
            pl.BlockSpec((1, tn), lambda i, j: (0, j)),
            pl.BlockSpec((1, tn), lambda i, j: (0, j + nblk)),
        ],
        out_specs=pl.BlockSpec((tm, tn), lambda i, j: (i, j)),
        scratch_shapes=[pltpu.VMEM((tm, d), BF16)],
        compiler_params=_cparams(("parallel", "arbitrary"), 48),
        name="norm_mod_glu",
    )(x2d, g.reshape(1, d), mod, w, w, b2, b2)


def _fourier_kernel(u_ref, cs_ref, cl_ref, sl_ref, o_ref, xc_scr, xs_scr, *, gc):
    @pl.when(pl.program_id(1) == 0)
    def _():
        for grp in range(N_FOURIER_GROUPS):
            r = jnp.dot(u_ref[:, grp * gc:(grp + 1) * gc], cs_ref[...], preferred_element_type=F32)
            xc_scr[:, grp * gc:(grp + 1) * gc] = r[:, :gc].astype(BF16)
            xs_scr[:, grp * gc:(grp + 1) * gc] = r[:, gc:].astype(BF16)

    acc = jnp.dot(cl_ref[...], xc_scr[...], preferred_element_type=F32)
    acc = acc - jnp.dot(sl_ref[...], xs_scr[...], preferred_element_type=F32)
    o_ref[...] = acc.astype(o_ref.dtype)


def _dft_tables(length, scale):
    k = lax.broadcasted_iota(I32, (length, length), 0)
    n = lax.broadcasted_iota(I32, (length, length), 1)
    ang = ((k * n) % length).astype(F32) * (2.0 * math.pi / length)
    return jnp.cos(ang) * scale, jnp.sin(ang) * scale


def fourier_mix(p, dims, tm):
    gc = dims.fw // N_FOURIER_GROUPS
    cc, sc = _dft_tables(gc, 1.0 / math.sqrt(dims.seq * gc))
    cs = jnp.concatenate([cc, sc], axis=1).astype(BF16)
    cl, sl = _dft_tables(dims.seq, 1.0)
    cl = cl.astype(BF16)
    sl = sl.astype(BF16)
    kern = functools.partial(_fourier_kernel, gc=gc)
    return pl.pallas_call(
        kern,
        out_shape=jax.ShapeDtypeStruct((dims.batch * dims.seq, dims.fw), BF16),
        grid=(dims.batch, dims.seq // tm),
        in_specs=[
            pl.BlockSpec((dims.seq, dims.fw), lambda b, m: (b, 0)),
            pl.BlockSpec((gc, 2 * gc), lambda b, m: (0, 0)),
            pl.BlockSpec((tm, dims.seq), lambda b, m: (m, 0)),
            pl.BlockSpec((tm, dims.seq), lambda b, m: (m, 0)),
        ],
        out_specs=pl.BlockSpec((tm, dims.fw), lambda b, m: (b * (dims.seq // tm) + m, 0)),
        scratch_shapes=[pltpu.VMEM((dims.seq, dims.fw), BF16), pltpu.VMEM((dims.seq, dims.fw), BF16)],
        compiler_params=_cparams(("parallel", "arbitrary"), 48),
        name="fourier_mix",
    )(p, cs, cl, sl)


def _na_kernel(q_ref, k_ref, v_ref, kc_ref, vc_ref, t_ref, o_ref, *, n_rows):
    r = pl.program_id(1)
    kr = min(NA_ROWS, n_rows)
    rs = jnp.clip(r - kr // 2, 0, n_rows - kr)
    row0 = pl.multiple_of(rs * GRID_W, GRID_W)
    nloc = kr * GRID_W
    lane = lax.broadcasted_iota(I32, (GRID_W, LANES), 1)
    low = lane < NA_HEAD_DIM
    contract_last = (((1,), (1,)), ((), ()))
    scale = NA_HEAD_DIM ** -0.5
    for pr in range(NA_HEADS // 2):
        cols = pl.ds(pr * LANES, LANES)
        qp = q_ref[:, cols] * scale
        kp = k_ref[pl.ds(row0, nloc), cols]
        vp = v_ref[pl.ds(row0, nloc), cols]
        kcp = kc_ref[:, cols]
        vcp = vc_ref[:, cols]
        outs = []
        for hh in range(2):
            qh = jnp.where(low if hh == 0 else jnp.logical_not(low), qp, jnp.zeros_like(qp))
            s = lax.dot_general(qh, kp, contract_last, preferred_element_type=F32)
            s = s + t_ref[0, 2 * pr + hh].astype(F32)
            sc = lax.dot_general(qh, kcp, contract_last, preferred_element_type=F32)
            m = jnp.maximum(jnp.max(s, axis=-1, keepdims=True), jnp.max(sc, axis=-1, keepdims=True))
            e = jnp.exp(s - m)
            ec = jnp.exp(sc - m)
            den = jnp.sum(e, axis=-1, keepdims=True) + jnp.sum(ec, axis=-1, keepdims=True)
            o = jnp.dot(e.astype(BF16), vp, preferred_element_type=F32)
            o = o + jnp.dot(ec.astype(BF16), vcp, preferred_element_type=F32)
            outs.append(o / den)
        o_ref[:, cols] = jnp.where(low, outs[0], outs[1]).astype(o_ref.dtype)


def _na_bias_table(rpb, n_rows):
    kr = min(NA_ROWS, n_rows)
    col = jnp.arange(GRID_W)
    col_start = jnp.clip(col - NA_COLS // 2, 0, GRID_W - NA_COLS)
    col_mask = (col[None, :] >= col_start[:, None]) & (col[None, :] < col_start[:, None] + NA_COLS)
    dc_idx = jnp.clip(col[None, :] - col[:, None] + NA_COLS - 1, 0, 2 * NA_COLS - 2)
    dr = jnp.arange(kr)[None, :] - jnp.arange(kr)[:, None] + NA_ROWS - 1
    t = rpb.astype(F32)[:, dr][..., dc_idx]
    t = jnp.where(col_mask[None, None, None], t, NEG_BIG)
    t = t.transpose(1, 0, 3, 2, 4).reshape(kr, NA_HEADS, GRID_W, kr * GRID_W)
    return t.astype(BF16)


def neighbourhood_attention(p, kvc, rpb, dims):
    n_rows = dims.seq // GRID_W
    kr = min(NA_ROWS, n_rows)
    table = _na_bias_table(rpb, n_rows)
    naw = dims.naw
    qb = dims.fw // naw
    kern = functools.partial(_na_kernel, n_rows=n_rows)

    def t_map(b, r):
        return (r - jnp.clip(r - kr // 2, 0, n_rows - kr), 0, 0, 0)

    return pl.pallas_call(
        kern,
        out_shape=jax.ShapeDtypeStruct((dims.batch * dims.seq, naw), BF16),
        grid=(dims.batch, n_rows),
        in_specs=[
            pl.BlockSpec((GRID_W, naw), lambda b, r: (b * n_rows + r, qb)),
            pl.BlockSpec((dims.seq, naw), lambda b, r: (b, qb + 1)),
            pl.BlockSpec((dims.seq, naw), lambda b, r: (b, qb + 2)),
            pl.BlockSpec((dims.ctx, naw), lambda b, r: (b, 0)),
            pl.BlockSpec((dims.ctx, naw), lambda b, r: (b, 1)),
            pl.BlockSpec((1, NA_HEADS, GRID_W, kr * GRID_W), t_map),
        ],
        out_specs=pl.BlockSpec((GRID_W, naw), lambda b, r: (b * n_rows + r, 0)),
        compiler_params=_cparams(("parallel", "arbitrary"), 48),
        name="neighbourhood_attention",
    )(p, p, p, kvc, kvc, table)


CONV_HALO = 16
CONV_RB = 128
CONV_CB = 256


def _conv_kernel(um_ref, up_ref, un_ref, w_ref, b_ref, lg_ref, lb_ref, o_ref, scr, y_scr, *, tl, n_l):
    l = pl.program_id(1)
    c = um_ref.shape[1]
    prev = jnp.where(l > 0, up_ref[...].astype(F32), 0.0)
    nxt = jnp.where(l < n_l - 1, un_ref[...].astype(F32), 0.0)
    scr[0:CONV_HALO, :] = prev
    scr[CONV_HALO:CONV_HALO + tl, :] = um_ref[...].astype(F32)
    scr[CONV_HALO + tl:2 * CONV_HALO + tl, :] = nxt
    base = CONV_HALO - CONV_K // 2
    for rb in range(tl // CONV_RB):
        for cb in range(c // CONV_CB):
            cols = pl.ds(cb * CONV_CB, CONV_CB)
            acc = jnp.zeros((CONV_RB, CONV_CB), F32)
            for k in range(CONV_K):
                acc = acc + scr[pl.ds(rb * CONV_RB + base + k, CONV_RB), cols] * w_ref[k:k + 1, cols]
            y_scr[pl.ds(rb * CONV_RB, CONV_RB), cols] = acc + b_ref[:, cols]
    y = y_scr[...]
    mu = jnp.mean(y, axis=-1, keepdims=True)
    yc = y - mu
    var = jnp.mean(yc * yc, axis=-1, keepdims=True)
    z = yc * lax.rsqrt(var + EPS) * lg_ref[...] + lb_ref[...]
    o_ref[...] = (z * _sigmoid(z)).astype(o_ref.dtype)


def conv_ln_swish(u, dw_w, dw_b, ln_g, ln_b, dims, tl):
    c = u.shape[1]
    n_l = dims.seq // tl
    hb = tl // CONV_HALO
    last_hblk = dims.batch * dims.seq // CONV_HALO - 1
    kern = functools.partial(_conv_kernel, tl=tl, n_l=n_l)
    return pl.pallas_call(
        kern,
        out_shape=jax.ShapeDtypeStruct(u.shape, BF16),
        grid=(dims.batch, n_l),
        in_specs=[
            pl.BlockSpec((tl, c), lambda b, l: (b * n_l + l, 0)),
            pl.BlockSpec((CONV_HALO, c), lambda b, l: (jnp.maximum((b * n_l + l) * hb - 1, 0), 0)),
            pl.BlockSpec((CONV_HALO, c), lambda b, l: (jnp.minimum((b * n_l + l + 1) * hb, last_hblk), 0)),
            pl.BlockSpec((CONV_K, c), lambda b, l: (0, 0)),
            pl.BlockSpec((1, c), lambda b, l: (0, 0)),
            pl.BlockSpec((1, c), lambda b, l: (0, 0)),
            pl.BlockSpec((1, c), lambda b, l: (0, 0)),
        ],
        out_specs=pl.BlockSpec((tl, c), lambda b, l: (b * n_l + l, 0)),
        scratch_shapes=[pltpu.VMEM((tl + 2 * CONV_HALO, c), F32), pltpu.VMEM((tl, c), F32)],
        compiler_params=_cparams(("parallel", "parallel"), 40),
        name="conv_ln_swish",
    )(u, u, u, dw_w, dw_b.reshape(1, c), ln_g.reshape(1, c), ln_b.reshape(1, c))


def _route(logits_t, rb):
    sel = [_sigmoid(logits_t[e:e + 1, :]) + rb[e:e + 1, :] for e in range(N_EXPERTS)]
    gs = []
    for g in range(N_GROUPS):
        v0, v1, v2, v3 = sel[4 * g:4 * g + 4]
        hi1, lo1 = jnp.maximum(v0, v1), jnp.minimum(v0, v1)
        hi2, lo2 = jnp.maximum(v2, v3), jnp.minimum(v2, v3)
        gs.append(jnp.maximum(hi1, hi2) + jnp.maximum(jnp.minimum(hi1, hi2), jnp.maximum(lo1, lo2)))
    best = gs[0]
    bg = jnp.zeros(best.shape, I32)
    for g in range(1, N_GROUPS):
        upd = gs[g] > best
        bg = jnp.where(upd, g, bg)
        best = jnp.where(upd, gs[g], best)
    v = []
    for i in range(EXPERTS_PER_GROUP):
        vi = sel[i]
        for g in range(1, N_GROUPS):
            vi = jnp.where(bg == g, sel[4 * g + i], vi)
        v.append(vi)
    picked = []
    for i in range(EXPERTS_PER_GROUP):
        rank = jnp.zeros(best.shape, I32)
        for j in range(EXPERTS_PER_GROUP):
            if j == i:
                continue
            ahead = (v[j] > v[i]) | ((v[j] == v[i]) & (j < i))
            rank = rank + ahead.astype(I32)
        picked.append(rank < 2)
    code = jnp.full(best.shape, len(PAIRS) - 1, I32)
    for idx in range(len(PAIRS) - 2, -1, -1):
        a, b = PAIRS[idx]
        code = jnp.where(picked[a] & picked[b], idx, code)
    return bg * len(PAIRS) + code


def _outproj_kernel(a_ref, b_ref, w_ref, bias_ref, x_ref, mod_ref, g_ref, rw1_ref, rw2_ref, rb_ref,
                    x1_ref, h2_ref, cls_ref, *, n_chunks):
    ka = a_ref.shape[1]
    tm = a_ref.shape[0]
    y = jnp.dot(a_ref[...], w_ref[0:ka, :], preferred_element_type=F32)
    y = y + jnp.dot(b_ref[...], w_ref[ka:, :], preferred_element_type=F32) + bias_ref[...]
    x1 = x_ref[...] + mod_ref[0, 2:3, :] * y
    x1_ref[...] = x1
    h2 = _rms_mod(x1, g_ref[...], mod_ref[0, 4:5, :], mod_ref[0, 3:4, :])
    for c in range(n_chunks):
        h2_ref[pl.ds(c, tm, stride=n_chunks), :] = h2[:, c * LANES:(c + 1) * LANES]
    hi = h2.astype(BF16)
    lo = (h2 - hi.astype(F32)).astype(BF16)
    s = jnp.dot(hi, rw1_ref[...], preferred_element_type=F32) + jnp.dot(lo, rw2_ref[...], preferred_element_type=F32)
    st = s.T
    logits_t = st[0:N_EXPERTS, :] + st[N_EXPERTS:2 * N_EXPERTS, :]
    cls_ref[...] = _route(logits_t, rb_ref[...])


def outproj_residual_route(a, a_col, b, b_col, w, bias, x2d, mod, ffn_g, router_w, router_b, rows_per_mod, tm):
    m, d = x2d.shape
    ka = kb = w.shape[0] // 2
    n_chunks = d // LANES
    rw_hi = router_w.astype(BF16)
    rw_lo = (router_w - rw_hi.astype(F32)).astype(BF16)
    pad = jnp.zeros((d, LANES - 2 * N_EXPERTS), BF16)
    rw1 = jnp.concatenate([rw_hi, rw_lo, pad], axis=1)
    rw2 = jnp.concatenate([rw_hi, jnp.zeros((d, N_EXPERTS), BF16), pad], axis=1)
    kern = functools.partial(_outproj_kernel, n_chunks=n_chunks)
    return pl.pallas_call(
        kern,
        out_shape=(
            jax.ShapeDtypeStruct((m, d), F32),
            jax.ShapeDtypeStruct((m * n_chunks, LANES), F32),
            jax.ShapeDtypeStruct((1, m), I32),
        ),
        grid=(m // tm,),
        in_specs=[
            pl.BlockSpec((tm, ka), lambda i: (i, a_col)),
            pl.BlockSpec((tm, kb), lambda i: (i, b_col)),
            pl.BlockSpec((ka + kb, d), lambda i: (0, 0)),
            pl.BlockSpec((1, d), lambda i: (0, 0)),
            pl.BlockSpec((tm, d), lambda i: (i, 0)),
            pl.BlockSpec((1, N_MOD, d), lambda i: ((i * tm) // rows_per_mod, 0, 0)),
            pl.BlockSpec((1, d), lambda i: (0, 0)),
            pl.BlockSpec((d, LANES), lambda i: (0, 0)),
            pl.BlockSpec((d, LANES), lambda i: (0, 0)),
            pl.BlockSpec((N_EXPERTS, 1), lambda i: (0, 0)),
        ],
        out_specs=(
            pl.BlockSpec((tm, d), lambda i: (i, 0)),
            pl.BlockSpec((tm * n_chunks, LANES), lambda i: (i, 0)),
            pl.BlockSpec((1, tm), lambda i: (0, i)),
        ),
        compiler_params=_cparams(("parallel",), 56),
        name="outproj_residual_route",
    )(a, b, w, bias.reshape(1, d), x2d, mod, ffn_g.reshape(1, d), rw1, rw2,
      router_b.astype(F32).reshape(N_EXPERTS, 1))


def _class_onehot(cls_ref):
    blk = cls_ref.shape[1]
    return (lax.broadcasted_iota(I32, (CLASS_PAD, blk), 0) == cls_ref[...]).astype(F32)


def _count_kernel(cls_ref, cnt_ref):
    @pl.when(pl.program_id(0) == 0)
    def _():
        cnt_ref[...] = jnp.zeros_like(cnt_ref)

    cnt_ref[...] += jnp.sum(_class_onehot(cls_ref), axis=1, keepdims=True)


def _dest_kernel(cls_ref, start_ref, dest_ref, carry_scr):
    @pl.when(pl.program_id(0) == 0)
    def _():
        carry_scr[...] = jnp.zeros_like(carry_scr)

    blk = cls_ref.shape[1]
    onehot = _class_onehot(cls_ref)
    tri = (lax.broadcasted_iota(I32, (blk, blk), 0) <= lax.broadcasted_iota(I32, (blk, blk), 1)).astype(BF16)
    cum = jnp.dot(onehot.astype(BF16), tri, preferred_element_type=F32)
    pos = cum - 1.0 + carry_scr[...] + start_ref[...]
    dest_ref[...] = jnp.sum(onehot * pos, axis=0, keepdims=True).astype(I32)
    carry_scr[...] += jnp.sum(onehot, axis=1, keepdims=True)


def sort_tokens(cls, tile):
    n = cls.shape[1]
    nb = n // SORT_BLK
    counts = pl.pallas_call(
        _count_kernel,
        out_shape=jax.ShapeDtypeStruct((CLASS_PAD, 1), F32),
        grid=(nb,),
        in_specs=[pl.BlockSpec((1, SORT_BLK), lambda j: (0, j))],
        out_specs=pl.BlockSpec((CLASS_PAD, 1), lambda j: (0, 0)),
        compiler_params=_cparams(("arbitrary",), 32),
        name="class_counts",
    )(cls)
    counts = counts.reshape(CLASS_PAD).astype(I32)
    padded = ((counts + tile - 1) // tile) * tile
    starts = (jnp.cumsum(padded) - padded).astype(F32).reshape(CLASS_PAD, 1)
    dest = pl.pallas_call(
        _dest_kernel,
        out_shape=jax.ShapeDtypeStruct((1, n), I32),
        grid=(nb,),
        in_specs=[pl.BlockSpec((1, SORT_BLK), lambda j: (0, j)),
                  pl.BlockSpec((CLASS_PAD, 1), lambda j: (0, 0))],
        out_specs=pl.BlockSpec((1, SORT_BLK), lambda j: (0, j)),
        scratch_shapes=[pltpu.VMEM((CLASS_PAD, 1), F32)],
        compiler_params=_cparams(("arbitrary",), 32),
        name="sorted_positions",
    )(cls, starts)
    return dest, counts


def _inverse_kernel(dest_ref, src_ref, *, n_tok, n_rows):
    def clear(k, carry):
        for u in range(SCALAR_UNROLL):
            src_ref[k * SCALAR_UNROLL + u] = 0
        return carry

    lax.fori_loop(0, n_rows // SCALAR_UNROLL, clear, 0)

    def scatter(k, carry):
        for u in range(SCALAR_UNROLL):
            t = k * SCALAR_UNROLL + u
            src_ref[dest_ref[t]] = t
        return carry

    lax.fori_loop(0, n_tok // SCALAR_UNROLL, scatter, 0)


def inverse_map(dest, n_rows):
    n_tok = dest.shape[0]
    kern = functools.partial(_inverse_kernel, n_tok=n_tok, n_rows=n_rows)
    return pl.pallas_call(
        kern,
        out_shape=jax.ShapeDtypeStruct((n_rows,), I32),
        in_specs=[pl.BlockSpec(memory_space=pltpu.SMEM)],
        out_specs=pl.BlockSpec(memory_space=pltpu.SMEM),
        name="inverse_map",
    )(dest)


def _start_row_gather(idx_ref, base, src_hbm, buf, sem, slot, tm, s):
    def body(k, carry):
        for u in range(SCALAR_UNROLL):
            r = k * SCALAR_UNROLL + u
            pltpu.make_async_copy(
                src_hbm.at[pl.ds(pl.multiple_of(idx_ref[base + r] * s, s), s), :],
                buf.at[slot, pl.ds(pl.multiple_of(r * s, s), s), :],
                sem.at[slot]).start()
        return carry

    lax.fori_loop(0, tm // SCALAR_UNROLL, body, 0)


def _wait_row_gather(src_hbm, buf, sem, slot, tm, s):
    pltpu.make_async_copy(src_hbm.at[pl.ds(0, tm * s), :], buf.at[slot], sem.at[slot]).wait()


def _load_slab_rows(ref, tm, n_chunks):
    return jnp.concatenate([ref[pl.ds(c, tm, stride=n_chunks), :] for c in range(n_chunks)], axis=-1)


def _moe_kernel(src_ref, e1_ref, e2_ref, valid_ref, h_hbm, wg1_ref, wg2_ref, wu1_ref, wu2_ref,
                wd1_ref, wd2_ref, rwt_ref, ys_ref, buf, sem, *, tm, n_chunks, n_tiles):
    i = pl.program_id(0)
    slot = i % 2

    @pl.when(i == 0)
    def _():
        _start_row_gather(src_ref, 0, h_hbm, buf, sem, 0, tm, n_chunks)

    nxt = jnp.minimum(i + 1, n_tiles - 1)

    @pl.when((i + 1 < n_tiles) & (valid_ref[nxt] == 1))
    def _():
        _start_row_gather(src_ref, nxt * tm, h_hbm, buf, sem, 1 - slot, tm, n_chunks)

    @pl.when(valid_ref[i] == 0)
    def _():
        ys_ref[...] = jnp.zeros_like(ys_ref)

    @pl.when(valid_ref[i] == 1)
    def _():
        _wait_row_gather(h_hbm, buf, sem, slot, tm, n_chunks)
        x = _load_slab_rows(buf.at[slot], tm, n_chunks).astype(BF16)
        xr = x.astype(F32)
        l1 = jnp.sum(xr * rwt_ref[pl.ds(e1_ref[i], 1), :], axis=-1, keepdims=True)
        l2 = jnp.sum(xr * rwt_ref[pl.ds(e2_ref[i], 1), :], axis=-1, keepdims=True)
        s1 = _sigmoid(l1)
        s2 = _sigmoid(l2)
        tot = s1 + s2

        def expert(wg_ref, wu_ref, weight):
            gt = jnp.dot(x, wg_ref[0], preferred_element_type=F32)
            up = jnp.dot(x, wu_ref[0], preferred_element_type=F32)
            return (gt * _sigmoid(gt) * up * weight).astype(BF16)

        a1 = expert(wg1_ref, wu1_ref, s1 / tot)
        a2 = expert(wg2_ref, wu2_ref, s2 / tot)
        y = jnp.dot(a1, wd1_ref[0], preferred_element_type=F32)
        y = y + jnp.dot(a2, wd2_ref[0], preferred_element_type=F32)
        for c in range(n_chunks):
            ys_ref[pl.ds(c, tm, stride=n_chunks), :] = y[:, c * LANES:(c + 1) * LANES]


def moe_ffn(src, tile_e1, tile_e2, tile_valid, h_rows, w_gate, w_up, w_down, router_w_t, d, tm):
    n_tiles = tile_e1.shape[0]
    n_chunks = d // LANES
    fe = w_gate.shape[2]
    kern = functools.partial(_moe_kernel, tm=tm, n_chunks=n_chunks, n_tiles=n_tiles)

    def w_in_spec(which):
        return pl.BlockSpec((1, d, fe), lambda i, src, e1, e2, valid: ((e1, e2)[which][i], 0, 0))

    def w_out_spec(which):
        return pl.BlockSpec((1, fe, d), lambda i, src, e1, e2, valid: ((e1, e2)[which][i], 0, 0))

    return pl.pallas_call(
        kern,
        out_shape=jax.ShapeDtypeStruct((n_tiles * tm * n_chunks, LANES), F32),
        grid_spec=pltpu.PrefetchScalarGridSpec(
            num_scalar_prefetch=4,
            grid=(n_tiles,),
            in_specs=[
                pl.BlockSpec(memory_space=pl.ANY),
                w_in_spec(0), w_in_spec(1), w_in_spec(0), w_in_spec(1), w_out_spec(0), w_out_spec(1),
                pl.BlockSpec((N_EXPERTS, d), lambda i, src, e1, e2, valid: (0, 0)),
            ],
            out_specs=pl.BlockSpec((tm * n_chunks, LANES), lambda i, src, e1, e2, valid: (i, 0)),
            scratch_shapes=[pltpu.VMEM((2, tm * n_chunks, LANES), F32), pltpu.SemaphoreType.DMA((2,))],
        ),
        compiler_params=_cparams(("arbitrary",), 56),
        name="moe_ffn",
    )(src, tile_e1, tile_e2, tile_valid, h_rows, w_gate, w_gate, w_up, w_up, w_down, w_down, router_w_t)


def _combine_kernel(dest_ref, ys_hbm, x1_ref, mod_ref, g_ref, o_ref, buf, sem, *, tm, s, n_tiles, final):
    i = pl.program_id(0)
    slot = i % 2

    @pl.when(i == 0)
    def _():
        _start_row_gather(dest_ref, 0, ys_hbm, buf, sem, 0, tm, s)

    @pl.when(i + 1 < n_tiles)
    def _():
        _start_row_gather(dest_ref, (i + 1) * tm, ys_hbm, buf, sem, 1 - slot, tm, s)

    _wait_row_gather(ys_hbm, buf, sem, slot, tm, s)
    y = _load_slab_rows(buf.at[slot], tm, s)
    x2 = x1_ref[...] + mod_ref[0, 5:6, :] * y
    if final:
        ms = jnp.mean(x2 * x2, axis=-1, keepdims=True)
        x2 = x2 * lax.rsqrt(ms + EPS) * g_ref[...]
    o_ref[...] = x2


def combine_residual(dest, ys, x1, mod, final_g, rows_per_mod, tm, final):
    m, d = x1.shape
    s = d // LANES
    n_tiles = m // tm
    kern = functools.partial(_combine_kernel, tm=tm, s=s, n_tiles=n_tiles, final=final)
    return pl.pallas_call(
        kern,
        out_shape=jax.ShapeDtypeStruct((m, d), F32),
        grid_spec=pltpu.PrefetchScalarGridSpec(
            num_scalar_prefetch=1,
            grid=(n_tiles,),
            in_specs=[
                pl.BlockSpec(memory_space=pl.ANY),
                pl.BlockSpec((tm, d), lambda i, dest: (i, 0)),
                pl.BlockSpec((1, N_MOD, d), lambda i, dest: ((i * tm) // rows_per_mod, 0, 0)),
                pl.BlockSpec((1, d), lambda i, dest: (0, 0)),
            ],
            out_specs=pl.BlockSpec((tm, d), lambda i, dest: (i, 0)),
            scratch_shapes=[pltpu.VMEM((2, tm * s, LANES), F32), pltpu.SemaphoreType.DMA((2,))],
        ),
        compiler_params=_cparams(("arbitrary",), 40),
        name="combine_residual",
    )(dest, ys, x1, mod, final_g.reshape(1, d))


def _tile_tables(counts, n_tiles, tm):
    tiles_per_class = (counts[:N_CLASSES] + tm - 1) // tm
    tile_end = jnp.cumsum(tiles_per_class)
    total = tile_end[-1]
    idx = jnp.arange(n_tiles, dtype=I32)
    valid = idx < total
    blk = jnp.where(valid, idx, total - 1)
    cls = jnp.minimum(jnp.sum((tile_end[None, :] <= blk[:, None]).astype(I32), axis=1), N_CLASSES - 1)
    pair = jnp.asarray(PAIRS, I32)
    grp = cls // len(PAIRS)
    e1 = grp * EXPERTS_PER_GROUP + pair[cls % len(PAIRS), 0]
    e2 = grp * EXPERTS_PER_GROUP + pair[cls % len(PAIRS), 1]
    return e1.astype(I32), e2.astype(I32), valid.astype(I32)


def grouped_moe_residual(x1, h2_rows, cls, mod, router_w, w_gate, w_up, w_down, final_g, rows_per_mod, final):
    m, d = x1.shape
    s = d // LANES
    n_tiles = m // MOE_TM + N_CLASSES
    dest, counts = sort_tokens(cls, MOE_TM)
    dest = dest.reshape(m)
    e1, e2, valid = _tile_tables(counts, n_tiles, MOE_TM)
    src = inverse_map(dest, n_tiles * MOE_TM)
    ys = moe_ffn(src, e1, e2, valid, h2_rows, w_gate, w_up, w_down, router_w.T.astype(F32), d, MOE_TM)
    return combine_residual(dest, ys, x1, mod, final_g, rows_per_mod, GATHER_TM, final)


def _forward(dims, x, c, ctx, c_ctx, ada_w, ada_b, mix_norm_g, ffn_norm_g, ev_w_in, ev_rpb, ev_w_out,
             od_w_in, od_b_in, od_dw_w, od_dw_b, od_ln_g, od_ln_b, od_w_out, od_b_out,
             router_w, router_b, moe_w_gate, moe_w_up, moe_w_down, final_norm_g, *, tiles):
    b, seq, d = dims.batch, dims.seq, dims.d
    n = b * seq
    depth = ada_w.shape[0]
    assert depth == 2, "layer 0 is the Fourier/attention mixer, layer 1 the Conformer mixer"
    x2d = x.reshape(n, d)
    ctx2d = ctx.reshape(b * dims.ctx, d)

    cvec = jnp.zeros((16, d), F32).at[:b].set(c).at[b].set(c_ctx)
    mod_all = adaln_mod(cvec, ada_w, ada_b, tiles["mod_tn"]).reshape(depth, 16, N_MOD, d)

    mod0 = mod_all[0, :b]
    modc0 = mod_all[0, b:b + 1]
    w_in = ev_w_in[0].astype(BF16)
    p = norm_mod_matmul(x2d, mix_norm_g[0], mod0, w_in, seq, 0, 1, tiles["in_tm"], dims.naw)
    kv_blk0 = (dims.fw + dims.naw) // dims.naw
    kvc = norm_mod_matmul(ctx2d, mix_norm_g[0], modc0, w_in, b * dims.ctx, 0, 1,
                          min(tiles["in_tm"], b * dims.ctx), dims.naw, col0_blk=kv_blk0)
    fo = fourier_mix(p, dims, tiles["four_tm"])
    ao = neighbourhood_attention(p, kvc, ev_rpb[0], dims)
    x1, h2_rows, cls = outproj_residual_route(
        fo, 0, ao, 0, ev_w_out[0].astype(BF16), jnp.zeros((d,), F32), x2d, mod0, ffn_norm_g[0],
        router_w, router_b, seq, tiles["out_tm"])
    x2 = grouped_moe_residual(x1, h2_rows, cls, mod0, router_w, moe_w_gate[0].astype(BF16),
                              moe_w_up[0].astype(BF16), moe_w_down[0].astype(BF16), final_norm_g, seq, False)

    mod1 = mod_all[1, :b]
    u = norm_mod_glu(x2, mix_norm_g[1], mod1, od_w_in[0].astype(BF16), od_b_in[0], seq,
                     tiles["in_tm"], tiles["glu_tn"])
    un = conv_ln_swish(u, od_dw_w[0], od_dw_b[0], od_ln_g[0], od_ln_b[0], dims, tiles["conv_tl"])
    x3, h4_rows, cls1 = outproj_residual_route(
        un, 0, un, 1, od_w_out[0].astype(BF16), od_b_out[0], x2, mod1, ffn_norm_g[1],
        router_w, router_b, seq, tiles["out_tm"])
    out = grouped_moe_residual(x3, h4_rows, cls1, mod1, router_w, moe_w_gate[1].astype(BF16),
                               moe_w_up[1].astype(BF16), moe_w_down[1].astype(BF16), final_norm_g, seq, True)
    return out.reshape(b, seq, d)


TILES = dict(mod_tn=1024, in_tm=512, glu_tn=512, four_tm=512, out_tm=256, conv_tl=256)


def kernel(x, c, ctx, c_ctx, ada_w, ada_b, mix_norm_g, ffn_norm_g, ev_w_in, ev_rpb, ev_w_out, od_w_in, od_b_in,
           od_dw_w, od_dw_b, od_ln_g, od_ln_b, od_w_out, od_b_out, router_w, router_b, moe_w_gate, moe_w_up,
           moe_w_down, final_norm_g):
    b, seq, d = x.shape
    dims = Dims(batch=b, seq=seq, d=d, ctx=ctx.shape[1], fw=ev_w_out.shape[1] - NA_HEADS * NA_HEAD_DIM,
                naw=NA_HEADS * NA_HEAD_DIM, fe=moe_w_gate.shape[3])
    return _forward(dims, x, c, ctx, c_ctx, ada_w, ada_b, mix_norm_g, ffn_norm_g, ev_w_in, ev_rpb, ev_w_out,
                    od_w_in, od_b_in, od_dw_w, od_dw_b, od_ln_g, od_ln_b, od_w_out, od_b_out,
                    router_w, router_b, moe_w_gate, moe_w_up, moe_w_down, final_norm_g, tiles=TILES)
```

```python
import functools
import math
from typing import NamedTuple

import jax
import jax.numpy as jnp
import numpy as np
from jax import lax
from jax.experimental import pallas as pl
from jax.experimental.pallas import tpu as pltpu

F32 = jnp.float32
BF16 = jnp.bfloat16
I32 = jnp.int32

LANES = 128
NEG_BIG = -1e30
EPS = 1e-6

N_MOD = 6
N_FOURIER_GROUPS = 4
NA_HEADS = 16
NA_HEAD_DIM = 64
NA_ROWS = 8
NA_COLS = 16
GRID_W = 64
CONV_K = 31
N_EXPERTS = 16
N_GROUPS = 4
EXPERTS_PER_GROUP = 4
PAIRS = ((0, 1), (0, 2), (0, 3), (1, 3), (1, 2), (2, 3))
N_CLASSES = N_GROUPS * len(PAIRS)
CLASS_PAD = 32
MOE_TM = 256
SORT_BLK = 512
GATHER_TM = 256
SCALAR_UNROLL = 8


class Dims(NamedTuple):
    batch: int
    seq: int
    d: int
    ctx: int
    fw: int
    naw: int
    fe: int


def _cparams(sem, vmem_mb):
    return pltpu.CompilerParams(dimension_semantics=sem, vmem_limit_bytes=vmem_mb << 20)


def _sigmoid(x):
    return 1.0 / (1.0 + jnp.exp(-x))


def _rms_mod(x, g, scale, shift):
    ms = jnp.mean(x * x, axis=-1, keepdims=True)
    return (x * lax.rsqrt(ms + EPS) * g) * (1.0 + scale) + shift


def _mod_kernel(c_ref, w_ref, b_ref, o_ref):
    c = c_ref[...]
    s = (c * _sigmoid(c)).astype(BF16)
    o_ref[0] = jnp.dot(s, w_ref[0].astype(BF16), preferred_element_type=F32) + b_ref[0]


def adaln_mod(cvec, ada_w, ada_b, tn):
    depth, d, n = ada_w.shape
    rows = cvec.shape[0]
    return pl.pallas_call(
        _mod_kernel,
        out_shape=jax.ShapeDtypeStruct((depth, rows, n), F32),
        grid=(depth, n // tn),
        in_specs=[
            pl.BlockSpec((rows, d), lambda l, j: (0, 0)),
            pl.BlockSpec((1, d, tn), lambda l, j: (l, 0, j)),
            pl.BlockSpec((1, 1, tn), lambda l, j: (l, 0, j)),
        ],
        out_specs=pl.BlockSpec((1, rows, tn), lambda l, j: (l, 0, j)),
        compiler_params=_cparams(("parallel", "parallel"), 40),
        name="adaln_mod",
    )(cvec, ada_w, ada_b.reshape(depth, 1, n))


def _inproj_kernel(x_ref, g_ref, mod_ref, w_ref, o_ref, h_scr, *, shift_idx, scale_idx):
    @pl.when(pl.program_id(1) == 0)
    def _():
        h = _rms_mod(x_ref[...], g_ref[...], mod_ref[0, scale_idx:scale_idx + 1, :],
                     mod_ref[0, shift_idx:shift_idx + 1, :])
        h_scr[...] = h.astype(BF16)

    o_ref[...] = jnp.dot(h_scr[...], w_ref[...], preferred_element_type=F32).astype(o_ref.dtype)


def norm_mod_matmul(x2d, g, mod, w, rows_per_mod, shift_idx, scale_idx, tm, tn, col0_blk=0):
    m, d = x2d.shape
    n = w.shape[1] - col0_blk * tn
    kern = functools.partial(_inproj_kernel, shift_idx=shift_idx, scale_idx=scale_idx)
    return pl.pallas_call(
        kern,
        out_shape=jax.ShapeDtypeStruct((m, n), BF16),
        grid=(m // tm, n // tn),
        in_specs=[
            pl.BlockSpec((tm, d), lambda i, j: (i, 0)),
            pl.BlockSpec((1, d), lambda i, j: (0, 0)),
            pl.BlockSpec((1, N_MOD, d), lambda i, j: ((i * tm) // rows_per_mod, 0, 0)),
            pl.BlockSpec((d, tn), lambda i, j: (0, j + col0_blk)),
        ],
        out_specs=pl.BlockSpec((tm, tn), lambda i, j: (i, j)),
        scratch_shapes=[pltpu.VMEM((tm, d), BF16)],
        compiler_params=_cparams(("parallel", "arbitrary"), 48),
        name="norm_mod_matmul",
    )(x2d, g.reshape(1, d), mod, w)


def _cast_kernel(x_ref, o_ref):
    o_ref[...] = x_ref[...].astype(o_ref.dtype)


def cast_experts_bf16(w, layer):
    _, e, k, n = w.shape
    return pl.pallas_call(
        _cast_kernel,
        out_shape=jax.ShapeDtypeStruct((e, k, n), BF16),
        grid=(e,),
        in_specs=[pl.BlockSpec((None, 1, k, n), lambda i: (layer, i, 0, 0))],
        out_specs=pl.BlockSpec((1, k, n), lambda i: (i, 0, 0)),
        compiler_params=_cparams(("parallel",), 32),
        name="cast_experts_bf16",
    )(w)


def _fourier_kernel(u_ref, cs_ref, cl_ref, sl_ref, o_ref, xc_scr, xs_scr, *, gc):
    @pl.when(pl.program_id(1) == 0)
    def _():
        for grp in range(N_FOURIER_GROUPS):
            r = jnp.dot(u_ref[:, grp * gc:(grp + 1) * gc], cs_ref[...], preferred_element_type=F32)
            xc_scr[:, grp * gc:(grp + 1) * gc] = r[:, :gc].astype(BF16)
            xs_scr[:, grp * gc:(grp + 1) * gc] = r[:, gc:].astype(BF16)

    acc = jnp.dot(cl_ref[...], xc_scr[...], preferred_element_type=F32)
    acc = acc - jnp.dot(sl_ref[...], xs_scr[...], preferred_element_type=F32)
    o_ref[...] = acc.astype(o_ref.dtype)


def _dft_tables(length, scale):
    kn = np.outer(np.arange(length), np.arange(length)) % length
    ang = kn.astype(np.float64) * (2.0 * math.pi / length)
    return (np.cos(ang) * scale).astype(np.float32), (np.sin(ang) * scale).astype(np.float32)


def fourier_mix(p, dims, tm):
    gc = dims.fw // N_FOURIER_GROUPS
    cc, sc = _dft_tables(gc, 1.0 / math.sqrt(dims.seq * gc))
    cs = jnp.asarray(np.concatenate([cc, sc], axis=1).astype(BF16))
    cl, sl = (jnp.asarray(t.astype(BF16)) for t in _dft_tables(dims.seq, 1.0))
    kern = functools.partial(_fourier_kernel, gc=gc)
    return pl.pallas_call(
        kern,
        out_shape=jax.ShapeDtypeStruct((dims.batch * dims.seq, dims.fw), BF16),
        grid=(dims.batch, dims.seq // tm),
        in_specs=[
            pl.BlockSpec((dims.seq, dims.fw), lambda b, m: (b, 0)),
            pl.BlockSpec((gc, 2 * gc), lambda b, m: (0, 0)),
            pl.BlockSpec((tm, dims.seq), lambda b, m: (m, 0)),
            pl.BlockSpec((tm, dims.seq), lambda b, m: (m, 0)),
        ],
        out_specs=pl.BlockSpec((tm, dims.fw), lambda b, m: (b * (dims.seq // tm) + m, 0)),
        scratch_shapes=[pltpu.VMEM((dims.seq, dims.fw), BF16), pltpu.VMEM((dims.seq, dims.fw), BF16)],
        compiler_params=_cparams(("parallel", "arbitrary"), 48),
        name="fourier_mix",
    )(p, cs, cl, sl)


NA_QROWS = 2


def _na_geometry(n_rows):
    kr = min(NA_ROWS, n_rows)
    n_union = kr + NA_QROWS - 1
    steps = []
    for rp in range(n_rows // NA_QROWS):
        rows = [rp * NA_QROWS + a for a in range(NA_QROWS)]
        starts = [min(max(r - kr // 2, 0), n_rows - kr) for r in rows]
        u = min(starts[0], n_rows - n_union)
        steps.append((u, tuple((r - u, rs - u) for r, rs in zip(rows, starts))))
    variants = sorted(set(v for _, v in steps))
    return kr, n_union, [u for u, _ in steps], [variants.index(v) for _, v in steps], variants


def _na_kernel(var_ref, ustart_ref, q_ref, k_ref, v_ref, kc_ref, vc_ref, t_ref, o_ref, *, n_union):
    del var_ref
    rp = pl.program_id(1)
    row0 = pl.multiple_of(ustart_ref[rp] * GRID_W, GRID_W)
    nloc = n_union * GRID_W
    low = lax.broadcasted_iota(I32, (GRID_W, LANES), 1) < NA_HEAD_DIM
    nt = (((1,), (1,)), ((), ()))
    tn = (((0,), (0,)), ((), ()))
    scale = NA_HEAD_DIM ** -0.5
    for pr in range(NA_HEADS // 2):
        cols = pl.ds(pr * LANES, LANES)
        q2 = q_ref[:, cols] * scale
        parts = []
        for a in range(NA_QROWS):
            qa = q2[a * GRID_W:(a + 1) * GRID_W]
            zero = jnp.zeros_like(qa)
            parts += [jnp.where(low, qa, zero), jnp.where(low, zero, qa)]
        qm = jnp.concatenate(parts, axis=0)
        kp = k_ref[pl.ds(row0, nloc), cols]
        vp = v_ref[pl.ds(row0, nloc), cols]
        s = lax.dot_general(kp, qm, nt, preferred_element_type=F32) + t_ref[0, pr].astype(F32)
        sc = lax.dot_general(kc_ref[:, cols], qm, nt, preferred_element_type=F32)
        m = jnp.maximum(jnp.max(s, axis=0, keepdims=True), jnp.max(sc, axis=0, keepdims=True))
        e = jnp.exp(s - m)
        ec = jnp.exp(sc - m)
        den = jnp.sum(e, axis=0, keepdims=True) + jnp.sum(ec, axis=0, keepdims=True)
        ot = lax.dot_general(vp, e.astype(BF16), tn, preferred_element_type=F32)
        ot = ot + lax.dot_general(vc_ref[:, cols], ec.astype(BF16), tn, preferred_element_type=F32)
        o = (ot / den).T
        for a in range(NA_QROWS):
            base = a * 2 * GRID_W
            o_ref[a * GRID_W:(a + 1) * GRID_W, cols] = jnp.where(
                low, o[base:base + GRID_W], o[base + GRID_W:base + 2 * GRID_W]).astype(o_ref.dtype)


def _na_bias_table(rpb, n_rows):
    kr, n_union, _, _, variants = _na_geometry(n_rows)
    col = jnp.arange(GRID_W)
    col_start = jnp.clip(col - NA_COLS // 2, 0, GRID_W - NA_COLS)
    col_mask = (col[None, :] >= col_start[:, None]) & (col[None, :] < col_start[:, None] + NA_COLS)
    dc_idx = jnp.clip(col[None, :] - col[:, None] + NA_COLS - 1, 0, 2 * NA_COLS - 2)
    colbias = jnp.where(col_mask[None, None], rpb.astype(F32)[:, :, dc_idx], NEG_BIG)
    colbias = colbias.transpose(0, 1, 3, 2).astype(BF16)
    masked = jnp.full((NA_HEADS, GRID_W, GRID_W), NEG_BIG, BF16)
    out = []
    for var in variants:
        per_row = []
        for r_off, rs_off in var:
            blocks = [colbias[:, jj - r_off + NA_ROWS - 1] if rs_off <= jj < rs_off + kr else masked
                      for jj in range(n_union)]
            per_row.append(jnp.concatenate(blocks, axis=1))
        t = jnp.stack(per_row, axis=0)
        t = t.reshape(NA_QROWS, NA_HEADS // 2, 2, n_union * GRID_W, GRID_W)
        out.append(t.transpose(1, 3, 0, 2, 4).reshape(NA_HEADS // 2, n_union * GRID_W, NA_QROWS * 2 * GRID_W))
    return jnp.stack(out, axis=0)


def neighbourhood_attention(p, kvc, rpb, dims):
    n_rows = dims.seq // GRID_W
    _, n_union, ustarts, var_of_step, _ = _na_geometry(n_rows)
    n_steps = n_rows // NA_QROWS
    table = _na_bias_table(rpb, n_rows)
    naw = dims.naw
    qb = dims.fw // naw
    tq = NA_QROWS * GRID_W
    kern = functools.partial(_na_kernel, n_union=n_union)
    return pl.pallas_call(
        kern,
        out_shape=jax.ShapeDtypeStruct((dims.batch * dims.seq, naw), BF16),
        grid_spec=pltpu.PrefetchScalarGridSpec(
            num_scalar_prefetch=2,
            grid=(dims.batch, n_steps),
            in_specs=[
                pl.BlockSpec((tq, naw), lambda b, r, var, us: (b * n_steps + r, qb)),
                pl.BlockSpec((dims.seq, naw), lambda b, r, var, us: (b, qb + 1)),
                pl.BlockSpec((dims.seq, naw), lambda b, r, var, us: (b, qb + 2)),
                pl.BlockSpec((dims.ctx, naw), lambda b, r, var, us: (b, 0)),
                pl.BlockSpec((dims.ctx, naw), lambda b, r, var, us: (b, 1)),
                pl.BlockSpec((1, NA_HEADS // 2, n_union * GRID_W, 2 * tq), lambda b, r, var, us: (var[r], 0, 0, 0)),
            ],
            out_specs=pl.BlockSpec((tq, naw), lambda b, r, var, us: (b * n_steps + r, 0)),
        ),
        compiler_params=_cparams(("parallel", "arbitrary"), 48),
        name="neighbourhood_attention",
    )(jnp.asarray(var_of_step, I32), jnp.asarray(ustarts, I32), p, p, p, kvc, kvc, table)


CONV_HALO = 16
CONV_RB = 128
SUBLANES = 8


def _glu_conv_kernel(h_ref, wa_ref, wg_ref, ba_ref, bg_ref, dw_ref, db_ref, o_ref, scr, *, seq):
    h = h_ref[...]
    a = jnp.dot(h, wa_ref[...], preferred_element_type=F32) + ba_ref[...]
    gt = jnp.dot(h, wg_ref[...], preferred_element_type=F32) + bg_ref[...]
    tn = a.shape[1]
    scr[0:CONV_HALO, :] = jnp.zeros((CONV_HALO, tn), F32)
    scr[CONV_HALO + seq:2 * CONV_HALO + seq, :] = jnp.zeros((CONV_HALO, tn), F32)
    scr[CONV_HALO:CONV_HALO + seq, :] = a * _sigmoid(gt)
    base = CONV_HALO - CONV_K // 2
    n_shift = (CONV_K + base + SUBLANES - 1) // SUBLANES
    for rb in range(seq // CONV_RB):
        for cb in range(tn // LANES):
            cols = pl.ds(cb * LANES, LANES)
            acc = db_ref[:, cols]
            for ph in range(SUBLANES):
                part = None
                for st in range(n_shift):
                    k = st * SUBLANES + ph - base
                    if 0 <= k < CONV_K:
                        rows = pl.ds(rb * CONV_RB + st * SUBLANES, CONV_RB + SUBLANES)
                        term = scr[rows, cols] * dw_ref[k:k + 1, cols]
                        part = term if part is None else part + term
                acc = acc + part[ph:ph + CONV_RB]
            o_ref[pl.ds(rb * CONV_RB, CONV_RB), cols] = acc.astype(o_ref.dtype)


def glu_conv(h, w, b, dw_w, dw_b, dims, tn):
    d = h.shape[1]
    half = w.shape[1] // 2
    nblk = half // tn
    b2 = b.reshape(1, 2 * half)
    kern = functools.partial(_glu_conv_kernel, seq=dims.seq)
    return pl.pallas_call(
        kern,
        out_shape=jax.ShapeDtypeStruct((dims.batch * dims.seq, half), BF16),
        grid=(dims.batch, nblk),
        in_specs=[
            pl.BlockSpec((dims.seq, d), lambda s, j: (s, 0)),
            pl.BlockSpec((d, tn), lambda s, j: (0, j)),
            pl.BlockSpec((d, tn), lambda s, j: (0, j + nblk)),
            pl.BlockSpec((1, tn), lambda s, j: (0, j)),
            pl.BlockSpec((1, tn), lambda s, j: (0, j + nblk)),
            pl.BlockSpec((CONV_K, tn), lambda s, j: (0, j)),
            pl.BlockSpec((1, tn), lambda s, j: (0, j)),
        ],
        out_specs=pl.BlockSpec((dims.seq, tn), lambda s, j: (s, j)),
        scratch_shapes=[pltpu.VMEM((dims.seq + 2 * CONV_HALO, tn), F32)],
        compiler_params=_cparams(("parallel", "arbitrary"), 48),
        name="glu_conv",
    )(h, w, w, b2, b2, dw_w, dw_b.reshape(1, half))


def _route(logits_t, rb):
    sel = [_sigmoid(logits_t[e:e + 1, :]) + rb[e:e + 1, :] for e in range(N_EXPERTS)]
    gs = []
    for g in range(N_GROUPS):
        v0, v1, v2, v3 = sel[4 * g:4 * g + 4]
        hi1, lo1 = jnp.maximum(v0, v1), jnp.minimum(v0, v1)
        hi2, lo2 = jnp.maximum(v2, v3), jnp.minimum(v2, v3)
        gs.append(jnp.maximum(hi1, hi2) + jnp.maximum(jnp.minimum(hi1, hi2), jnp.maximum(lo1, lo2)))
    best = gs[0]
    bg = jnp.zeros(best.shape, I32)
    for g in range(1, N_GROUPS):
        upd = gs[g] > best
        bg = jnp.where(upd, g, bg)
        best = jnp.where(upd, gs[g], best)
    v = []
    for i in range(EXPERTS_PER_GROUP):
        vi = sel[i]
        for g in range(1, N_GROUPS):
            vi = jnp.where(bg == g, sel[4 * g + i], vi)
        v.append(vi)
    picked = []
    for i in range(EXPERTS_PER_GROUP):
        rank = jnp.zeros(best.shape, I32)
        for j in range(EXPERTS_PER_GROUP):
            if j == i:
                continue
            ahead = (v[j] > v[i]) | ((v[j] == v[i]) & (j < i))
            rank = rank + ahead.astype(I32)
        picked.append(rank < 2)
    code = jnp.full(best.shape, len(PAIRS) - 1, I32)
    for idx in range(len(PAIRS) - 2, -1, -1):
        a, b = PAIRS[idx]
        code = jnp.where(picked[a] & picked[b], idx, code)
    return bg * len(PAIRS) + code


def _outproj_kernel(a_ref, b_ref, lng_ref, lnb_ref, w_ref, bias_ref, x_ref, mod_ref, g_ref, rw1_ref, rw2_ref,
                    rb_ref, x1_ref, h2_ref, cls_ref, *, n_chunks, ln_swish):
    ka = a_ref.shape[1]
    tm = a_ref.shape[0]
    if ln_swish:
        t = jnp.concatenate([a_ref[...], b_ref[...]], axis=-1).astype(F32)
        mu = jnp.mean(t, axis=-1, keepdims=True)
        tc = t - mu
        var = jnp.mean(tc * tc, axis=-1, keepdims=True)
        z = tc * lax.rsqrt(var + EPS) * lng_ref[...] + lnb_ref[...]
        z = (z * _sigmoid(z)).astype(BF16)
        y = jnp.dot(z, w_ref[...], preferred_element_type=F32) + bias_ref[...]
    else:
        y = jnp.dot(a_ref[...], w_ref[0:ka, :], preferred_element_type=F32)
        y = y + jnp.dot(b_ref[...], w_ref[ka:, :], preferred_element_type=F32) + bias_ref[...]
    x1 = x_ref[...] + mod_ref[0, 2:3, :] * y
    x1_ref[...] = x1
    h2 = _rms_mod(x1, g_ref[...], mod_ref[0, 4:5, :], mod_ref[0, 3:4, :])
    for c in range(n_chunks):
        h2_ref[pl.ds(c, tm, stride=n_chunks), :] = h2[:, c * LANES:(c + 1) * LANES]
    hi = h2.astype(BF16)
    lo = (h2 - hi.astype(F32)).astype(BF16)
    s = jnp.dot(hi, rw1_ref[...], preferred_element_type=F32) + jnp.dot(lo, rw2_ref[...], preferred_element_type=F32)
    st = s.T
    logits_t = st[0:N_EXPERTS, :] + st[N_EXPERTS:2 * N_EXPERTS, :]
    cls_ref[...] = _route(logits_t, rb_ref[...])


def outproj_residual_route(a, a_col, b, b_col, ln_g, ln_b, w, bias, x2d, mod, ffn_g, router_w, router_b,
                           rows_per_mod, tm, ln_swish):
    m, d = x2d.shape
    ka = kb = w.shape[0] // 2
    kin = ka + kb
    n_chunks = d // LANES
    rw_hi = router_w.astype(BF16)
    rw_lo = (router_w - rw_hi.astype(F32)).astype(BF16)
    pad = jnp.zeros((d, LANES - 2 * N_EXPERTS), BF16)
    rw1 = jnp.concatenate([rw_hi, rw_lo, pad], axis=1)
    rw2 = jnp.concatenate([rw_hi, jnp.zeros((d, N_EXPERTS), BF16), pad], axis=1)
    kern = functools.partial(_outproj_kernel, n_chunks=n_chunks, ln_swish=ln_swish)
    return pl.pallas_call(
        kern,
        out_shape=(
            jax.ShapeDtypeStruct((m, d), F32),
            jax.ShapeDtypeStruct((m * n_chunks, LANES), F32),
            jax.ShapeDtypeStruct((1, m), I32),
        ),
        grid=(m // tm,),
        in_specs=[
            pl.BlockSpec((tm, ka), lambda i: (i, a_col)),
            pl.BlockSpec((tm, kb), lambda i: (i, b_col)),
            pl.BlockSpec((1, kin), lambda i: (0, 0)),
            pl.BlockSpec((1, kin), lambda i: (0, 0)),
            pl.BlockSpec((kin, d), lambda i: (0, 0)),
            pl.BlockSpec((1, d), lambda i: (0, 0)),
            pl.BlockSpec((tm, d), lambda i: (i, 0)),
            pl.BlockSpec((1, N_MOD, d), lambda i: ((i * tm) // rows_per_mod, 0, 0)),
            pl.BlockSpec((1, d), lambda i: (0, 0)),
            pl.BlockSpec((d, LANES), lambda i: (0, 0)),
            pl.BlockSpec((d, LANES), lambda i: (0, 0)),
            pl.BlockSpec((N_EXPERTS, 1), lambda i: (0, 0)),
        ],
        out_specs=(
            pl.BlockSpec((tm, d), lambda i: (i, 0)),
            pl.BlockSpec((tm * n_chunks, LANES), lambda i: (i, 0)),
            pl.BlockSpec((1, tm), lambda i: (0, i)),
        ),
        compiler_params=_cparams(("parallel",), 56),
        name="outproj_residual_route",
    )(a, b, ln_g.reshape(1, kin), ln_b.reshape(1, kin), w, bias.reshape(1, d), x2d, mod, ffn_g.reshape(1, d),
      rw1, rw2, router_b.astype(F32).reshape(N_EXPERTS, 1))


def _class_onehot(cls_ref):
    blk = cls_ref.shape[1]
    return (lax.broadcasted_iota(I32, (CLASS_PAD, blk), 0) == cls_ref[...]).astype(F32)


def _count_kernel(cls_ref, cnt_ref):
    @pl.when(pl.program_id(0) == 0)
    def _():
        cnt_ref[...] = jnp.zeros_like(cnt_ref)

    cnt_ref[...] += jnp.sum(_class_onehot(cls_ref), axis=1, keepdims=True)


def _dest_kernel(cls_ref, start_ref, dest_ref, carry_scr):
    @pl.when(pl.program_id(0) == 0)
    def _():
        carry_scr[...] = jnp.zeros_like(carry_scr)

    blk = cls_ref.shape[1]
    onehot = _class_onehot(cls_ref)
    tri = (lax.broadcasted_iota(I32, (blk, blk), 0) <= lax.broadcasted_iota(I32, (blk, blk), 1)).astype(BF16)
    cum = jnp.dot(onehot.astype(BF16), tri, preferred_element_type=F32)
    pos = cum - 1.0 + carry_scr[...] + start_ref[...]
    dest_ref[...] = jnp.sum(onehot * pos, axis=0, keepdims=True).astype(I32)
    carry_scr[...] += jnp.sum(onehot, axis=1, keepdims=True)


def sort_tokens(cls, tile):
    n = cls.shape[1]
    nb = n // SORT_BLK
    counts = pl.pallas_call(
        _count_kernel,
        out_shape=jax.ShapeDtypeStruct((CLASS_PAD, 1), F32),
        grid=(nb,),
        in_specs=[pl.BlockSpec((1, SORT_BLK), lambda j: (0, j))],
        out_specs=pl.BlockSpec((CLASS_PAD, 1), lambda j: (0, 0)),
        compiler_params=_cparams(("arbitrary",), 32),
        name="class_counts",
    )(cls)
    counts = counts.reshape(CLASS_PAD).astype(I32)
    padded = ((counts + tile - 1) // tile) * tile
    starts = (jnp.cumsum(padded) - padded).astype(F32).reshape(CLASS_PAD, 1)
    dest = pl.pallas_call(
        _dest_kernel,
        out_shape=jax.ShapeDtypeStruct((1, n), I32),
        grid=(nb,),
        in_specs=[pl.BlockSpec((1, SORT_BLK), lambda j: (0, j)),
                  pl.BlockSpec((CLASS_PAD, 1), lambda j: (0, 0))],
        out_specs=pl.BlockSpec((1, SORT_BLK), lambda j: (0, j)),
        scratch_shapes=[pltpu.VMEM((CLASS_PAD, 1), F32)],
        compiler_params=_cparams(("arbitrary",), 32),
        name="sorted_positions",
    )(cls, starts)
    return dest, counts


def _inverse_kernel(dest_ref, src_ref, *, n_tok, n_rows):
    def clear(k, carry):
        for u in range(SCALAR_UNROLL):
            src_ref[k * SCALAR_UNROLL + u] = 0
        return carry

    lax.fori_loop(0, n_rows // SCALAR_UNROLL, clear, 0)

    def scatter(k, carry):
        for u in range(SCALAR_UNROLL):
            t = k * SCALAR_UNROLL + u
            src_ref[dest_ref[t]] = t
        return carry

    lax.fori_loop(0, n_tok // SCALAR_UNROLL, scatter, 0)


def inverse_map(dest, n_rows):
    n_tok = dest.shape[0]
    kern = functools.partial(_inverse_kernel, n_tok=n_tok, n_rows=n_rows)
    return pl.pallas_call(
        kern,
        out_shape=jax.ShapeDtypeStruct((n_rows,), I32),
        in_specs=[pl.BlockSpec(memory_space=pltpu.SMEM)],
        out_specs=pl.BlockSpec(memory_space=pltpu.SMEM),
        name="inverse_map",
    )(dest)


def _start_row_gather(idx_ref, base, src_hbm, buf, sem, slot, tm, s):
    def body(k, carry):
        for u in range(SCALAR_UNROLL):
            r = k * SCALAR_UNROLL + u
            pltpu.make_async_copy(
                src_hbm.at[pl.ds(pl.multiple_of(idx_ref[base + r] * s, s), s), :],
                buf.at[slot, pl.ds(pl.multiple_of(r * s, s), s), :],
                sem.at[slot]).start()
        return carry

    lax.fori_loop(0, tm // SCALAR_UNROLL, body, 0)


def _wait_row_gather(src_hbm, buf, sem, slot, tm, s):
    pltpu.make_async_copy(src_hbm.at[pl.ds(0, tm * s), :], buf.at[slot], sem.at[slot]).wait()


def _load_slab_rows(ref, tm, n_chunks):
    return jnp.concatenate([ref[pl.ds(c, tm, stride=n_chunks), :] for c in range(n_chunks)], axis=-1)


def _moe_kernel(src_ref, e1_ref, e2_ref, valid_ref, h_hbm, wg1_ref, wg2_ref, wu1_ref, wu2_ref,
                wd1_ref, wd2_ref, rwt_ref, ys_ref, buf, sem, *, tm, n_chunks, n_tiles):
    i = pl.program_id(0)
    slot = i % 2

    @pl.when(i == 0)
    def _():
        _start_row_gather(src_ref, 0, h_hbm, buf, sem, 0, tm, n_chunks)

    nxt = jnp.minimum(i + 1, n_tiles - 1)

    @pl.when((i + 1 < n_tiles) & (valid_ref[nxt] == 1))
    def _():
        _start_row_gather(src_ref, nxt * tm, h_hbm, buf, sem, 1 - slot, tm, n_chunks)

    @pl.when(valid_ref[i] == 0)
    def _():
        ys_ref[...] = jnp.zeros_like(ys_ref)

    @pl.when(valid_ref[i] == 1)
    def _():
        _wait_row_gather(h_hbm, buf, sem, slot, tm, n_chunks)
        x = _load_slab_rows(buf.at[slot], tm, n_chunks).astype(BF16)
        xr = x.astype(F32)
        l1 = jnp.sum(xr * rwt_ref[pl.ds(e1_ref[i], 1), :], axis=-1, keepdims=True)
        l2 = jnp.sum(xr * rwt_ref[pl.ds(e2_ref[i], 1), :], axis=-1, keepdims=True)
        s1 = _sigmoid(l1)
        s2 = _sigmoid(l2)
        tot = s1 + s2

        def expert(wg_ref, wu_ref, weight):
            gt = jnp.dot(x, wg_ref[0], preferred_element_type=F32)
            up = jnp.dot(x, wu_ref[0], preferred_element_type=F32)
            return (gt * _sigmoid(gt) * up * weight).astype(BF16)

        a1 = expert(wg1_ref, wu1_ref, s1 / tot)
        a2 = expert(wg2_ref, wu2_ref, s2 / tot)
        y = jnp.dot(a1, wd1_ref[0], preferred_element_type=F32)
        y = y + jnp.dot(a2, wd2_ref[0], preferred_element_type=F32)
        for c in range(n_chunks):
            ys_ref[pl.ds(c, tm, stride=n_chunks), :] = y[:, c * LANES:(c + 1) * LANES]


def moe_ffn(src, tile_e1, tile_e2, tile_valid, h_rows, w_gate, w_up, w_down, router_w_t, d, tm):
    n_tiles = tile_e1.shape[0]
    n_chunks = d // LANES
    fe = w_gate.shape[2]
    kern = functools.partial(_moe_kernel, tm=tm, n_chunks=n_chunks, n_tiles=n_tiles)

    def w_in_spec(which):
        return pl.BlockSpec((1, d, fe), lambda i, src, e1, e2, valid: ((e1, e2)[which][i], 0, 0))

    def w_out_spec(which):
        return pl.BlockSpec((1, fe, d), lambda i, src, e1, e2, valid: ((e1, e2)[which][i], 0, 0))

    return pl.pallas_call(
        kern,
        out_shape=jax.ShapeDtypeStruct((n_tiles * tm * n_chunks, LANES), F32),
        grid_spec=pltpu.PrefetchScalarGridSpec(
            num_scalar_prefetch=4,
            grid=(n_tiles,),
            in_specs=[
                pl.BlockSpec(memory_space=pl.ANY),
                w_in_spec(0), w_in_spec(1), w_in_spec(0), w_in_spec(1), w_out_spec(0), w_out_spec(1),
                pl.BlockSpec((N_EXPERTS, d), lambda i, src, e1, e2, valid: (0, 0)),
            ],
            out_specs=pl.BlockSpec((tm * n_chunks, LANES), lambda i, src, e1, e2, valid: (i, 0)),
            scratch_shapes=[pltpu.VMEM((2, tm * n_chunks, LANES), F32), pltpu.SemaphoreType.DMA((2,))],
        ),
        compiler_params=_cparams(("arbitrary",), 56),
        name="moe_ffn",
    )(src, tile_e1, tile_e2, tile_valid, h_rows, w_gate, w_gate, w_up, w_up, w_down, w_down, router_w_t)


def _combine_kernel(dest_ref, ys_hbm, x1_ref, mod_ref, g_ref, modn_ref, *rest, tm, s, n_tiles, final):
    if final:
        o_ref, buf, sem = rest
    else:
        o_ref, hn_ref, buf, sem = rest
    i = pl.program_id(0)
    slot = i % 2

    @pl.when(i == 0)
    def _():
        _start_row_gather(dest_ref, 0, ys_hbm, buf, sem, 0, tm, s)

    @pl.when(i + 1 < n_tiles)
    def _():
        _start_row_gather(dest_ref, (i + 1) * tm, ys_hbm, buf, sem, 1 - slot, tm, s)

    _wait_row_gather(ys_hbm, buf, sem, slot, tm, s)
    y = _load_slab_rows(buf.at[slot], tm, s)
    x2 = x1_ref[...] + mod_ref[0, 5:6, :] * y
    if final:
        ms = jnp.mean(x2 * x2, axis=-1, keepdims=True)
        o_ref[...] = x2 * lax.rsqrt(ms + EPS) * g_ref[...]
    else:
        o_ref[...] = x2
        hn_ref[...] = _rms_mod(x2, g_ref[...], modn_ref[0, 1:2, :], modn_ref[0, 0:1, :]).astype(hn_ref.dtype)


def combine_residual(dest, ys, x1, mod, g, mod_next, rows_per_mod, tm, final):
    m, d = x1.shape
    s = d // LANES
    n_tiles = m // tm
    kern = functools.partial(_combine_kernel, tm=tm, s=s, n_tiles=n_tiles, final=final)
    row_spec = pl.BlockSpec((tm, d), lambda i, dest: (i, 0))
    mod_spec = pl.BlockSpec((1, N_MOD, d), lambda i, dest: ((i * tm) // rows_per_mod, 0, 0))
    if final:
        out_shape, out_specs = jax.ShapeDtypeStruct((m, d), F32), row_spec
    else:
        out_shape = (jax.ShapeDtypeStruct((m, d), F32), jax.ShapeDtypeStruct((m, d), BF16))
        out_specs = (row_spec, row_spec)
    return pl.pallas_call(
        kern,
        out_shape=out_shape,
        grid_spec=pltpu.PrefetchScalarGridSpec(
            num_scalar_prefetch=1,
            grid=(n_tiles,),
            in_specs=[
                pl.BlockSpec(memory_space=pl.ANY),
                row_spec,
                mod_spec,
                pl.BlockSpec((1, d), lambda i, dest: (0, 0)),
                mod_spec,
            ],
            out_specs=out_specs,
            scratch_shapes=[pltpu.VMEM((2, tm * s, LANES), F32), pltpu.SemaphoreType.DMA((2,))],
        ),
        compiler_params=_cparams(("arbitrary",), 40),
        name="combine_residual",
    )(dest, ys, x1, mod, g.reshape(1, d), mod_next)


def _tile_tables(counts, n_tiles, tm):
    tiles_per_class = (counts[:N_CLASSES] + tm - 1) // tm
    tile_end = jnp.cumsum(tiles_per_class)
    total = tile_end[-1]
    idx = jnp.arange(n_tiles, dtype=I32)
    valid = idx < total
    blk = jnp.where(valid, idx, total - 1)
    cls = jnp.minimum(jnp.sum((tile_end[None, :] <= blk[:, None]).astype(I32), axis=1), N_CLASSES - 1)
    pair = jnp.asarray(PAIRS, I32)
    grp = cls // len(PAIRS)
    e1 = grp * EXPERTS_PER_GROUP + pair[cls % len(PAIRS), 0]
    e2 = grp * EXPERTS_PER_GROUP + pair[cls % len(PAIRS), 1]
    return e1.astype(I32), e2.astype(I32), valid.astype(I32)


def grouped_moe_residual(x1, h2_rows, cls, mod, router_w, w_gate, w_up, w_down, layer, g, mod_next,
                         rows_per_mod, final):
    m, d = x1.shape
    n_tiles = m // MOE_TM + N_CLASSES
    dest, counts = sort_tokens(cls, MOE_TM)
    dest = dest.reshape(m)
    e1, e2, valid = _tile_tables(counts, n_tiles, MOE_TM)
    src = inverse_map(dest, n_tiles * MOE_TM)
    ys = moe_ffn(src, e1, e2, valid, h2_rows, cast_experts_bf16(w_gate, layer), cast_experts_bf16(w_up, layer),
                 cast_experts_bf16(w_down, layer), router_w.T.astype(F32), d, MOE_TM)
    return combine_residual(dest, ys, x1, mod, g, mod_next, rows_per_mod, GATHER_TM, final)


def _forward(dims, x, c, ctx, c_ctx, ada_w, ada_b, mix_norm_g, ffn_norm_g, ev_w_in, ev_rpb, ev_w_out,
             od_w_in, od_b_in, od_dw_w, od_dw_b, od_ln_g, od_ln_b, od_w_out, od_b_out,
             router_w, router_b, moe_w_gate, moe_w_up, moe_w_down, final_norm_g, *, tiles):
    b, seq, d = dims.batch, dims.seq, dims.d
    n = b * seq
    depth = ada_w.shape[0]
    assert depth == 2, "layer 0 is the Fourier/attention mixer, layer 1 the Conformer mixer"
    x2d = x.reshape(n, d)
    ctx2d = ctx.reshape(b * dims.ctx, d)

    cvec = jnp.zeros((16, d), F32).at[:b].set(c).at[b].set(c_ctx)
    mod_all = adaln_mod(cvec, ada_w, ada_b, tiles["mod_tn"]).reshape(depth, 16, N_MOD, d)

    mod0 = mod_all[0, :b]
    modc0 = mod_all[0, b:b + 1]
    w_in = ev_w_in[0].astype(BF16)
    p = norm_mod_matmul(x2d, mix_norm_g[0], mod0, w_in, seq, 0, 1, tiles["in_tm"], dims.naw)
    kv_blk0 = (dims.fw + dims.naw) // dims.naw
    kvc = norm_mod_matmul(ctx2d, mix_norm_g[0], modc0, w_in, b * dims.ctx, 0, 1,
                          min(tiles["in_tm"], b * dims.ctx), dims.naw, col0_blk=kv_blk0)
    fo = fourier_mix(p, dims, tiles["four_tm"])
    ao = neighbourhood_attention(p, kvc, ev_rpb[0], dims)
    mod1 = mod_all[1, :b]
    mix_w = ev_w_out.shape[1]
    x1, h2_rows, cls = outproj_residual_route(
        fo, 0, ao, 0, jnp.ones((mix_w,), F32), jnp.zeros((mix_w,), F32), ev_w_out[0].astype(BF16),
        jnp.zeros((d,), F32), x2d, mod0, ffn_norm_g[0], router_w, router_b, seq, tiles["out_tm"], False)
    x2, hmix1 = grouped_moe_residual(x1, h2_rows, cls, mod0, router_w, moe_w_gate, moe_w_up, moe_w_down, 0,
                                     mix_norm_g[1], mod1, seq, False)

    yc = glu_conv(hmix1, od_w_in[0].astype(BF16), od_b_in[0], od_dw_w[0], od_dw_b[0], dims, tiles["glu_tn"])
    x3, h4_rows, cls1 = outproj_residual_route(
        yc, 0, yc, 1, od_ln_g[0], od_ln_b[0], od_w_out[0].astype(BF16), od_b_out[0], x2, mod1, ffn_norm_g[1],
        router_w, router_b, seq, tiles["out_tm"], True)
    out = grouped_moe_residual(x3, h4_rows, cls1, mod1, router_w, moe_w_gate, moe_w_up, moe_w_down, 1,
                               final_norm_g, mod1, seq, True)
    return out.reshape(b, seq, d)


TILES = dict(mod_tn=1024, in_tm=512, glu_tn=256, four_tm=512, out_tm=256)


def kernel(x, c, ctx, c_ctx, ada_w, ada_b, mix_norm_g, ffn_norm_g, ev_w_in, ev_rpb, ev_w_out, od_w_in, od_b_in,
           od_dw_w, od_dw_b, od_ln_g, od_ln_b, od_w_out, od_b_out, router_w, router_b, moe_w_gate, moe_w_up,
           moe_w_down, final_norm_g):
    b, seq, d = x.shape
    dims = Dims(batch=b, seq=seq, d=d, ctx=ctx.shape[1], fw=ev_w_out.shape[1] - NA_HEADS * NA_HEAD_DIM,
                naw=NA_HEADS * NA_HEAD_DIM, fe=moe_w_gate.shape[3])
    return _forward(dims, x, c, ctx, c_ctx, ada_w, ada_b, mix_norm_g, ffn_norm_g, ev_w_in, ev_rpb, ev_w_out,
                    od_w_in, od_b_in, od_dw_w, od_dw_b, od_ln_g, od_ln_b, od_w_out, od_b_out,
                    router_w, router_b, moe_w_gate, moe_w_up, moe_w_down, final_norm_g, tiles=TILES)
```

```python
import functools
import math
from typing import NamedTuple

import jax
import jax.numpy as jnp
import numpy as np
from jax import lax
from jax.experimental import pallas as pl
from jax.experimental.pallas import tpu as pltpu

F32 = jnp.float32
BF16 = jnp.bfloat16
I32 = jnp.int32

LANES = 128
NEG_BIG = -1e30
EPS = 1e-6

N_MOD = 6
N_FOURIER_GROUPS = 4
NA_HEADS = 16
NA_HEAD_DIM = 64
NA_ROWS = 8
NA_COLS = 16
GRID_W = 64
CONV_K = 31
N_EXPERTS = 16
N_GROUPS = 4
EXPERTS_PER_GROUP = 4
PAIRS = ((0, 1), (0, 2), (0, 3), (1, 3), (1, 2), (2, 3))
N_CLASSES = N_GROUPS * len(PAIRS)
CLASS_PAD = 32
MOE_TM = 256
SORT_BLK = 512
GATHER_TM = 256
SCALAR_UNROLL = 8


class Dims(NamedTuple):
    batch: int
    seq: int
    d: int
    ctx: int
    fw: int
    naw: int
    fe: int


def _cparams(sem, vmem_mb):
    return pltpu.CompilerParams(dimension_semantics=sem, vmem_limit_bytes=vmem_mb << 20)


def _sigmoid(x):
    return 1.0 / (1.0 + jnp.exp(-x))


def _rms_mod(x, g, scale, shift):
    ms = jnp.mean(x * x, axis=-1, keepdims=True)
    return (x * lax.rsqrt(ms + EPS) * g) * (1.0 + scale) + shift


def _mod_kernel(c_ref, w_ref, b_ref, o_ref):
    c = c_ref[...]
    s = (c * _sigmoid(c)).astype(BF16)
    o_ref[0] = jnp.dot(s, w_ref[0].astype(BF16), preferred_element_type=F32) + b_ref[0]


def adaln_mod(cvec, ada_w, ada_b, tn):
    depth, d, n = ada_w.shape
    rows = cvec.shape[0]
    return pl.pallas_call(
        _mod_kernel,
        out_shape=jax.ShapeDtypeStruct((depth, rows, n), F32),
        grid=(depth, n // tn),
        in_specs=[
            pl.BlockSpec((rows, d), lambda l, j: (0, 0)),
            pl.BlockSpec((1, d, tn), lambda l, j: (l, 0, j)),
            pl.BlockSpec((1, 1, tn), lambda l, j: (l, 0, j)),
        ],
        out_specs=pl.BlockSpec((1, rows, tn), lambda l, j: (l, 0, j)),
        compiler_params=_cparams(("parallel", "parallel"), 40),
        name="adaln_mod",
    )(cvec, ada_w, ada_b.reshape(depth, 1, n))


def _inproj_kernel(x0_ref, xn_ref, g_ref, mod0_ref, modn_ref, w_ref, o_ref, h_scr, *, shift_idx, scale_idx):
    i = pl.program_id(0)

    def norm_into(x_ref, mod_ref, slot):
        h = _rms_mod(x_ref[...], g_ref[...], mod_ref[0, scale_idx:scale_idx + 1, :],
                     mod_ref[0, shift_idx:shift_idx + 1, :])
        h_scr[slot] = h.astype(BF16)

    @pl.when(i == 0)
    def _():
        norm_into(x0_ref, mod0_ref, 0)

    def step(cur, nxt):
        o_ref[...] = jnp.dot(h_scr[cur], w_ref[...], preferred_element_type=F32).astype(o_ref.dtype)
        norm_into(xn_ref, modn_ref, nxt)

    @pl.when(i % 2 == 0)
    def _():
        step(0, 1)

    @pl.when(i % 2 == 1)
    def _():
        step(1, 0)


def norm_mod_matmul(x2d, g, mod, w, rows_per_mod, shift_idx, scale_idx, tm, n, col_blk):
    m, d = x2d.shape
    last = m // tm - 1
    kern = functools.partial(_inproj_kernel, shift_idx=shift_idx, scale_idx=scale_idx)

    def nxt(i):
        return jnp.minimum(i + 1, last)

    return pl.pallas_call(
        kern,
        out_shape=jax.ShapeDtypeStruct((m, n), BF16),
        grid=(m // tm,),
        in_specs=[
            pl.BlockSpec((tm, d), lambda i: (0, 0)),
            pl.BlockSpec((tm, d), lambda i: (nxt(i), 0)),
            pl.BlockSpec((1, d), lambda i: (0, 0)),
            pl.BlockSpec((1, N_MOD, d), lambda i: (0, 0, 0)),
            pl.BlockSpec((1, N_MOD, d), lambda i: ((nxt(i) * tm) // rows_per_mod, 0, 0)),
            pl.BlockSpec((d, n), lambda i: (0, col_blk), pipeline_mode=pl.Buffered(1)),
        ],
        out_specs=pl.BlockSpec((tm, n), lambda i: (i, 0)),
        scratch_shapes=[pltpu.VMEM((2, tm, d), BF16)],
        compiler_params=_cparams(("arbitrary",), 48),
        name="norm_mod_matmul",
    )(x2d, x2d, g.reshape(1, d), mod, mod, w)


def _cast_kernel(x_ref, o_ref):
    o_ref[...] = x_ref[...].astype(o_ref.dtype)


def cast_experts_bf16(w, layer):
    _, e, k, n = w.shape
    return pl.pallas_call(
        _cast_kernel,
        out_shape=jax.ShapeDtypeStruct((e, k, n), BF16),
        grid=(e,),
        in_specs=[pl.BlockSpec((None, 1, k, n), lambda i: (layer, i, 0, 0))],
        out_specs=pl.BlockSpec((1, k, n), lambda i: (i, 0, 0)),
        compiler_params=_cparams(("parallel",), 32),
        name="cast_experts_bf16",
    )(w)


def _fourier_kernel(u_ref, cs_ref, cl_ref, sl_ref, o_ref, xc_scr, xs_scr, *, gc):
    @pl.when(pl.program_id(1) == 0)
    def _():
        for grp in range(N_FOURIER_GROUPS):
            r = jnp.dot(u_ref[:, grp * gc:(grp + 1) * gc], cs_ref[...], preferred_element_type=F32)
            xc_scr[:, grp * gc:(grp + 1) * gc] = r[:, :gc].astype(BF16)
            xs_scr[:, grp * gc:(grp + 1) * gc] = r[:, gc:].astype(BF16)

    acc = jnp.dot(cl_ref[...], xc_scr[...], preferred_element_type=F32)
    acc = acc - jnp.dot(sl_ref[...], xs_scr[...], preferred_element_type=F32)
    o_ref[...] = acc.astype(o_ref.dtype)


def _dft_tables(length, scale):
    kn = np.outer(np.arange(length), np.arange(length)) % length
    ang = kn.astype(np.float64) * (2.0 * math.pi / length)
    return (np.cos(ang) * scale).astype(np.float32), (np.sin(ang) * scale).astype(np.float32)


def fourier_mix(p, dims, tm):
    gc = dims.fw // N_FOURIER_GROUPS
    cc, sc = _dft_tables(gc, 1.0 / math.sqrt(dims.seq * gc))
    cs = jnp.asarray(np.concatenate([cc, sc], axis=1).astype(BF16))
    cl, sl = (jnp.asarray(t.astype(BF16)) for t in _dft_tables(dims.seq, 1.0))
    kern = functools.partial(_fourier_kernel, gc=gc)
    return pl.pallas_call(
        kern,
        out_shape=jax.ShapeDtypeStruct((dims.batch * dims.seq, dims.fw), BF16),
        grid=(dims.batch, dims.seq // tm),
        in_specs=[
            pl.BlockSpec((dims.seq, dims.fw), lambda b, m: (b, 0)),
            pl.BlockSpec((gc, 2 * gc), lambda b, m: (0, 0)),
            pl.BlockSpec((tm, dims.seq), lambda b, m: (m, 0)),
            pl.BlockSpec((tm, dims.seq), lambda b, m: (m, 0)),
        ],
        out_specs=pl.BlockSpec((tm, dims.fw), lambda b, m: (b * (dims.seq // tm) + m, 0)),
        scratch_shapes=[pltpu.VMEM((dims.seq, dims.fw), BF16), pltpu.VMEM((dims.seq, dims.fw), BF16)],
        compiler_params=_cparams(("parallel", "arbitrary"), 48),
        name="fourier_mix",
    )(p, cs, cl, sl)


NA_QROWS = 2


def _na_geometry(n_rows):
    kr = min(NA_ROWS, n_rows)
    n_union = kr + NA_QROWS - 1
    steps = []
    for rp in range(n_rows // NA_QROWS):
        rows = [rp * NA_QROWS + a for a in range(NA_QROWS)]
        starts = [min(max(r - kr // 2, 0), n_rows - kr) for r in rows]
        u = min(starts[0], n_rows - n_union)
        steps.append((u, tuple((r - u, rs - u) for r, rs in zip(rows, starts))))
    variants = sorted(set(v for _, v in steps))
    return kr, n_union, [u for u, _ in steps], [variants.index(v) for _, v in steps], variants


def _na_kernel(var_ref, ustart_ref, q_ref, k_ref, v_ref, kc_ref, vc_ref, t_ref, o_ref, s_scr, *, n_union):
    del var_ref
    rp = pl.program_id(1)
    row0 = pl.multiple_of(ustart_ref[rp] * GRID_W, GRID_W)
    nloc = n_union * GRID_W
    nctx = kc_ref.shape[0]
    low = lax.broadcasted_iota(I32, (GRID_W, LANES), 1) < NA_HEAD_DIM
    nt = (((1,), (1,)), ((), ()))
    tn = (((0,), (0,)), ((), ()))
    scale = NA_HEAD_DIM ** -0.5
    n_pairs = NA_HEADS // 2

    def scores(pr):
        cols = pl.ds(pr * LANES, LANES)
        q2 = q_ref[:, cols] * scale
        parts = []
        for a in range(NA_QROWS):
            qa = q2[a * GRID_W:(a + 1) * GRID_W]
            zero = jnp.zeros_like(qa)
            parts += [jnp.where(low, qa, zero), jnp.where(low, zero, qa)]
        qm = jnp.concatenate(parts, axis=0)
        kp = k_ref[pl.ds(row0, nloc), cols]
        s_scr[pr % 2, 0:nloc, :] = (lax.dot_general(kp, qm, nt, preferred_element_type=F32)
                                     + t_ref[0, pr].astype(F32))
        s_scr[pr % 2, nloc:nloc + nctx, :] = lax.dot_general(kc_ref[:, cols], qm, nt, preferred_element_type=F32)

    def attend(pr):
        cols = pl.ds(pr * LANES, LANES)
        s = s_scr[pr % 2]
        m = jnp.max(s, axis=0, keepdims=True)
        e = jnp.exp(s - m)
        den = jnp.sum(e, axis=0, keepdims=True)
        eb = e.astype(BF16)
        vp = v_ref[pl.ds(row0, nloc), cols]
        ot = lax.dot_general(vp, eb[0:nloc], tn, preferred_element_type=F32)
        ot = ot + lax.dot_general(vc_ref[:, cols], eb[nloc:], tn, preferred_element_type=F32)
        o = (ot / den).T
        for a in range(NA_QROWS):
            base = a * 2 * GRID_W
            o_ref[a * GRID_W:(a + 1) * GRID_W, cols] = jnp.where(
                low, o[base:base + GRID_W], o[base + GRID_W:base + 2 * GRID_W]).astype(o_ref.dtype)

    scores(0)
    for pr in range(n_pairs):
        if pr + 1 < n_pairs:
            scores(pr + 1)
        attend(pr)


def _na_bias_table(rpb, n_rows):
    kr, n_union, _, _, variants = _na_geometry(n_rows)
    col = jnp.arange(GRID_W)
    col_start = jnp.clip(col - NA_COLS // 2, 0, GRID_W - NA_COLS)
    col_mask = (col[None, :] >= col_start[:, None]) & (col[None, :] < col_start[:, None] + NA_COLS)
    dc_idx = jnp.clip(col[None, :] - col[:, None] + NA_COLS - 1, 0, 2 * NA_COLS - 2)
    colbias = jnp.where(col_mask[None, None], rpb.astype(F32)[:, :, dc_idx], NEG_BIG)
    colbias = colbias.transpose(0, 1, 3, 2).astype(BF16)
    masked = jnp.full((NA_HEADS, GRID_W, GRID_W), NEG_BIG, BF16)
    out = []
    for var in variants:
        per_row = []
        for r_off, rs_off in var:
            blocks = [colbias[:, jj - r_off + NA_ROWS - 1] if rs_off <= jj < rs_off + kr else masked
                      for jj in range(n_union)]
            per_row.append(jnp.concatenate(blocks, axis=1))
        t = jnp.stack(per_row, axis=0)
        t = t.reshape(NA_QROWS, NA_HEADS // 2, 2, n_union * GRID_W, GRID_W)
        out.append(t.transpose(1, 3, 0, 2, 4).reshape(NA_HEADS // 2, n_union * GRID_W, NA_QROWS * 2 * GRID_W))
    return jnp.stack(out, axis=0)


def neighbourhood_attention(p, kvc, rpb, dims):
    n_rows = dims.seq // GRID_W
    _, n_union, ustarts, var_of_step, _ = _na_geometry(n_rows)
    n_steps = n_rows // NA_QROWS
    table = _na_bias_table(rpb, n_rows)
    naw = dims.naw
    qb = dims.fw // naw
    tq = NA_QROWS * GRID_W
    kern = functools.partial(_na_kernel, n_union=n_union)
    return pl.pallas_call(
        kern,
        out_shape=jax.ShapeDtypeStruct((dims.batch * dims.seq, naw), BF16),
        grid_spec=pltpu.PrefetchScalarGridSpec(
            num_scalar_prefetch=2,
            grid=(dims.batch, n_steps),
            in_specs=[
                pl.BlockSpec((tq, naw), lambda b, r, var, us: (b * n_steps + r, qb)),
                pl.BlockSpec((dims.seq, naw), lambda b, r, var, us: (b, qb + 1)),
                pl.BlockSpec((dims.seq, naw), lambda b, r, var, us: (b, qb + 2)),
                pl.BlockSpec((dims.ctx, naw), lambda b, r, var, us: (b, 0)),
                pl.BlockSpec((dims.ctx, naw), lambda b, r, var, us: (b, 1)),
                pl.BlockSpec((1, NA_HEADS // 2, n_union * GRID_W, 2 * tq), lambda b, r, var, us: (var[r], 0, 0, 0)),
            ],
            out_specs=pl.BlockSpec((tq, naw), lambda b, r, var, us: (b * n_steps + r, 0)),
            scratch_shapes=[pltpu.VMEM((2, n_union * GRID_W + dims.ctx, 2 * tq), F32)],
        ),
        compiler_params=_cparams(("parallel", "arbitrary"), 48),
        name="neighbourhood_attention",
    )(jnp.asarray(var_of_step, I32), jnp.asarray(ustarts, I32), p, p, p, kvc, kvc, table)


CONV_HALO = 16
CONV_RB = 128
CONV_MM = 256
SUBLANES = 8


def _glu_conv_kernel(h_ref, wa_ref, wg_ref, ba_ref, bg_ref, dw_ref, db_ref, o_ref, scr, *, seq):
    tn = o_ref.shape[1]
    scr[0:CONV_HALO, :] = jnp.zeros((CONV_HALO, tn), F32)
    scr[CONV_HALO + seq:2 * CONV_HALO + seq, :] = jnp.zeros((CONV_HALO, tn), F32)
    base = CONV_HALO - CONV_K // 2
    n_shift = (CONV_K + base + SUBLANES - 1) // SUBLANES

    def project(c):
        row0 = c * CONV_MM
        h = h_ref[pl.ds(row0, CONV_MM), :]
        a = jnp.dot(h, wa_ref[...], preferred_element_type=F32) + ba_ref[...]
        gt = jnp.dot(h, wg_ref[...], preferred_element_type=F32) + bg_ref[...]
        scr[pl.ds(row0 + CONV_HALO, CONV_MM), :] = a * _sigmoid(gt)

    def conv_block(rb):
        row0 = rb * CONV_RB
        for cb in range(tn // LANES):
            cols = pl.ds(cb * LANES, LANES)
            acc = db_ref[:, cols]
            for ph in range(SUBLANES):
                part = None
                for st in range(n_shift):
                    k = st * SUBLANES + ph - base
                    if 0 <= k < CONV_K:
                        rows = pl.ds(row0 + st * SUBLANES, CONV_RB + SUBLANES)
                        term = scr[rows, cols] * dw_ref[k:k + 1, cols]
                        part = term if part is None else part + term
                acc = acc + part[ph:ph + CONV_RB]
            o_ref[pl.ds(row0, CONV_RB), cols] = acc.astype(o_ref.dtype)

    n_mm = seq // CONV_MM
    n_rb = seq // CONV_RB
    always = pl.program_id(0) >= 0
    project(0)
    done = 0
    for c in range(n_mm):
        ready = ((c + 1) * CONV_MM - CONV_K // 2) // CONV_RB if c + 1 < n_mm else n_rb

        @pl.when(always)
        def _(c=c, done=done, ready=ready):
            if c + 1 < n_mm:
                project(c + 1)
            for rb in range(done, ready):
                conv_block(rb)

        done = ready


def glu_conv(h, w, b, dw_w, dw_b, dims, tn):
    d = h.shape[1]
    half = w.shape[1] // 2
    nblk = half // tn
    b2 = b.reshape(1, 2 * half)
    kern = functools.partial(_glu_conv_kernel, seq=dims.seq)
    return pl.pallas_call(
        kern,
        out_shape=jax.ShapeDtypeStruct((dims.batch * dims.seq, half), BF16),
        grid=(dims.batch, nblk),
        in_specs=[
            pl.BlockSpec((dims.seq, d), lambda s, j: (s, 0)),
            pl.BlockSpec((d, tn), lambda s, j: (0, j)),
            pl.BlockSpec((d, tn), lambda s, j: (0, j + nblk)),
            pl.BlockSpec((1, tn), lambda s, j: (0, j)),
            pl.BlockSpec((1, tn), lambda s, j: (0, j + nblk)),
            pl.BlockSpec((CONV_K, tn), lambda s, j: (0, j)),
            pl.BlockSpec((1, tn), lambda s, j: (0, j)),
        ],
        out_specs=pl.BlockSpec((dims.seq, tn), lambda s, j: (s, j)),
        scratch_shapes=[pltpu.VMEM((dims.seq + 2 * CONV_HALO, tn), F32)],
        compiler_params=_cparams(("parallel", "arbitrary"), 48),
        name="glu_conv",
    )(h, w, w, b2, b2, dw_w, dw_b.reshape(1, half))


def _route(logits_t, rb):
    sel = [_sigmoid(logits_t[e:e + 1, :]) + rb[e:e + 1, :] for e in range(N_EXPERTS)]
    gs = []
    for g in range(N_GROUPS):
        v0, v1, v2, v3 = sel[4 * g:4 * g + 4]
        hi1, lo1 = jnp.maximum(v0, v1), jnp.minimum(v0, v1)
        hi2, lo2 = jnp.maximum(v2, v3), jnp.minimum(v2, v3)
        gs.append(jnp.maximum(hi1, hi2) + jnp.maximum(jnp.minimum(hi1, hi2), jnp.maximum(lo1, lo2)))
    best = gs[0]
    bg = jnp.zeros(best.shape, I32)
    for g in range(1, N_GROUPS):
        upd = gs[g] > best
        bg = jnp.where(upd, g, bg)
        best = jnp.where(upd, gs[g], best)
    v = []
    for i in range(EXPERTS_PER_GROUP):
        vi = sel[i]
        for g in range(1, N_GROUPS):
            vi = jnp.where(bg == g, sel[4 * g + i], vi)
        v.append(vi)
    picked = []
    for i in range(EXPERTS_PER_GROUP):
        rank = jnp.zeros(best.shape, I32)
        for j in range(EXPERTS_PER_GROUP):
            if j == i:
                continue
            ahead = (v[j] > v[i]) | ((v[j] == v[i]) & (j < i))
            rank = rank + ahead.astype(I32)
        picked.append(rank < 2)
    code = jnp.full(best.shape, len(PAIRS) - 1, I32)
    for idx in range(len(PAIRS) - 2, -1, -1):
        a, b = PAIRS[idx]
        code = jnp.where(picked[a] & picked[b], idx, code)
    return bg * len(PAIRS) + code


def _outproj_kernel(a_ref, b_ref, lng_ref, lnb_ref, w_ref, bias_ref, x_ref, mod_ref, g_ref, rw1_ref, rw2_ref,
                    rb_ref, x1_ref, h2_ref, cls_ref, *, n_chunks, ln_swish):
    ka = a_ref.shape[1]
    tm = a_ref.shape[0]
    if ln_swish:
        t = jnp.concatenate([a_ref[...], b_ref[...]], axis=-1).astype(F32)
        mu = jnp.mean(t, axis=-1, keepdims=True)
        tc = t - mu
        var = jnp.mean(tc * tc, axis=-1, keepdims=True)
        z = tc * lax.rsqrt(var + EPS) * lng_ref[...] + lnb_ref[...]
        z = (z * _sigmoid(z)).astype(BF16)
        y = jnp.dot(z, w_ref[...], preferred_element_type=F32) + bias_ref[...]
    else:
        y = jnp.dot(a_ref[...], w_ref[0:ka, :], preferred_element_type=F32)
        y = y + jnp.dot(b_ref[...], w_ref[ka:, :], preferred_element_type=F32) + bias_ref[...]
    x1 = x_ref[...] + mod_ref[0, 2:3, :] * y
    x1_ref[...] = x1
    h2 = _rms_mod(x1, g_ref[...], mod_ref[0, 4:5, :], mod_ref[0, 3:4, :])
    for c in range(n_chunks):
        h2_ref[pl.ds(c, tm, stride=n_chunks), :] = h2[:, c * LANES:(c + 1) * LANES]
    hi = h2.astype(BF16)
    lo = (h2 - hi.astype(F32)).astype(BF16)
    s = jnp.dot(hi, rw1_ref[...], preferred_element_type=F32) + jnp.dot(lo, rw2_ref[...], preferred_element_type=F32)
    st = s.T
    logits_t = st[0:N_EXPERTS, :] + st[N_EXPERTS:2 * N_EXPERTS, :]
    cls_ref[...] = _route(logits_t, rb_ref[...])


def outproj_residual_route(a, a_col, b, b_col, ln_g, ln_b, w, bias, x2d, mod, ffn_g, router_w, router_b,
                           rows_per_mod, tm, ln_swish):
    m, d = x2d.shape
    ka = kb = w.shape[0] // 2
    kin = ka + kb
    n_chunks = d // LANES
    rw_hi = router_w.astype(BF16)
    rw_lo = (router_w - rw_hi.astype(F32)).astype(BF16)
    pad = jnp.zeros((d, LANES - 2 * N_EXPERTS), BF16)
    rw1 = jnp.concatenate([rw_hi, rw_lo, pad], axis=1)
    rw2 = jnp.concatenate([rw_hi, jnp.zeros((d, N_EXPERTS), BF16), pad], axis=1)
    kern = functools.partial(_outproj_kernel, n_chunks=n_chunks, ln_swish=ln_swish)
    return pl.pallas_call(
        kern,
        out_shape=(
            jax.ShapeDtypeStruct((m, d), F32),
            jax.ShapeDtypeStruct((m * n_chunks, LANES), F32),
            jax.ShapeDtypeStruct((1, m), I32),
        ),
        grid=(m // tm,),
        in_specs=[
            pl.BlockSpec((tm, ka), lambda i: (i, a_col)),
            pl.BlockSpec((tm, kb), lambda i: (i, b_col)),
            pl.BlockSpec((1, kin), lambda i: (0, 0)),
            pl.BlockSpec((1, kin), lambda i: (0, 0)),
            pl.BlockSpec((kin, d), lambda i: (0, 0)),
            pl.BlockSpec((1, d), lambda i: (0, 0)),
            pl.BlockSpec((tm, d), lambda i: (i, 0)),
            pl.BlockSpec((1, N_MOD, d), lambda i: ((i * tm) // rows_per_mod, 0, 0)),
            pl.BlockSpec((1, d), lambda i: (0, 0)),
            pl.BlockSpec((d, LANES), lambda i: (0, 0)),
            pl.BlockSpec((d, LANES), lambda i: (0, 0)),
            pl.BlockSpec((N_EXPERTS, 1), lambda i: (0, 0)),
        ],
        out_specs=(
            pl.BlockSpec((tm, d), lambda i: (i, 0)),
            pl.BlockSpec((tm * n_chunks, LANES), lambda i: (i, 0)),
            pl.BlockSpec((1, tm), lambda i: (0, i)),
        ),
        compiler_params=_cparams(("parallel",), 56),
        name="outproj_residual_route",
    )(a, b, ln_g.reshape(1, kin), ln_b.reshape(1, kin), w, bias.reshape(1, d), x2d, mod, ffn_g.reshape(1, d),
      rw1, rw2, router_b.astype(F32).reshape(N_EXPERTS, 1))


def _class_onehot(cls_ref):
    blk = cls_ref.shape[1]
    return (lax.broadcasted_iota(I32, (CLASS_PAD, blk), 0) == cls_ref[...]).astype(F32)


def _count_kernel(cls_ref, cnt_ref):
    @pl.when(pl.program_id(0) == 0)
    def _():
        cnt_ref[...] = jnp.zeros_like(cnt_ref)

    cnt_ref[...] += jnp.sum(_class_onehot(cls_ref), axis=1, keepdims=True)


def _dest_kernel(cls_ref, start_ref, dest_ref, carry_scr):
    @pl.when(pl.program_id(0) == 0)
    def _():
        carry_scr[...] = jnp.zeros_like(carry_scr)

    blk = cls_ref.shape[1]
    onehot = _class_onehot(cls_ref)
    tri = (lax.broadcasted_iota(I32, (blk, blk), 0) <= lax.broadcasted_iota(I32, (blk, blk), 1)).astype(BF16)
    cum = jnp.dot(onehot.astype(BF16), tri, preferred_element_type=F32)
    pos = cum - 1.0 + carry_scr[...] + start_ref[...]
    dest_ref[...] = jnp.sum(onehot * pos, axis=0, keepdims=True).astype(I32)
    carry_scr[...] += jnp.sum(onehot, axis=1, keepdims=True)


def sort_tokens(cls, tile):
    n = cls.shape[1]
    nb = n // SORT_BLK
    counts = pl.pallas_call(
        _count_kernel,
        out_shape=jax.ShapeDtypeStruct((CLASS_PAD, 1), F32),
        grid=(nb,),
        in_specs=[pl.BlockSpec((1, SORT_BLK), lambda j: (0, j))],
        out_specs=pl.BlockSpec((CLASS_PAD, 1), lambda j: (0, 0)),
        compiler_params=_cparams(("arbitrary",), 32),
        name="class_counts",
    )(cls)
    counts = counts.reshape(CLASS_PAD).astype(I32)
    padded = ((counts + tile - 1) // tile) * tile
    starts = (jnp.cumsum(padded) - padded).astype(F32).reshape(CLASS_PAD, 1)
    dest = pl.pallas_call(
        _dest_kernel,
        out_shape=jax.ShapeDtypeStruct((1, n), I32),
        grid=(nb,),
        in_specs=[pl.BlockSpec((1, SORT_BLK), lambda j: (0, j)),
                  pl.BlockSpec((CLASS_PAD, 1), lambda j: (0, 0))],
        out_specs=pl.BlockSpec((1, SORT_BLK), lambda j: (0, j)),
        scratch_shapes=[pltpu.VMEM((CLASS_PAD, 1), F32)],
        compiler_params=_cparams(("arbitrary",), 32),
        name="sorted_positions",
    )(cls, starts)
    return dest, counts


def _inverse_kernel(dest_ref, src_ref, *, n_tok, n_rows):
    def clear(k, carry):
        for u in range(SCALAR_UNROLL):
            src_ref[k * SCALAR_UNROLL + u] = 0
        return carry

    lax.fori_loop(0, n_rows // SCALAR_UNROLL, clear, 0)

    def scatter(k, carry):
        for u in range(SCALAR_UNROLL):
            t = k * SCALAR_UNROLL + u
            src_ref[dest_ref[t]] = t
        return carry

    lax.fori_loop(0, n_tok // SCALAR_UNROLL, scatter, 0)


def inverse_map(dest, n_rows):
    n_tok = dest.shape[0]
    kern = functools.partial(_inverse_kernel, n_tok=n_tok, n_rows=n_rows)
    return pl.pallas_call(
        kern,
        out_shape=jax.ShapeDtypeStruct((n_rows,), I32),
        in_specs=[pl.BlockSpec(memory_space=pltpu.SMEM)],
        out_specs=pl.BlockSpec(memory_space=pltpu.SMEM),
        name="inverse_map",
    )(dest)


def _start_row_gather(idx_ref, base, src_hbm, buf, sem, slot, tm, s):
    def body(k, carry):
        for u in range(SCALAR_UNROLL):
            r = k * SCALAR_UNROLL + u
            pltpu.make_async_copy(
                src_hbm.at[pl.ds(pl.multiple_of(idx_ref[base + r] * s, s), s), :],
                buf.at[slot, pl.ds(pl.multiple_of(r * s, s), s), :],
                sem.at[slot]).start()
        return carry

    lax.fori_loop(0, tm // SCALAR_UNROLL, body, 0)


def _wait_row_gather(src_hbm, buf, sem, slot, tm, s):
    pltpu.make_async_copy(src_hbm.at[pl.ds(0, tm * s), :], buf.at[slot], sem.at[slot]).wait()


def _load_slab_rows(ref, tm, n_chunks):
    return jnp.concatenate([ref[pl.ds(c, tm, stride=n_chunks), :] for c in range(n_chunks)], axis=-1)


def _moe_kernel(src_ref, e1_ref, e2_ref, valid_ref, h_hbm, wg1_ref, wg2_ref, wu1_ref, wu2_ref,
                wd1_ref, wd2_ref, rwt_ref, ys_ref, buf, sem, *, tm, n_chunks, n_tiles):
    i = pl.program_id(0)
    slot = i % 2
    nxt_base = jnp.minimum(i + 1, n_tiles - 1) * tm

    @pl.when(i == 0)
    def _():
        _start_row_gather(src_ref, 0, h_hbm, buf, sem, 0, tm, n_chunks)

    @pl.when(valid_ref[i] == 0)
    def _():
        _start_row_gather(src_ref, nxt_base, h_hbm, buf, sem, 1 - slot, tm, n_chunks)
        _wait_row_gather(h_hbm, buf, sem, slot, tm, n_chunks)
        ys_ref[...] = jnp.zeros_like(ys_ref)

    def ffn(cur, nxt):
        n_pieces = 8
        piece = tm // n_pieces

        def start_piece(p):
            for r in range(p * piece, (p + 1) * piece):
                pltpu.make_async_copy(
                    h_hbm.at[pl.ds(pl.multiple_of(src_ref[nxt_base + r] * n_chunks, n_chunks), n_chunks), :],
                    buf.at[nxt, pl.ds(r * n_chunks, n_chunks), :],
                    sem.at[nxt]).start()

        _wait_row_gather(h_hbm, buf, sem, cur, tm, n_chunks)
        x = _load_slab_rows(buf.at[cur], tm, n_chunks).astype(BF16)
        start_piece(0)
        xr = x.astype(F32)
        l1 = jnp.sum(xr * rwt_ref[pl.ds(e1_ref[i], 1), :], axis=-1, keepdims=True)
        l2 = jnp.sum(xr * rwt_ref[pl.ds(e2_ref[i], 1), :], axis=-1, keepdims=True)
        s1 = _sigmoid(l1)
        s2 = _sigmoid(l2)
        tot = s1 + s2
        start_piece(1)
        g1 = jnp.dot(x, wg1_ref[0], preferred_element_type=F32)
        start_piece(2)
        u1 = jnp.dot(x, wu1_ref[0], preferred_element_type=F32)
        a1 = (g1 * _sigmoid(g1) * u1 * (s1 / tot)).astype(BF16)
        start_piece(3)
        g2 = jnp.dot(x, wg2_ref[0], preferred_element_type=F32)
        start_piece(4)
        u2 = jnp.dot(x, wu2_ref[0], preferred_element_type=F32)
        a2 = (g2 * _sigmoid(g2) * u2 * (s2 / tot)).astype(BF16)
        start_piece(5)
        y = jnp.dot(a1, wd1_ref[0], preferred_element_type=F32)
        start_piece(6)
        y = y + jnp.dot(a2, wd2_ref[0], preferred_element_type=F32)
        start_piece(7)
        for c in range(n_chunks):
            ys_ref[pl.ds(c, tm, stride=n_chunks), :] = y[:, c * LANES:(c + 1) * LANES]

    @pl.when((valid_ref[i] == 1) & (slot == 0))
    def _():
        ffn(0, 1)

    @pl.when((valid_ref[i] == 1) & (slot == 1))
    def _():
        ffn(1, 0)

    @pl.when(i == n_tiles - 1)
    def _():
        _wait_row_gather(h_hbm, buf, sem, 1 - slot, tm, n_chunks)


def moe_ffn(src, tile_e1, tile_e2, tile_valid, h_rows, w_gate, w_up, w_down, router_w_t, d, tm):
    n_tiles = tile_e1.shape[0]
    n_chunks = d // LANES
    fe = w_gate.shape[2]
    kern = functools.partial(_moe_kernel, tm=tm, n_chunks=n_chunks, n_tiles=n_tiles)

    def w_in_spec(which):
        return pl.BlockSpec((1, d, fe), lambda i, src, e1, e2, valid: ((e1, e2)[which][i], 0, 0))

    def w_out_spec(which):
        return pl.BlockSpec((1, fe, d), lambda i, src, e1, e2, valid: ((e1, e2)[which][i], 0, 0))

    return pl.pallas_call(
        kern,
        out_shape=jax.ShapeDtypeStruct((n_tiles * tm * n_chunks, LANES), F32),
        grid_spec=pltpu.PrefetchScalarGridSpec(
            num_scalar_prefetch=4,
            grid=(n_tiles,),
            in_specs=[
                pl.BlockSpec(memory_space=pl.ANY),
                w_in_spec(0), w_in_spec(1), w_in_spec(0), w_in_spec(1), w_out_spec(0), w_out_spec(1),
                pl.BlockSpec((N_EXPERTS, d), lambda i, src, e1, e2, valid: (0, 0)),
            ],
            out_specs=pl.BlockSpec((tm * n_chunks, LANES), lambda i, src, e1, e2, valid: (i, 0)),
            scratch_shapes=[pltpu.VMEM((2, tm * n_chunks, LANES), F32), pltpu.SemaphoreType.DMA((2,))],
        ),
        compiler_params=_cparams(("arbitrary",), 56),
        name="moe_ffn",
    )(src, tile_e1, tile_e2, tile_valid, h_rows, w_gate, w_gate, w_up, w_up, w_down, w_down, router_w_t)


def _combine_kernel(dest_ref, ys_hbm, x1_ref, mod_ref, g_ref, modn_ref, *rest, tm, s, n_tiles, final):
    if final:
        o_ref, buf, sem = rest
    else:
        o_ref, hn_ref, buf, sem = rest
    i = pl.program_id(0)
    slot = i % 2

    @pl.when(i == 0)
    def _():
        _start_row_gather(dest_ref, 0, ys_hbm, buf, sem, 0, tm, s)

    @pl.when(i + 1 < n_tiles)
    def _():
        _start_row_gather(dest_ref, (i + 1) * tm, ys_hbm, buf, sem, 1 - slot, tm, s)

    _wait_row_gather(ys_hbm, buf, sem, slot, tm, s)
    y = _load_slab_rows(buf.at[slot], tm, s)
    x2 = x1_ref[...] + mod_ref[0, 5:6, :] * y
    if final:
        ms = jnp.mean(x2 * x2, axis=-1, keepdims=True)
        o_ref[...] = x2 * lax.rsqrt(ms + EPS) * g_ref[...]
    else:
        o_ref[...] = x2
        hn_ref[...] = _rms_mod(x2, g_ref[...], modn_ref[0, 1:2, :], modn_ref[0, 0:1, :]).astype(hn_ref.dtype)


def combine_residual(dest, ys, x1, mod, g, mod_next, rows_per_mod, tm, final):
    m, d = x1.shape
    s = d // LANES
    n_tiles = m // tm
    kern = functools.partial(_combine_kernel, tm=tm, s=s, n_tiles=n_tiles, final=final)
    row_spec = pl.BlockSpec((tm, d), lambda i, dest: (i, 0))
    mod_spec = pl.BlockSpec((1, N_MOD, d), lambda i, dest: ((i * tm) // rows_per_mod, 0, 0))
    if final:
        out_shape, out_specs = jax.ShapeDtypeStruct((m, d), F32), row_spec
    else:
        out_shape = (jax.ShapeDtypeStruct((m, d), F32), jax.ShapeDtypeStruct((m, d), BF16))
        out_specs = (row_spec, row_spec)
    return pl.pallas_call(
        kern,
        out_shape=out_shape,
        grid_spec=pltpu.PrefetchScalarGridSpec(
            num_scalar_prefetch=1,
            grid=(n_tiles,),
            in_specs=[
                pl.BlockSpec(memory_space=pl.ANY),
                row_spec,
                mod_spec,
                pl.BlockSpec((1, d), lambda i, dest: (0, 0)),
                mod_spec,
            ],
            out_specs=out_specs,
            scratch_shapes=[pltpu.VMEM((2, tm * s, LANES), F32), pltpu.SemaphoreType.DMA((2,))],
        ),
        compiler_params=_cparams(("arbitrary",), 40),
        name="combine_residual",
    )(dest, ys, x1, mod, g.reshape(1, d), mod_next)


def _tile_tables(counts, n_tiles, tm):
    tiles_per_class = (counts[:N_CLASSES] + tm - 1) // tm
    tile_end = jnp.cumsum(tiles_per_class)
    total = tile_end[-1]
    idx = jnp.arange(n_tiles, dtype=I32)
    valid = idx < total
    blk = jnp.where(valid, idx, total - 1)
    cls = jnp.minimum(jnp.sum((tile_end[None, :] <= blk[:, None]).astype(I32), axis=1), N_CLASSES - 1)
    pair = jnp.asarray(PAIRS, I32)
    grp = cls // len(PAIRS)
    e1 = grp * EXPERTS_PER_GROUP + pair[cls % len(PAIRS), 0]
    e2 = grp * EXPERTS_PER_GROUP + pair[cls % len(PAIRS), 1]
    return e1.astype(I32), e2.astype(I32), valid.astype(I32)


def grouped_moe_residual(x1, h2_rows, cls, mod, router_w, w_gate, w_up, w_down, layer, g, mod_next,
                         rows_per_mod, final):
    m, d = x1.shape
    n_tiles = m // MOE_TM + N_CLASSES
    dest, counts = sort_tokens(cls, MOE_TM)
    dest = dest.reshape(m)
    e1, e2, valid = _tile_tables(counts, n_tiles, MOE_TM)
    src = inverse_map(dest, n_tiles * MOE_TM)
    ys = moe_ffn(src, e1, e2, valid, h2_rows, cast_experts_bf16(w_gate, layer), cast_experts_bf16(w_up, layer),
                 cast_experts_bf16(w_down, layer), router_w.T.astype(F32), d, MOE_TM)
    return combine_residual(dest, ys, x1, mod, g, mod_next, rows_per_mod, GATHER_TM, final)


def _forward(dims, x, c, ctx, c_ctx, ada_w, ada_b, mix_norm_g, ffn_norm_g, ev_w_in, ev_rpb, ev_w_out,
             od_w_in, od_b_in, od_dw_w, od_dw_b, od_ln_g, od_ln_b, od_w_out, od_b_out,
             router_w, router_b, moe_w_gate, moe_w_up, moe_w_down, final_norm_g, *, tiles):
    b, seq, d = dims.batch, dims.seq, dims.d
    n = b * seq
    depth = ada_w.shape[0]
    assert depth == 2, "layer 0 is the Fourier/attention mixer, layer 1 the Conformer mixer"
    x2d = x.reshape(n, d)
    ctx2d = ctx.reshape(b * dims.ctx, d)

    cvec = jnp.zeros((16, d), F32).at[:b].set(c).at[b].set(c_ctx)
    mod_all = adaln_mod(cvec, ada_w, ada_b, tiles["mod_tn"]).reshape(depth, 16, N_MOD, d)

    mod0 = mod_all[0, :b]
    modc0 = mod_all[0, b:b + 1]
    w_in = ev_w_in[0].astype(BF16)
    p = norm_mod_matmul(x2d, mix_norm_g[0], mod0, w_in, seq, 0, 1, tiles["in_tm"], w_in.shape[1], 0)
    kvc = norm_mod_matmul(ctx2d, mix_norm_g[0], modc0, w_in, b * dims.ctx, 0, 1,
                          min(tiles["in_tm"], b * dims.ctx), 2 * dims.naw, 1)
    fo = fourier_mix(p, dims, tiles["four_tm"])
    ao = neighbourhood_attention(p, kvc, ev_rpb[0], dims)
    mod1 = mod_all[1, :b]
    mix_w = ev_w_out.shape[1]
    x1, h2_rows, cls = outproj_residual_route(
        fo, 0, ao, 0, jnp.ones((mix_w,), F32), jnp.zeros((mix_w,), F32), ev_w_out[0].astype(BF16),
        jnp.zeros((d,), F32), x2d, mod0, ffn_norm_g[0], router_w, router_b, seq, tiles["out_tm"], False)
    x2, hmix1 = grouped_moe_residual(x1, h2_rows, cls, mod0, router_w, moe_w_gate, moe_w_up, moe_w_down, 0,
                                     mix_norm_g[1], mod1, seq, False)

    yc = glu_conv(hmix1, od_w_in[0].astype(BF16), od_b_in[0], od_dw_w[0], od_dw_b[0], dims, tiles["glu_tn"])
    x3, h4_rows, cls1 = outproj_residual_route(
        yc, 0, yc, 1, od_ln_g[0], od_ln_b[0], od_w_out[0].astype(BF16), od_b_out[0], x2, mod1, ffn_norm_g[1],
        router_w, router_b, seq, tiles["out_tm"], True)
    out = grouped_moe_residual(x3, h4_rows, cls1, mod1, router_w, moe_w_gate, moe_w_up, moe_w_down, 1,
                               final_norm_g, mod1, seq, True)
    return out.reshape(b, seq, d)


TILES = dict(mod_tn=1024, in_tm=256, glu_tn=256, four_tm=512, out_tm=256)


def kernel(x, c, ctx, c_ctx, ada_w, ada_b, mix_norm_g, ffn_norm_g, ev_w_in, ev_rpb, ev_w_out, od_w_in, od_b_in,
           od_dw_w, od_dw_b, od_ln_g, od_ln_b, od_w_out, od_b_out, router_w, router_b, moe_w_gate, moe_w_up,
           moe_w_down, final_norm_g):
    b, seq, d = x.shape
    dims = Dims(batch=b, seq=seq, d=d, ctx=ctx.shape[1], fw=ev_w_out.shape[1] - NA_HEADS * NA_HEAD_DIM,
                naw=NA_HEADS * NA_HEAD_DIM, fe=moe_w_gate.shape[3])
    return _forward(dims, x, c, ctx, c_ctx, ada_w, ada_b, mix_norm_g, ffn_norm_g, ev_w_in, ev_rpb, ev_w_out,
                    od_w_in, od_b_in, od_dw_w, od_dw_b, od_ln_g, od_ln_b, od_w_out, od_b_out,
                    router_w, router_b, moe_w_gate, moe_w_up, moe_w_down, final_norm_g, tiles=TILES)
```

```python
import functools
import math
from typing import NamedTuple

import jax
import jax.numpy as jnp
import numpy as np
from jax import lax
from jax.experimental import pallas as pl
from jax.experimental.pallas import tpu as pltpu

F32 = jnp.float32
BF16 = jnp.bfloat16
I32 = jnp.int32

LANES = 128
NEG_BIG = -1e30
EPS = 1e-6

N_MOD = 6
N_FOURIER_GROUPS = 4
NA_HEADS = 16
NA_HEAD_DIM = 64
NA_ROWS = 8
NA_COLS = 16
GRID_W = 64
CONV_K = 31
N_EXPERTS = 16
N_GROUPS = 4
EXPERTS_PER_GROUP = 4
PAIRS = ((0, 1), (0, 2), (0, 3), (1, 3), (1, 2), (2, 3))
N_CLASSES = N_GROUPS * len(PAIRS)
CLASS_PAD = 32
MOE_TM = 256
SORT_BLK = 512
GATHER_TM = 256
SCALAR_UNROLL = 8


class Dims(NamedTuple):
    batch: int
    seq: int
    d: int
    ctx: int
    fw: int
    naw: int
    fe: int


def _cparams(sem, vmem_mb):
    return pltpu.CompilerParams(dimension_semantics=sem, vmem_limit_bytes=vmem_mb << 20)


def _sigmoid(x):
    return 1.0 / (1.0 + jnp.exp(-x))


def _rms_mod(x, g, scale, shift):
    ms = jnp.mean(x * x, axis=-1, keepdims=True)
    return (x * lax.rsqrt(ms + EPS) * g) * (1.0 + scale) + shift


def _mod_kernel(c_ref, w_ref, b_ref, o_ref):
    c = c_ref[...]
    s = (c * _sigmoid(c)).astype(BF16)
    o_ref[0] = jnp.dot(s, w_ref[0].astype(BF16), preferred_element_type=F32) + b_ref[0]


def adaln_mod(cvec, ada_w, ada_b, tn):
    depth, d, n = ada_w.shape
    rows = cvec.shape[0]
    return pl.pallas_call(
        _mod_kernel,
        out_shape=jax.ShapeDtypeStruct((depth, rows, n), F32),
        grid=(depth, n // tn),
        in_specs=[
            pl.BlockSpec((rows, d), lambda l, j: (0, 0)),
            pl.BlockSpec((1, d, tn), lambda l, j: (l, 0, j)),
            pl.BlockSpec((1, 1, tn), lambda l, j: (l, 0, j)),
        ],
        out_specs=pl.BlockSpec((1, rows, tn), lambda l, j: (l, 0, j)),
        compiler_params=_cparams(("parallel", "parallel"), 40),
        name="adaln_mod",
    )(cvec, ada_w, ada_b.reshape(depth, 1, n))


def _inproj_kernel(x0_ref, xn_ref, g_ref, mod0_ref, modn_ref, w_ref, o_ref, h_scr, *, shift_idx, scale_idx):
    i = pl.program_id(0)

    def norm_into(x_ref, mod_ref, slot):
        h = _rms_mod(x_ref[...], g_ref[...], mod_ref[0, scale_idx:scale_idx + 1, :],
                     mod_ref[0, shift_idx:shift_idx + 1, :])
        h_scr[slot] = h.astype(BF16)

    @pl.when(i == 0)
    def _():
        norm_into(x0_ref, mod0_ref, 0)

    def step(cur, nxt):
        o_ref[...] = jnp.dot(h_scr[cur], w_ref[...], preferred_element_type=F32).astype(o_ref.dtype)
        norm_into(xn_ref, modn_ref, nxt)

    @pl.when(i % 2 == 0)
    def _():
        step(0, 1)

    @pl.when(i % 2 == 1)
    def _():
        step(1, 0)


def norm_mod_matmul(x2d, g, mod, w, rows_per_mod, shift_idx, scale_idx, tm, n, col_blk):
    m, d = x2d.shape
    last = m // tm - 1
    kern = functools.partial(_inproj_kernel, shift_idx=shift_idx, scale_idx=scale_idx)

    def nxt(i):
        return jnp.minimum(i + 1, last)

    return pl.pallas_call(
        kern,
        out_shape=jax.ShapeDtypeStruct((m, n), BF16),
        grid=(m // tm,),
        in_specs=[
            pl.BlockSpec((tm, d), lambda i: (0, 0)),
            pl.BlockSpec((tm, d), lambda i: (nxt(i), 0)),
            pl.BlockSpec((1, d), lambda i: (0, 0)),
            pl.BlockSpec((1, N_MOD, d), lambda i: (0, 0, 0)),
            pl.BlockSpec((1, N_MOD, d), lambda i: ((nxt(i) * tm) // rows_per_mod, 0, 0)),
            pl.BlockSpec((d, n), lambda i: (0, col_blk), pipeline_mode=pl.Buffered(1)),
        ],
        out_specs=pl.BlockSpec((tm, n), lambda i: (i, 0)),
        scratch_shapes=[pltpu.VMEM((2, tm, d), BF16)],
        compiler_params=_cparams(("arbitrary",), 48),
        name="norm_mod_matmul",
    )(x2d, x2d, g.reshape(1, d), mod, mod, w)


def _cast_kernel(x_ref, o_ref):
    o_ref[...] = x_ref[...].astype(o_ref.dtype)


def cast_experts_bf16(w, layer):
    _, e, k, n = w.shape
    return pl.pallas_call(
        _cast_kernel,
        out_shape=jax.ShapeDtypeStruct((e, k, n), BF16),
        grid=(e,),
        in_specs=[pl.BlockSpec((None, 1, k, n), lambda i: (layer, i, 0, 0))],
        out_specs=pl.BlockSpec((1, k, n), lambda i: (i, 0, 0)),
        compiler_params=_cparams(("parallel",), 32),
        name="cast_experts_bf16",
    )(w)


def _fourier_kernel(u_ref, cs_ref, cl_ref, sl_ref, o_ref, xc_scr, xs_scr, *, gc):
    @pl.when(pl.program_id(1) == 0)
    def _():
        for grp in range(N_FOURIER_GROUPS):
            r = jnp.dot(u_ref[:, grp * gc:(grp + 1) * gc], cs_ref[...], preferred_element_type=F32)
            xc_scr[:, grp * gc:(grp + 1) * gc] = r[:, :gc].astype(BF16)
            xs_scr[:, grp * gc:(grp + 1) * gc] = r[:, gc:].astype(BF16)

    acc = jnp.dot(cl_ref[...], xc_scr[...], preferred_element_type=F32)
    acc = acc - jnp.dot(sl_ref[...], xs_scr[...], preferred_element_type=F32)
    o_ref[...] = acc.astype(o_ref.dtype)


def _dft_tables(length, scale):
    kn = np.outer(np.arange(length), np.arange(length)) % length
    ang = kn.astype(np.float64) * (2.0 * math.pi / length)
    return (np.cos(ang) * scale).astype(np.float32), (np.sin(ang) * scale).astype(np.float32)


def fourier_mix(p, dims, tm):
    gc = dims.fw // N_FOURIER_GROUPS
    cc, sc = _dft_tables(gc, 1.0 / math.sqrt(dims.seq * gc))
    cs = jnp.asarray(np.concatenate([cc, sc], axis=1).astype(BF16))
    cl, sl = (jnp.asarray(t.astype(BF16)) for t in _dft_tables(dims.seq, 1.0))
    kern = functools.partial(_fourier_kernel, gc=gc)
    return pl.pallas_call(
        kern,
        out_shape=jax.ShapeDtypeStruct((dims.batch * dims.seq, dims.fw), BF16),
        grid=(dims.batch, dims.seq // tm),
        in_specs=[
            pl.BlockSpec((dims.seq, dims.fw), lambda b, m: (b, 0)),
            pl.BlockSpec((gc, 2 * gc), lambda b, m: (0, 0)),
            pl.BlockSpec((tm, dims.seq), lambda b, m: (m, 0)),
            pl.BlockSpec((tm, dims.seq), lambda b, m: (m, 0)),
        ],
        out_specs=pl.BlockSpec((tm, dims.fw), lambda b, m: (b * (dims.seq // tm) + m, 0)),
        scratch_shapes=[pltpu.VMEM((dims.seq, dims.fw), BF16), pltpu.VMEM((dims.seq, dims.fw), BF16)],
        compiler_params=_cparams(("parallel", "arbitrary"), 48),
        name="fourier_mix",
    )(p, cs, cl, sl)


NA_QROWS = 2


def _na_geometry(n_rows):
    kr = min(NA_ROWS, n_rows)
    n_union = kr + NA_QROWS - 1
    steps = []
    for rp in range(n_rows // NA_QROWS):
        rows = [rp * NA_QROWS + a for a in range(NA_QROWS)]
        starts = [min(max(r - kr // 2, 0), n_rows - kr) for r in rows]
        u = min(starts[0], n_rows - n_union)
        steps.append((u, tuple((r - u, rs - u) for r, rs in zip(rows, starts))))
    variants = sorted(set(v for _, v in steps))
    return kr, n_union, [u for u, _ in steps], [variants.index(v) for _, v in steps], variants


def _na_kernel(var_ref, ustart_ref, q_ref, k_ref, v_ref, kc_ref, vc_ref, t_ref, o_ref, s_scr, *, n_union):
    del var_ref
    rp = pl.program_id(1)
    row0 = pl.multiple_of(ustart_ref[rp] * GRID_W, GRID_W)
    nloc = n_union * GRID_W
    nctx = kc_ref.shape[0]
    low = lax.broadcasted_iota(I32, (GRID_W, LANES), 1) < NA_HEAD_DIM
    nt = (((1,), (1,)), ((), ()))
    tn = (((0,), (0,)), ((), ()))
    scale = NA_HEAD_DIM ** -0.5
    n_pairs = NA_HEADS // 2

    def scores(pr):
        cols = pl.ds(pr * LANES, LANES)
        q2 = q_ref[:, cols] * scale
        parts = []
        for a in range(NA_QROWS):
            qa = q2[a * GRID_W:(a + 1) * GRID_W]
            zero = jnp.zeros_like(qa)
            parts += [jnp.where(low, qa, zero), jnp.where(low, zero, qa)]
        qm = jnp.concatenate(parts, axis=0)
        kp = k_ref[pl.ds(row0, nloc), cols]
        s_scr[pr % 2, 0:nloc, :] = (lax.dot_general(kp, qm, nt, preferred_element_type=F32)
                                     + t_ref[0, pr].astype(F32))
        s_scr[pr % 2, nloc:nloc + nctx, :] = lax.dot_general(kc_ref[:, cols], qm, nt, preferred_element_type=F32)

    def attend(pr):
        cols = pl.ds(pr * LANES, LANES)
        s = s_scr[pr % 2]
        m = jnp.max(s, axis=0, keepdims=True)
        e = jnp.exp(s - m)
        den = jnp.sum(e, axis=0, keepdims=True)
        eb = e.astype(BF16)
        vp = v_ref[pl.ds(row0, nloc), cols]
        ot = lax.dot_general(vp, eb[0:nloc], tn, preferred_element_type=F32)
        ot = ot + lax.dot_general(vc_ref[:, cols], eb[nloc:], tn, preferred_element_type=F32)
        o = (ot / den).T
        for a in range(NA_QROWS):
            base = a * 2 * GRID_W
            o_ref[a * GRID_W:(a + 1) * GRID_W, cols] = jnp.where(
                low, o[base:base + GRID_W], o[base + GRID_W:base + 2 * GRID_W]).astype(o_ref.dtype)

    scores(0)
    for pr in range(n_pairs):
        if pr + 1 < n_pairs:
            scores(pr + 1)
        attend(pr)


def _na_bias_table(rpb, n_rows):
    kr, n_union, _, _, variants = _na_geometry(n_rows)
    col = jnp.arange(GRID_W)
    col_start = jnp.clip(col - NA_COLS // 2, 0, GRID_W - NA_COLS)
    col_mask = (col[None, :] >= col_start[:, None]) & (col[None, :] < col_start[:, None] + NA_COLS)
    dc_idx = jnp.clip(col[None, :] - col[:, None] + NA_COLS - 1, 0, 2 * NA_COLS - 2)
    colbias = jnp.where(col_mask[None, None], rpb.astype(F32)[:, :, dc_idx], NEG_BIG)
    colbias = colbias.transpose(0, 1, 3, 2).astype(BF16)
    masked = jnp.full((NA_HEADS, GRID_W, GRID_W), NEG_BIG, BF16)
    out = []
    for var in variants:
        per_row = []
        for r_off, rs_off in var:
            blocks = [colbias[:, jj - r_off + NA_ROWS - 1] if rs_off <= jj < rs_off + kr else masked
                      for jj in range(n_union)]
            per_row.append(jnp.concatenate(blocks, axis=1))
        t = jnp.stack(per_row, axis=0)
        t = t.reshape(NA_QROWS, NA_HEADS // 2, 2, n_union * GRID_W, GRID_W)
        out.append(t.transpose(1, 3, 0, 2, 4).reshape(NA_HEADS // 2, n_union * GRID_W, NA_QROWS * 2 * GRID_W))
    return jnp.stack(out, axis=0)


def neighbourhood_attention(p, kvc, rpb, dims):
    n_rows = dims.seq // GRID_W
    _, n_union, ustarts, var_of_step, _ = _na_geometry(n_rows)
    n_steps = n_rows // NA_QROWS
    table = _na_bias_table(rpb, n_rows)
    naw = dims.naw
    qb = dims.fw // naw
    tq = NA_QROWS * GRID_W
    kern = functools.partial(_na_kernel, n_union=n_union)
    return pl.pallas_call(
        kern,
        out_shape=jax.ShapeDtypeStruct((dims.batch * dims.seq, naw), BF16),
        grid_spec=pltpu.PrefetchScalarGridSpec(
            num_scalar_prefetch=2,
            grid=(dims.batch, n_steps),
            in_specs=[
                pl.BlockSpec((tq, naw), lambda b, r, var, us: (b * n_steps + r, qb)),
                pl.BlockSpec((dims.seq, naw), lambda b, r, var, us: (b, qb + 1)),
                pl.BlockSpec((dims.seq, naw), lambda b, r, var, us: (b, qb + 2)),
                pl.BlockSpec((dims.ctx, naw), lambda b, r, var, us: (b, 0)),
                pl.BlockSpec((dims.ctx, naw), lambda b, r, var, us: (b, 1)),
                pl.BlockSpec((1, NA_HEADS // 2, n_union * GRID_W, 2 * tq), lambda b, r, var, us: (var[r], 0, 0, 0)),
            ],
            out_specs=pl.BlockSpec((tq, naw), lambda b, r, var, us: (b * n_steps + r, 0)),
            scratch_shapes=[pltpu.VMEM((2, n_union * GRID_W + dims.ctx, 2 * tq), F32)],
        ),
        compiler_params=_cparams(("parallel", "arbitrary"), 48),
        name="neighbourhood_attention",
    )(jnp.asarray(var_of_step, I32), jnp.asarray(ustarts, I32), p, p, p, kvc, kvc, table)


CONV_HALO = 16
CONV_RB = 128
CONV_MM = 256
SUBLANES = 8


def _glu_conv_kernel(h_ref, wa_ref, wg_ref, ba_ref, bg_ref, dw_ref, db_ref, o_ref, scr, *, seq):
    tn = o_ref.shape[1]
    scr[0:CONV_HALO, :] = jnp.zeros((CONV_HALO, tn), F32)
    scr[CONV_HALO + seq:2 * CONV_HALO + seq, :] = jnp.zeros((CONV_HALO, tn), F32)
    base = CONV_HALO - CONV_K // 2
    n_shift = (CONV_K + base + SUBLANES - 1) // SUBLANES

    def project(c):
        row0 = c * CONV_MM
        h = h_ref[pl.ds(row0, CONV_MM), :]
        a = jnp.dot(h, wa_ref[...], preferred_element_type=F32) + ba_ref[...]
        gt = jnp.dot(h, wg_ref[...], preferred_element_type=F32) + bg_ref[...]
        scr[pl.ds(row0 + CONV_HALO, CONV_MM), :] = a * _sigmoid(gt)

    def conv_block(rb):
        row0 = rb * CONV_RB
        for cb in range(tn // LANES):
            cols = pl.ds(cb * LANES, LANES)
            acc = db_ref[:, cols]
            for ph in range(SUBLANES):
                part = None
                for st in range(n_shift):
                    k = st * SUBLANES + ph - base
                    if 0 <= k < CONV_K:
                        rows = pl.ds(row0 + st * SUBLANES, CONV_RB + SUBLANES)
                        term = scr[rows, cols] * dw_ref[k:k + 1, cols]
                        part = term if part is None else part + term
                acc = acc + part[ph:ph + CONV_RB]
            o_ref[pl.ds(row0, CONV_RB), cols] = acc.astype(o_ref.dtype)

    n_mm = seq // CONV_MM
    n_rb = seq // CONV_RB
    always = pl.program_id(0) >= 0
    project(0)
    done = 0
    for c in range(n_mm):
        ready = ((c + 1) * CONV_MM - CONV_K // 2) // CONV_RB if c + 1 < n_mm else n_rb

        @pl.when(always)
        def _(c=c, done=done, ready=ready):
            if c + 1 < n_mm:
                project(c + 1)
            for rb in range(done, ready):
                conv_block(rb)

        done = ready


def glu_conv(h, w, b, dw_w, dw_b, dims, tn):
    d = h.shape[1]
    half = w.shape[1] // 2
    nblk = half // tn
    b2 = b.reshape(1, 2 * half)
    kern = functools.partial(_glu_conv_kernel, seq=dims.seq)
    return pl.pallas_call(
        kern,
        out_shape=jax.ShapeDtypeStruct((dims.batch * dims.seq, half), BF16),
        grid=(dims.batch, nblk),
        in_specs=[
            pl.BlockSpec((dims.seq, d), lambda s, j: (s, 0)),
            pl.BlockSpec((d, tn), lambda s, j: (0, j)),
            pl.BlockSpec((d, tn), lambda s, j: (0, j + nblk)),
            pl.BlockSpec((1, tn), lambda s, j: (0, j)),
            pl.BlockSpec((1, tn), lambda s, j: (0, j + nblk)),
            pl.BlockSpec((CONV_K, tn), lambda s, j: (0, j)),
            pl.BlockSpec((1, tn), lambda s, j: (0, j)),
        ],
        out_specs=pl.BlockSpec((dims.seq, tn), lambda s, j: (s, j)),
        scratch_shapes=[pltpu.VMEM((dims.seq + 2 * CONV_HALO, tn), F32)],
        compiler_params=_cparams(("parallel", "arbitrary"), 48),
        name="glu_conv",
    )(h, w, w, b2, b2, dw_w, dw_b.reshape(1, half))


def _route(logits_t, rb):
    sel = [_sigmoid(logits_t[e:e + 1, :]) + rb[e:e + 1, :] for e in range(N_EXPERTS)]
    gs = []
    for g in range(N_GROUPS):
        v0, v1, v2, v3 = sel[4 * g:4 * g + 4]
        hi1, lo1 = jnp.maximum(v0, v1), jnp.minimum(v0, v1)
        hi2, lo2 = jnp.maximum(v2, v3), jnp.minimum(v2, v3)
        gs.append(jnp.maximum(hi1, hi2) + jnp.maximum(jnp.minimum(hi1, hi2), jnp.maximum(lo1, lo2)))
    best = gs[0]
    bg = jnp.zeros(best.shape, I32)
    for g in range(1, N_GROUPS):
        upd = gs[g] > best
        bg = jnp.where(upd, g, bg)
        best = jnp.where(upd, gs[g], best)
    v = []
    for i in range(EXPERTS_PER_GROUP):
        vi = sel[i]
        for g in range(1, N_GROUPS):
            vi = jnp.where(bg == g, sel[4 * g + i], vi)
        v.append(vi)
    picked = []
    for i in range(EXPERTS_PER_GROUP):
        rank = jnp.zeros(best.shape, I32)
        for j in range(EXPERTS_PER_GROUP):
            if j == i:
                continue
            ahead = (v[j] > v[i]) | ((v[j] == v[i]) & (j < i))
            rank = rank + ahead.astype(I32)
        picked.append(rank < 2)
    code = jnp.full(best.shape, len(PAIRS) - 1, I32)
    for idx in range(len(PAIRS) - 2, -1, -1):
        a, b = PAIRS[idx]
        code = jnp.where(picked[a] & picked[b], idx, code)
    return bg * len(PAIRS) + code


def _outproj_kernel(a_ref, b_ref, lng_ref, lnb_ref, w_ref, bias_ref, x_ref, mod_ref, g_ref, rw1_ref, rw2_ref,
                    rb_ref, x1_ref, h2_ref, cls_ref, *, ln_swish):
    ka = a_ref.shape[1]
    tm = a_ref.shape[0]
    if ln_swish:
        t = jnp.concatenate([a_ref[...], b_ref[...]], axis=-1).astype(F32)
        mu = jnp.mean(t, axis=-1, keepdims=True)
        tc = t - mu
        var = jnp.mean(tc * tc, axis=-1, keepdims=True)
        z = tc * lax.rsqrt(var + EPS) * lng_ref[...] + lnb_ref[...]
        z = (z * _sigmoid(z)).astype(BF16)
        y = jnp.dot(z, w_ref[...], preferred_element_type=F32) + bias_ref[...]
    else:
        y = jnp.dot(a_ref[...], w_ref[0:ka, :], preferred_element_type=F32)
        y = y + jnp.dot(b_ref[...], w_ref[ka:, :], preferred_element_type=F32) + bias_ref[...]
    x1 = x_ref[...] + mod_ref[0, 2:3, :] * y
    x1_ref[...] = x1
    h2 = _rms_mod(x1, g_ref[...], mod_ref[0, 4:5, :], mod_ref[0, 3:4, :])
    _store_packed_rows(h2_ref, h2, tm)
    hi = h2.astype(BF16)
    lo = (h2 - hi.astype(F32)).astype(BF16)
    s = jnp.dot(hi, rw1_ref[...], preferred_element_type=F32) + jnp.dot(lo, rw2_ref[...], preferred_element_type=F32)
    st = s.T
    logits_t = st[0:N_EXPERTS, :] + st[N_EXPERTS:2 * N_EXPERTS, :]
    cls_ref[...] = _route(logits_t, rb_ref[...])


def outproj_residual_route(a, a_col, b, b_col, ln_g, ln_b, w, bias, x2d, mod, ffn_g, router_w, router_b,
                           rows_per_mod, tm, ln_swish):
    m, d = x2d.shape
    ka = kb = w.shape[0] // 2
    kin = ka + kb
    n_slab = d // (2 * LANES)
    rw_hi = router_w.astype(BF16)
    rw_lo = (router_w - rw_hi.astype(F32)).astype(BF16)
    pad = jnp.zeros((d, LANES - 2 * N_EXPERTS), BF16)
    rw1 = jnp.concatenate([rw_hi, rw_lo, pad], axis=1)
    rw2 = jnp.concatenate([rw_hi, jnp.zeros((d, N_EXPERTS), BF16), pad], axis=1)
    kern = functools.partial(_outproj_kernel, ln_swish=ln_swish)
    return pl.pallas_call(
        kern,
        out_shape=(
            jax.ShapeDtypeStruct((m, d), F32),
            jax.ShapeDtypeStruct((m * n_slab, LANES), I32),
            jax.ShapeDtypeStruct((1, m), I32),
        ),
        grid=(m // tm,),
        in_specs=[
            pl.BlockSpec((tm, ka), lambda i: (i, a_col)),
            pl.BlockSpec((tm, kb), lambda i: (i, b_col)),
            pl.BlockSpec((1, kin), lambda i: (0, 0)),
            pl.BlockSpec((1, kin), lambda i: (0, 0)),
            pl.BlockSpec((kin, d), lambda i: (0, 0)),
            pl.BlockSpec((1, d), lambda i: (0, 0)),
            pl.BlockSpec((tm, d), lambda i: (i, 0)),
            pl.BlockSpec((1, N_MOD, d), lambda i: ((i * tm) // rows_per_mod, 0, 0)),
            pl.BlockSpec((1, d), lambda i: (0, 0)),
            pl.BlockSpec((d, LANES), lambda i: (0, 0)),
            pl.BlockSpec((d, LANES), lambda i: (0, 0)),
            pl.BlockSpec((N_EXPERTS, 1), lambda i: (0, 0)),
        ],
        out_specs=(
            pl.BlockSpec((tm, d), lambda i: (i, 0)),
            pl.BlockSpec((tm * n_slab, LANES), lambda i: (i, 0)),
            pl.BlockSpec((1, tm), lambda i: (0, i)),
        ),
        compiler_params=_cparams(("parallel",), 56),
        name="outproj_residual_route",
    )(a, b, ln_g.reshape(1, kin), ln_b.reshape(1, kin), w, bias.reshape(1, d), x2d, mod, ffn_g.reshape(1, d),
      rw1, rw2, router_b.astype(F32).reshape(N_EXPERTS, 1))


def _class_onehot(cls_ref):
    blk = cls_ref.shape[1]
    return (lax.broadcasted_iota(I32, (CLASS_PAD, blk), 0) == cls_ref[...]).astype(F32)


def _count_kernel(cls_ref, cnt_ref):
    @pl.when(pl.program_id(0) == 0)
    def _():
        cnt_ref[...] = jnp.zeros_like(cnt_ref)

    cnt_ref[...] += jnp.sum(_class_onehot(cls_ref), axis=1, keepdims=True)


def _dest_kernel(cls_ref, start_ref, dest_ref, carry_scr):
    @pl.when(pl.program_id(0) == 0)
    def _():
        carry_scr[...] = jnp.zeros_like(carry_scr)

    blk = cls_ref.shape[1]
    onehot = _class_onehot(cls_ref)
    tri = (lax.broadcasted_iota(I32, (blk, blk), 0) <= lax.broadcasted_iota(I32, (blk, blk), 1)).astype(BF16)
    cum = jnp.dot(onehot.astype(BF16), tri, preferred_element_type=F32)
    pos = cum - 1.0 + carry_scr[...] + start_ref[...]
    dest_ref[...] = jnp.sum(onehot * pos, axis=0, keepdims=True).astype(I32)
    carry_scr[...] += jnp.sum(onehot, axis=1, keepdims=True)


def sort_tokens(cls, tile):
    n = cls.shape[1]
    nb = n // SORT_BLK
    counts = pl.pallas_call(
        _count_kernel,
        out_shape=jax.ShapeDtypeStruct((CLASS_PAD, 1), F32),
        grid=(nb,),
        in_specs=[pl.BlockSpec((1, SORT_BLK), lambda j: (0, j))],
        out_specs=pl.BlockSpec((CLASS_PAD, 1), lambda j: (0, 0)),
        compiler_params=_cparams(("arbitrary",), 32),
        name="class_counts",
    )(cls)
    counts = counts.reshape(CLASS_PAD).astype(I32)
    padded = ((counts + tile - 1) // tile) * tile
    starts = (jnp.cumsum(padded) - padded).astype(F32).reshape(CLASS_PAD, 1)
    dest = pl.pallas_call(
        _dest_kernel,
        out_shape=jax.ShapeDtypeStruct((1, n), I32),
        grid=(nb,),
        in_specs=[pl.BlockSpec((1, SORT_BLK), lambda j: (0, j)),
                  pl.BlockSpec((CLASS_PAD, 1), lambda j: (0, 0))],
        out_specs=pl.BlockSpec((1, SORT_BLK), lambda j: (0, j)),
        scratch_shapes=[pltpu.VMEM((CLASS_PAD, 1), F32)],
        compiler_params=_cparams(("arbitrary",), 32),
        name="sorted_positions",
    )(cls, starts)
    return dest, counts


def _inverse_kernel(dest_ref, src_ref, *, n_tok, n_rows):
    def clear(k, carry):
        for u in range(SCALAR_UNROLL):
            src_ref[k * SCALAR_UNROLL + u] = 0
        return carry

    lax.fori_loop(0, n_rows // SCALAR_UNROLL, clear, 0)

    def scatter(k, carry):
        for u in range(SCALAR_UNROLL):
            t = k * SCALAR_UNROLL + u
            src_ref[dest_ref[t]] = t
        return carry

    lax.fori_loop(0, n_tok // SCALAR_UNROLL, scatter, 0)


def inverse_map(dest, n_rows):
    n_tok = dest.shape[0]
    kern = functools.partial(_inverse_kernel, n_tok=n_tok, n_rows=n_rows)
    return pl.pallas_call(
        kern,
        out_shape=jax.ShapeDtypeStruct((n_rows,), I32),
        in_specs=[pl.BlockSpec(memory_space=pltpu.SMEM)],
        out_specs=pl.BlockSpec(memory_space=pltpu.SMEM),
        name="inverse_map",
    )(dest)


def _start_row_gather(idx_ref, base, src_hbm, buf, sem, slot, tm, s):
    def body(k, carry):
        for u in range(SCALAR_UNROLL):
            r = k * SCALAR_UNROLL + u
            pltpu.make_async_copy(
                src_hbm.at[pl.ds(pl.multiple_of(idx_ref[base + r] * s, s), s), :],
                buf.at[slot, pl.ds(pl.multiple_of(r * s, s), s), :],
                sem.at[slot]).start()
        return carry

    lax.fori_loop(0, tm // SCALAR_UNROLL, body, 0)


def _wait_row_gather(src_hbm, buf, sem, slot, tm, s):
    pltpu.make_async_copy(src_hbm.at[pl.ds(0, tm * s), :], buf.at[slot], sem.at[slot]).wait()


HI_HALF = -65536


def _store_packed_rows(ref, v, tm):
    half = v.shape[1] // 2
    n_slab = half // LANES
    for c in range(n_slab):
        a = v[:, c * LANES:(c + 1) * LANES].astype(BF16).astype(F32)
        b = v[:, half + c * LANES:half + (c + 1) * LANES].astype(BF16).astype(F32)
        word = lax.bitcast_convert_type(a, I32) | lax.shift_right_logical(lax.bitcast_convert_type(b, I32), 16)
        ref[pl.ds(c, tm, stride=n_slab), :] = word


def _load_packed_rows(ref, tm, n_slab):
    words = [ref[pl.ds(c, tm, stride=n_slab), :] for c in range(n_slab)]
    hi = [lax.bitcast_convert_type(w & HI_HALF, F32) for w in words]
    lo = [lax.bitcast_convert_type(lax.shift_left(w, 16), F32) for w in words]
    return jnp.concatenate(hi + lo, axis=-1)


def _moe_kernel(src_ref, e1_ref, e2_ref, valid_ref, h_hbm, wg1_ref, wg2_ref, wu1_ref, wu2_ref,
                wd1_ref, wd2_ref, rwt_ref, ys_ref, buf, sem, *, tm, n_slab, n_tiles):
    i = pl.program_id(0)
    slot = i % 2

    @pl.when(i == 0)
    def _():
        _start_row_gather(src_ref, 0, h_hbm, buf, sem, 0, tm, n_slab)

    nxt = jnp.minimum(i + 1, n_tiles - 1)

    @pl.when((i + 1 < n_tiles) & (valid_ref[nxt] == 1))
    def _():
        _start_row_gather(src_ref, nxt * tm, h_hbm, buf, sem, 1 - slot, tm, n_slab)

    @pl.when(valid_ref[i] == 0)
    def _():
        ys_ref[...] = jnp.zeros_like(ys_ref)

    @pl.when(valid_ref[i] == 1)
    def _():
        _wait_row_gather(h_hbm, buf, sem, slot, tm, n_slab)
        xr = _load_packed_rows(buf.at[slot], tm, n_slab)
        x = xr.astype(BF16)
        l1 = jnp.sum(xr * rwt_ref[pl.ds(e1_ref[i], 1), :], axis=-1, keepdims=True)
        l2 = jnp.sum(xr * rwt_ref[pl.ds(e2_ref[i], 1), :], axis=-1, keepdims=True)
        s1 = _sigmoid(l1)
        s2 = _sigmoid(l2)
        tot = s1 + s2

        def expert(wg_ref, wu_ref, weight):
            gt = jnp.dot(x, wg_ref[0], preferred_element_type=F32)
            up = jnp.dot(x, wu_ref[0], preferred_element_type=F32)
            return (gt * _sigmoid(gt) * up * weight).astype(BF16)

        a1 = expert(wg1_ref, wu1_ref, s1 / tot)
        a2 = expert(wg2_ref, wu2_ref, s2 / tot)
        y = jnp.dot(a1, wd1_ref[0], preferred_element_type=F32)
        y = y + jnp.dot(a2, wd2_ref[0], preferred_element_type=F32)
        _store_packed_rows(ys_ref, y, tm)


def moe_ffn(src, tile_e1, tile_e2, tile_valid, h_rows, w_gate, w_up, w_down, router_w_t, d, tm):
    n_tiles = tile_e1.shape[0]
    n_slab = d // (2 * LANES)
    fe = w_gate.shape[2]
    kern = functools.partial(_moe_kernel, tm=tm, n_slab=n_slab, n_tiles=n_tiles)

    def w_in_spec(which):
        return pl.BlockSpec((1, d, fe), lambda i, src, e1, e2, valid: ((e1, e2)[which][i], 0, 0))

    def w_out_spec(which):
        return pl.BlockSpec((1, fe, d), lambda i, src, e1, e2, valid: ((e1, e2)[which][i], 0, 0))

    return pl.pallas_call(
        kern,
        out_shape=jax.ShapeDtypeStruct((n_tiles * tm * n_slab, LANES), I32),
        grid_spec=pltpu.PrefetchScalarGridSpec(
            num_scalar_prefetch=4,
            grid=(n_tiles,),
            in_specs=[
                pl.BlockSpec(memory_space=pl.ANY),
                w_in_spec(0), w_in_spec(1), w_in_spec(0), w_in_spec(1), w_out_spec(0), w_out_spec(1),
                pl.BlockSpec((N_EXPERTS, d), lambda i, src, e1, e2, valid: (0, 0)),
            ],
            out_specs=pl.BlockSpec((tm * n_slab, LANES), lambda i, src, e1, e2, valid: (i, 0)),
            scratch_shapes=[pltpu.VMEM((2, tm * n_slab, LANES), I32), pltpu.SemaphoreType.DMA((2,))],
        ),
        compiler_params=_cparams(("arbitrary",), 56),
        name="moe_ffn",
    )(src, tile_e1, tile_e2, tile_valid, h_rows, w_gate, w_gate, w_up, w_up, w_down, w_down, router_w_t)


def _combine_kernel(dest_ref, ys_hbm, x1_ref, mod_ref, g_ref, modn_ref, *rest, tm, s, n_tiles, final):
    if final:
        o_ref, buf, sem = rest
    else:
        o_ref, hn_ref, buf, sem = rest
    i = pl.program_id(0)
    slot = i % 2

    @pl.when(i == 0)
    def _():
        _start_row_gather(dest_ref, 0, ys_hbm, buf, sem, 0, tm, s)

    @pl.when(i + 1 < n_tiles)
    def _():
        _start_row_gather(dest_ref, (i + 1) * tm, ys_hbm, buf, sem, 1 - slot, tm, s)

    _wait_row_gather(ys_hbm, buf, sem, slot, tm, s)
    y = _load_packed_rows(buf.at[slot], tm, s)
    x2 = x1_ref[...] + mod_ref[0, 5:6, :] * y
    if final:
        ms = jnp.mean(x2 * x2, axis=-1, keepdims=True)
        o_ref[...] = x2 * lax.rsqrt(ms + EPS) * g_ref[...]
    else:
        o_ref[...] = x2
        hn_ref[...] = _rms_mod(x2, g_ref[...], modn_ref[0, 1:2, :], modn_ref[0, 0:1, :]).astype(hn_ref.dtype)


def combine_residual(dest, ys, x1, mod, g, mod_next, rows_per_mod, tm, final):
    m, d = x1.shape
    s = d // (2 * LANES)
    n_tiles = m // tm
    kern = functools.partial(_combine_kernel, tm=tm, s=s, n_tiles=n_tiles, final=final)
    row_spec = pl.BlockSpec((tm, d), lambda i, dest: (i, 0))
    mod_spec = pl.BlockSpec((1, N_MOD, d), lambda i, dest: ((i * tm) // rows_per_mod, 0, 0))
    if final:
        out_shape, out_specs = jax.ShapeDtypeStruct((m, d), F32), row_spec
    else:
        out_shape = (jax.ShapeDtypeStruct((m, d), F32), jax.ShapeDtypeStruct((m, d), BF16))
        out_specs = (row_spec, row_spec)
    return pl.pallas_call(
        kern,
        out_shape=out_shape,
        grid_spec=pltpu.PrefetchScalarGridSpec(
            num_scalar_prefetch=1,
            grid=(n_tiles,),
            in_specs=[
                pl.BlockSpec(memory_space=pl.ANY),
                row_spec,
                mod_spec,
                pl.BlockSpec((1, d), lambda i, dest: (0, 0)),
                mod_spec,
            ],
            out_specs=out_specs,
            scratch_shapes=[pltpu.VMEM((2, tm * s, LANES), I32), pltpu.SemaphoreType.DMA((2,))],
        ),
        compiler_params=_cparams(("arbitrary",), 40),
        name="combine_residual",
    )(dest, ys, x1, mod, g.reshape(1, d), mod_next)


def _tile_tables(counts, n_tiles, tm):
    tiles_per_class = (counts[:N_CLASSES] + tm - 1) // tm
    tile_end = jnp.cumsum(tiles_per_class)
    total = tile_end[-1]
    idx = jnp.arange(n_tiles, dtype=I32)
    valid = idx < total
    blk = jnp.where(valid, idx, total - 1)
    cls = jnp.minimum(jnp.sum((tile_end[None, :] <= blk[:, None]).astype(I32), axis=1), N_CLASSES - 1)
    pair = jnp.asarray(PAIRS, I32)
    grp = cls // len(PAIRS)
    e1 = grp * EXPERTS_PER_GROUP + pair[cls % len(PAIRS), 0]
    e2 = grp * EXPERTS_PER_GROUP + pair[cls % len(PAIRS), 1]
    return e1.astype(I32), e2.astype(I32), valid.astype(I32)


def grouped_moe_residual(x1, h2_rows, cls, mod, router_w, w_gate, w_up, w_down, layer, g, mod_next,
                         rows_per_mod, final):
    m, d = x1.shape
    n_tiles = m // MOE_TM + N_CLASSES
    dest, counts = sort_tokens(cls, MOE_TM)
    dest = dest.reshape(m)
    e1, e2, valid = _tile_tables(counts, n_tiles, MOE_TM)
    src = inverse_map(dest, n_tiles * MOE_TM)
    ys = moe_ffn(src, e1, e2, valid, h2_rows, cast_experts_bf16(w_gate, layer), cast_experts_bf16(w_up, layer),
                 cast_experts_bf16(w_down, layer), router_w.T.astype(F32), d, MOE_TM)
    return combine_residual(dest, ys, x1, mod, g, mod_next, rows_per_mod, GATHER_TM, final)


def _forward(dims, x, c, ctx, c_ctx, ada_w, ada_b, mix_norm_g, ffn_norm_g, ev_w_in, ev_rpb, ev_w_out,
             od_w_in, od_b_in, od_dw_w, od_dw_b, od_ln_g, od_ln_b, od_w_out, od_b_out,
             router_w, router_b, moe_w_gate, moe_w_up, moe_w_down, final_norm_g, *, tiles):
    b, seq, d = dims.batch, dims.seq, dims.d
    n = b * seq
    depth = ada_w.shape[0]
    assert depth == 2, "layer 0 is the Fourier/attention mixer, layer 1 the Conformer mixer"
    x2d = x.reshape(n, d)
    ctx2d = ctx.reshape(b * dims.ctx, d)

    cvec = jnp.zeros((16, d), F32).at[:b].set(c).at[b].set(c_ctx)
    mod_all = adaln_mod(cvec, ada_w, ada_b, tiles["mod_tn"]).reshape(depth, 16, N_MOD, d)

    mod0 = mod_all[0, :b]
    modc0 = mod_all[0, b:b + 1]
    w_in = ev_w_in[0].astype(BF16)
    p = norm_mod_matmul(x2d, mix_norm_g[0], mod0, w_in, seq, 0, 1, tiles["in_tm"], w_in.shape[1], 0)
    kvc = norm_mod_matmul(ctx2d, mix_norm_g[0], modc0, w_in, b * dims.ctx, 0, 1,
                          min(tiles["in_tm"], b * dims.ctx), 2 * dims.naw, 1)
    fo = fourier_mix(p, dims, tiles["four_tm"])
    ao = neighbourhood_attention(p, kvc, ev_rpb[0], dims)
    mod1 = mod_all[1, :b]
    mix_w = ev_w_out.shape[1]
    x1, h2_rows, cls = outproj_residual_route(
        fo, 0, ao, 0, jnp.ones((mix_w,), F32), jnp.zeros((mix_w,), F32), ev_w_out[0].astype(BF16),
        jnp.zeros((d,), F32), x2d, mod0, ffn_norm_g[0], router_w, router_b, seq, tiles["out_tm"], False)
    x2, hmix1 = grouped_moe_residual(x1, h2_rows, cls, mod0, router_w, moe_w_gate, moe_w_up, moe_w_down, 0,
                                     mix_norm_g[1], mod1, seq, False)

    yc = glu_conv(hmix1, od_w_in[0].astype(BF16), od_b_in[0], od_dw_w[0], od_dw_b[0], dims, tiles["glu_tn"])
    x3, h4_rows, cls1 = outproj_residual_route(
        yc, 0, yc, 1, od_ln_g[0], od_ln_b[0], od_w_out[0].astype(BF16), od_b_out[0], x2, mod1, ffn_norm_g[1],
        router_w, router_b, seq, tiles["out_tm"], True)
    out = grouped_moe_residual(x3, h4_rows, cls1, mod1, router_w, moe_w_gate, moe_w_up, moe_w_down, 1,
                               final_norm_g, mod1, seq, True)
    return out.reshape(b, seq, d)


TILES = dict(mod_tn=1024, in_tm=256, glu_tn=256, four_tm=512, out_tm=256)


def kernel(x, c, ctx, c_ctx, ada_w, ada_b, mix_norm_g, ffn_norm_g, ev_w_in, ev_rpb, ev_w_out, od_w_in, od_b_in,
           od_dw_w, od_dw_b, od_ln_g, od_ln_b, od_w_out, od_b_out, router_w, router_b, moe_w_gate, moe_w_up,
           moe_w_down, final_norm_g):
    b, seq, d = x.shape
    dims = Dims(batch=b, seq=seq, d=d, ctx=ctx.shape[1], fw=ev_w_out.shape[1] - NA_HEADS * NA_HEAD_DIM,
                naw=NA_HEADS * NA_HEAD_DIM, fe=moe_w_gate.shape[3])
    return _forward(dims, x, c, ctx, c_ctx, ada_w, ada_b, mix_norm_g, ffn_norm_g, ev_w_in, ev_rpb, ev_w_out,
                    od_w_in, od_b_in, od_dw_w, od_dw_b, od_ln_g, od_ln_b, od_w_out, od_b_out,
                    router_w, router_b, moe_w_gate, moe_w_up, moe_w_down, final_norm_g, tiles=TILES)
```

```python
import functools
import math
from typing import NamedTuple

import jax
import jax.numpy as jnp
import numpy as np
from jax import lax
from jax.experimental import pallas as pl
from jax.experimental.pallas import tpu as pltpu

F32 = jnp.float32
BF16 = jnp.bfloat16
I32 = jnp.int32

LANES = 128
NEG_BIG = -1e30
EPS = 1e-6

N_MOD = 6
N_FOURIER_GROUPS = 4
NA_HEADS = 16
NA_HEAD_DIM = 64
NA_ROWS = 8
NA_COLS = 16
GRID_W = 64
CONV_K = 31
N_EXPERTS = 16
N_GROUPS = 4
EXPERTS_PER_GROUP = 4
PAIRS = ((0, 1), (0, 2), (0, 3), (1, 3), (1, 2), (2, 3))
N_CLASSES = N_GROUPS * len(PAIRS)
CLASS_PAD = 32
MOE_TM = 256
SORT_BLK = 512
GATHER_TM = 256
SCALAR_UNROLL = 8


class Dims(NamedTuple):
    batch: int
    seq: int
    d: int
    ctx: int
    fw: int
    naw: int
    fe: int


def _cparams(sem, vmem_mb):
    return pltpu.CompilerParams(dimension_semantics=sem, vmem_limit_bytes=vmem_mb << 20)


def _sigmoid(x):
    return 1.0 / (1.0 + jnp.exp(-x))


def _rms_mod(x, g, scale, shift):
    ms = jnp.mean(x * x, axis=-1, keepdims=True)
    return (x * lax.rsqrt(ms + EPS) * g) * (1.0 + scale) + shift


def _mod_kernel(c_ref, w_ref, b_ref, o_ref):
    c = c_ref[...]
    s = (c * _sigmoid(c)).astype(BF16)
    o_ref[0] = jnp.dot(s, w_ref[0].astype(BF16), preferred_element_type=F32) + b_ref[0]


def adaln_mod(cvec, ada_w, ada_b, tn):
    depth, d, n = ada_w.shape
    rows = cvec.shape[0]
    return pl.pallas_call(
        _mod_kernel,
        out_shape=jax.ShapeDtypeStruct((depth, rows, n), F32),
        grid=(depth, n // tn),
        in_specs=[
            pl.BlockSpec((rows, d), lambda l, j: (0, 0)),
            pl.BlockSpec((1, d, tn), lambda l, j: (l, 0, j)),
            pl.BlockSpec((1, 1, tn), lambda l, j: (l, 0, j)),
        ],
        out_specs=pl.BlockSpec((1, rows, tn), lambda l, j: (l, 0, j)),
        compiler_params=_cparams(("parallel", "parallel"), 40),
        name="adaln_mod",
    )(cvec, ada_w, ada_b.reshape(depth, 1, n))


def _inproj_kernel(x0_ref, xn_ref, g_ref, mod0_ref, modn_ref, w_ref, o_ref, h_scr, *, shift_idx, scale_idx):
    i = pl.program_id(0)

    def norm_into(x_ref, mod_ref, slot):
        h = _rms_mod(x_ref[...], g_ref[...], mod_ref[0, scale_idx:scale_idx + 1, :],
                     mod_ref[0, shift_idx:shift_idx + 1, :])
        h_scr[slot] = h.astype(BF16)

    @pl.when(i == 0)
    def _():
        norm_into(x0_ref, mod0_ref, 0)

    def step(cur, nxt):
        o_ref[...] = jnp.dot(h_scr[cur], w_ref[...], preferred_element_type=F32).astype(o_ref.dtype)
        norm_into(xn_ref, modn_ref, nxt)

    @pl.when(i % 2 == 0)
    def _():
        step(0, 1)

    @pl.when(i % 2 == 1)
    def _():
        step(1, 0)


def norm_mod_matmul(x2d, g, mod, w, rows_per_mod, shift_idx, scale_idx, tm, n, col_blk):
    m, d = x2d.shape
    last = m // tm - 1
    kern = functools.partial(_inproj_kernel, shift_idx=shift_idx, scale_idx=scale_idx)

    def nxt(i):
        return jnp.minimum(i + 1, last)

    return pl.pallas_call(
        kern,
        out_shape=jax.ShapeDtypeStruct((m, n), BF16),
        grid=(m // tm,),
        in_specs=[
            pl.BlockSpec((tm, d), lambda i: (0, 0)),
            pl.BlockSpec((tm, d), lambda i: (nxt(i), 0)),
            pl.BlockSpec((1, d), lambda i: (0, 0)),
            pl.BlockSpec((1, N_MOD, d), lambda i: (0, 0, 0)),
            pl.BlockSpec((1, N_MOD, d), lambda i: ((nxt(i) * tm) // rows_per_mod, 0, 0)),
            pl.BlockSpec((d, n), lambda i: (0, col_blk), pipeline_mode=pl.Buffered(1)),
        ],
        out_specs=pl.BlockSpec((tm, n), lambda i: (i, 0)),
        scratch_shapes=[pltpu.VMEM((2, tm, d), BF16)],
        compiler_params=_cparams(("arbitrary",), 48),
        name="norm_mod_matmul",
    )(x2d, x2d, g.reshape(1, d), mod, mod, w)


def _cast_kernel(x_ref, o_ref):
    o_ref[...] = x_ref[...].astype(o_ref.dtype)


def cast_experts_bf16(w, layer):
    _, e, k, n = w.shape
    return pl.pallas_call(
        _cast_kernel,
        out_shape=jax.ShapeDtypeStruct((e, k, n), BF16),
        grid=(e,),
        in_specs=[pl.BlockSpec((None, 1, k, n), lambda i: (layer, i, 0, 0))],
        out_specs=pl.BlockSpec((1, k, n), lambda i: (i, 0, 0)),
        compiler_params=_cparams(("parallel",), 32),
        name="cast_experts_bf16",
    )(w)


def _fourier_kernel(u_ref, cs_ref, cl_ref, sl_ref, o_ref, xc_scr, xs_scr, *, gc):
    @pl.when(pl.program_id(1) == 0)
    def _():
        for grp in range(N_FOURIER_GROUPS):
            r = jnp.dot(u_ref[:, grp * gc:(grp + 1) * gc], cs_ref[...], preferred_element_type=F32)
            xc_scr[:, grp * gc:(grp + 1) * gc] = r[:, :gc].astype(BF16)
            xs_scr[:, grp * gc:(grp + 1) * gc] = r[:, gc:].astype(BF16)

    acc = jnp.dot(cl_ref[...], xc_scr[...], preferred_element_type=F32)
    acc = acc - jnp.dot(sl_ref[...], xs_scr[...], preferred_element_type=F32)
    o_ref[...] = acc.astype(o_ref.dtype)


def _dft_tables(length, scale):
    kn = np.outer(np.arange(length), np.arange(length)) % length
    ang = kn.astype(np.float64) * (2.0 * math.pi / length)
    return (np.cos(ang) * scale).astype(np.float32), (np.sin(ang) * scale).astype(np.float32)


def fourier_mix(p, dims, tm):
    gc = dims.fw // N_FOURIER_GROUPS
    cc, sc = _dft_tables(gc, 1.0 / math.sqrt(dims.seq * gc))
    cs = jnp.asarray(np.concatenate([cc, sc], axis=1).astype(BF16))
    cl, sl = (jnp.asarray(t.astype(BF16)) for t in _dft_tables(dims.seq, 1.0))
    kern = functools.partial(_fourier_kernel, gc=gc)
    return pl.pallas_call(
        kern,
        out_shape=jax.ShapeDtypeStruct((dims.batch * dims.seq, dims.fw), BF16),
        grid=(dims.batch, dims.seq // tm),
        in_specs=[
            pl.BlockSpec((dims.seq, dims.fw), lambda b, m: (b, 0)),
            pl.BlockSpec((gc, 2 * gc), lambda b, m: (0, 0)),
            pl.BlockSpec((tm, dims.seq), lambda b, m: (m, 0)),
            pl.BlockSpec((tm, dims.seq), lambda b, m: (m, 0)),
        ],
        out_specs=pl.BlockSpec((tm, dims.fw), lambda b, m: (b * (dims.seq // tm) + m, 0)),
        scratch_shapes=[pltpu.VMEM((dims.seq, dims.fw), BF16), pltpu.VMEM((dims.seq, dims.fw), BF16)],
        compiler_params=_cparams(("parallel", "arbitrary"), 48),
        name="fourier_mix",
    )(p, cs, cl, sl)


NA_QROWS = 2


def _na_geometry(n_rows):
    kr = min(NA_ROWS, n_rows)
    n_union = kr + NA_QROWS - 1
    steps = []
    for rp in range(n_rows // NA_QROWS):
        rows = [rp * NA_QROWS + a for a in range(NA_QROWS)]
        starts = [min(max(r - kr // 2, 0), n_rows - kr) for r in rows]
        u = min(starts[0], n_rows - n_union)
        steps.append((u, tuple((r - u, rs - u) for r, rs in zip(rows, starts))))
    variants = sorted(set(v for _, v in steps))
    return kr, n_union, [u for u, _ in steps], [variants.index(v) for _, v in steps], variants


def _na_kernel(dr_ref, ustart_ref, q_ref, k_ref, v_ref, kc_ref, vc_ref, t_ref, o_ref, s_scr, *, n_union):
    rp = pl.program_id(1)
    row0 = pl.multiple_of(ustart_ref[rp] * GRID_W, GRID_W)
    nloc = n_union * GRID_W
    nctx = kc_ref.shape[0]
    low = lax.broadcasted_iota(I32, (GRID_W, LANES), 1) < NA_HEAD_DIM
    nt = (((1,), (1,)), ((), ()))
    tn = (((0,), (0,)), ((), ()))
    scale = NA_HEAD_DIM ** -0.5
    n_pairs = NA_HEADS // 2

    def scores(pr):
        cols = pl.ds(pr * LANES, LANES)
        q2 = q_ref[:, cols] * scale
        parts = []
        for a in range(NA_QROWS):
            qa = q2[a * GRID_W:(a + 1) * GRID_W]
            zero = jnp.zeros_like(qa)
            parts += [jnp.where(low, qa, zero), jnp.where(low, zero, qa)]
        qm = jnp.concatenate(parts, axis=0)
        kp = k_ref[pl.ds(row0, nloc), cols]
        loc = lax.dot_general(kp, qm, nt, preferred_element_type=F32)
        for jj in range(n_union):
            for a in range(NA_QROWS):
                bias = t_ref[pr, dr_ref[(rp * NA_QROWS + a) * n_union + jj]].astype(F32)
                rows = slice(jj * GRID_W, (jj + 1) * GRID_W)
                lanes = slice(a * 2 * GRID_W, (a + 1) * 2 * GRID_W)
                s_scr[pr % 2, rows, lanes] = loc[rows, lanes] + bias
        s_scr[pr % 2, nloc:nloc + nctx, :] = lax.dot_general(kc_ref[:, cols], qm, nt, preferred_element_type=F32)

    def attend(pr):
        cols = pl.ds(pr * LANES, LANES)
        s = s_scr[pr % 2]
        m = jnp.max(s, axis=0, keepdims=True)
        e = jnp.exp(s - m)
        den = jnp.sum(e, axis=0, keepdims=True)
        eb = e.astype(BF16)
        vp = v_ref[pl.ds(row0, nloc), cols]
        ot = lax.dot_general(vp, eb[0:nloc], tn, preferred_element_type=F32)
        ot = ot + lax.dot_general(vc_ref[:, cols], eb[nloc:], tn, preferred_element_type=F32)
        o = (ot / den).T
        for a in range(NA_QROWS):
            base = a * 2 * GRID_W
            o_ref[a * GRID_W:(a + 1) * GRID_W, cols] = jnp.where(
                low, o[base:base + GRID_W], o[base + GRID_W:base + 2 * GRID_W]).astype(o_ref.dtype)

    always = pl.program_id(0) >= 0
    scores(0)
    for pr in range(n_pairs):
        @pl.when(always)
        def _(pr=pr):
            if pr + 1 < n_pairs:
                scores(pr + 1)
            attend(pr)


N_DR = 2 * NA_ROWS - 1


def _colbias_kernel(rpb_ref, onehot_ref, neg_ref, o_ref):
    o_ref[...] = (jnp.dot(rpb_ref[...], onehot_ref[...], preferred_element_type=F32) + neg_ref[...]).astype(o_ref.dtype)


def _na_bias_blocks(rpb):
    col = np.arange(GRID_W)
    col_start = np.clip(col - NA_COLS // 2, 0, GRID_W - NA_COLS)
    col_mask = (col[None, :] >= col_start[:, None]) & (col[None, :] < col_start[:, None] + NA_COLS)
    dc_idx = np.clip(col[None, :] - col[:, None] + NA_COLS - 1, 0, 2 * NA_COLS - 2)
    n_dc = 2 * NA_COLS - 1
    onehot = np.zeros((LANES, GRID_W, GRID_W), np.float32)
    kk, qq = np.meshgrid(col, col, indexing="ij")
    onehot[dc_idx[qq, kk], kk, qq] = col_mask[qq, kk]
    neg = np.where(col_mask.T, 0.0, NEG_BIG).astype(np.float32).reshape(1, GRID_W * GRID_W)
    rows = NA_HEADS * N_DR
    rows_pad = -(-rows // LANES) * LANES
    rpb2d = jnp.zeros((rows_pad, LANES), F32).at[:rows, :n_dc].set(rpb.astype(F32).reshape(rows, n_dc))
    colbias = pl.pallas_call(
        _colbias_kernel,
        out_shape=jax.ShapeDtypeStruct((rows_pad, GRID_W * GRID_W), BF16),
        name="na_colbias",
    )(rpb2d, jnp.asarray(onehot.reshape(LANES, GRID_W * GRID_W), BF16), jnp.asarray(neg))
    colbias = colbias[:rows].reshape(NA_HEADS, N_DR, GRID_W, GRID_W)
    masked = jnp.full((NA_HEADS, 1, GRID_W, GRID_W), NEG_BIG, BF16)
    t = jnp.concatenate([colbias, masked], axis=1).reshape(NA_HEADS // 2, 2, N_DR + 1, GRID_W, GRID_W)
    return t.transpose(0, 2, 3, 1, 4).reshape(NA_HEADS // 2, N_DR + 1, GRID_W, 2 * GRID_W)


def neighbourhood_attention(p, kvc, rpb, dims):
    n_rows = dims.seq // GRID_W
    kr, n_union, ustarts, var_of_step, variants = _na_geometry(n_rows)
    n_steps = n_rows // NA_QROWS
    dr_idx = [[(jj - r_off + NA_ROWS - 1) if rs_off <= jj < rs_off + kr else N_DR
               for jj in range(n_union)] for var in (variants[v] for v in var_of_step) for r_off, rs_off in var]
    blocks = _na_bias_blocks(rpb)
    naw = dims.naw
    qb = dims.fw // naw
    tq = NA_QROWS * GRID_W
    kern = functools.partial(_na_kernel, n_union=n_union)
    return pl.pallas_call(
        kern,
        out_shape=jax.ShapeDtypeStruct((dims.batch * dims.seq, naw), BF16),
        grid_spec=pltpu.PrefetchScalarGridSpec(
            num_scalar_prefetch=2,
            grid=(dims.batch, n_steps),
            in_specs=[
                pl.BlockSpec((tq, naw), lambda b, r, dr, us: (b * n_steps + r, qb)),
                pl.BlockSpec((dims.seq, naw), lambda b, r, dr, us: (b, qb + 1)),
                pl.BlockSpec((dims.seq, naw), lambda b, r, dr, us: (b, qb + 2)),
                pl.BlockSpec((dims.ctx, naw), lambda b, r, dr, us: (b, 0)),
                pl.BlockSpec((dims.ctx, naw), lambda b, r, dr, us: (b, 1)),
                pl.BlockSpec(blocks.shape, lambda b, r, dr, us: (0, 0, 0, 0)),
            ],
            out_specs=pl.BlockSpec((tq, naw), lambda b, r, dr, us: (b * n_steps + r, 0)),
            scratch_shapes=[pltpu.VMEM((2, n_union * GRID_W + dims.ctx, 2 * tq), F32)],
        ),
        compiler_params=_cparams(("parallel", "arbitrary"), 48),
        name="neighbourhood_attention",
    )(jnp.asarray(np.asarray(dr_idx, np.int32).reshape(-1)), jnp.asarray(ustarts, I32), p, p, p, kvc, kvc, blocks)


CONV_HALO = 16
CONV_RB = 128
CONV_MM = 256
SUBLANES = 8


def _glu_conv_kernel(h_ref, wa_ref, wg_ref, ba_ref, bg_ref, dw_ref, db_ref, o_ref, scr, *, seq):
    tn = o_ref.shape[1]
    scr[0:CONV_HALO, :] = jnp.zeros((CONV_HALO, tn), F32)
    scr[CONV_HALO + seq:2 * CONV_HALO + seq, :] = jnp.zeros((CONV_HALO, tn), F32)
    base = CONV_HALO - CONV_K // 2
    n_shift = (CONV_K + base + SUBLANES - 1) // SUBLANES

    def project(c):
        row0 = c * CONV_MM
        h = h_ref[pl.ds(row0, CONV_MM), :]
        a = jnp.dot(h, wa_ref[...], preferred_element_type=F32) + ba_ref[...]
        gt = jnp.dot(h, wg_ref[...], preferred_element_type=F32) + bg_ref[...]
        scr[pl.ds(row0 + CONV_HALO, CONV_MM), :] = a * _sigmoid(gt)

    def conv_block(rb):
        row0 = rb * CONV_RB
        for cb in range(tn // LANES):
            cols = pl.ds(cb * LANES, LANES)
            acc = db_ref[:, cols]
            for ph in range(SUBLANES):
                part = None
                for st in range(n_shift):
                    k = st * SUBLANES + ph - base
                    if 0 <= k < CONV_K:
                        rows = pl.ds(row0 + st * SUBLANES, CONV_RB + SUBLANES)
                        term = scr[rows, cols] * dw_ref[k:k + 1, cols]
                        part = term if part is None else part + term
                acc = acc + part[ph:ph + CONV_RB]
            o_ref[pl.ds(row0, CONV_RB), cols] = acc.astype(o_ref.dtype)

    n_mm = seq // CONV_MM
    n_rb = seq // CONV_RB
    always = pl.program_id(0) >= 0
    project(0)
    done = 0
    for c in range(n_mm):
        ready = ((c + 1) * CONV_MM - CONV_K // 2) // CONV_RB if c + 1 < n_mm else n_rb

        @pl.when(always)
        def _(c=c, done=done, ready=ready):
            if c + 1 < n_mm:
                project(c + 1)
            for rb in range(done, ready):
                conv_block(rb)

        done = ready


def glu_conv(h, w, b, dw_w, dw_b, dims, tn):
    d = h.shape[1]
    half = w.shape[1] // 2
    nblk = half // tn
    b2 = b.reshape(1, 2 * half)
    kern = functools.partial(_glu_conv_kernel, seq=dims.seq)
    return pl.pallas_call(
        kern,
        out_shape=jax.ShapeDtypeStruct((dims.batch * dims.seq, half), BF16),
        grid=(dims.batch, nblk),
        in_specs=[
            pl.BlockSpec((dims.seq, d), lambda s, j: (s, 0)),
            pl.BlockSpec((d, tn), lambda s, j: (0, j)),
            pl.BlockSpec((d, tn), lambda s, j: (0, j + nblk)),
            pl.BlockSpec((1, tn), lambda s, j: (0, j)),
            pl.BlockSpec((1, tn), lambda s, j: (0, j + nblk)),
            pl.BlockSpec((CONV_K, tn), lambda s, j: (0, j)),
            pl.BlockSpec((1, tn), lambda s, j: (0, j)),
        ],
        out_specs=pl.BlockSpec((dims.seq, tn), lambda s, j: (s, j)),
        scratch_shapes=[pltpu.VMEM((dims.seq + 2 * CONV_HALO, tn), F32)],
        compiler_params=_cparams(("parallel", "arbitrary"), 48),
        name="glu_conv",
    )(h, w, w, b2, b2, dw_w, dw_b.reshape(1, half))


def _route(logits_t, rb):
    sel = [_sigmoid(logits_t[e:e + 1, :]) + rb[e:e + 1, :] for e in range(N_EXPERTS)]
    gs = []
    for g in range(N_GROUPS):
        v0, v1, v2, v3 = sel[4 * g:4 * g + 4]
        hi1, lo1 = jnp.maximum(v0, v1), jnp.minimum(v0, v1)
        hi2, lo2 = jnp.maximum(v2, v3), jnp.minimum(v2, v3)
        gs.append(jnp.maximum(hi1, hi2) + jnp.maximum(jnp.minimum(hi1, hi2), jnp.maximum(lo1, lo2)))
    best = gs[0]
    bg = jnp.zeros(best.shape, I32)
    for g in range(1, N_GROUPS):
        upd = gs[g] > best
        bg = jnp.where(upd, g, bg)
        best = jnp.where(upd, gs[g], best)
    v = []
    for i in range(EXPERTS_PER_GROUP):
        vi = sel[i]
        for g in range(1, N_GROUPS):
            vi = jnp.where(bg == g, sel[4 * g + i], vi)
        v.append(vi)
    picked = []
    for i in range(EXPERTS_PER_GROUP):
        rank = jnp.zeros(best.shape, I32)
        for j in range(EXPERTS_PER_GROUP):
            if j == i:
                continue
            ahead = (v[j] > v[i]) | ((v[j] == v[i]) & (j < i))
            rank = rank + ahead.astype(I32)
        picked.append(rank < 2)
    code = jnp.full(best.shape, len(PAIRS) - 1, I32)
    for idx in range(len(PAIRS) - 2, -1, -1):
        a, b = PAIRS[idx]
        code = jnp.where(picked[a] & picked[b], idx, code)
    return bg * len(PAIRS) + code


def _outproj_kernel(a_ref, b_ref, lng_ref, lnb_ref, w_ref, bias_ref, x_ref, mod_ref, g_ref, rw1_ref, rw2_ref,
                    rb_ref, x1_ref, h2_ref, cls_ref, *, ln_swish):
    ka = a_ref.shape[1]
    tm = a_ref.shape[0]
    if ln_swish:
        t = jnp.concatenate([a_ref[...], b_ref[...]], axis=-1).astype(F32)
        mu = jnp.mean(t, axis=-1, keepdims=True)
        tc = t - mu
        var = jnp.mean(tc * tc, axis=-1, keepdims=True)
        z = tc * lax.rsqrt(var + EPS) * lng_ref[...] + lnb_ref[...]
        z = (z * _sigmoid(z)).astype(BF16)
        y = jnp.dot(z, w_ref[...], preferred_element_type=F32) + bias_ref[...]
    else:
        y = jnp.dot(a_ref[...], w_ref[0:ka, :], preferred_element_type=F32)
        y = y + jnp.dot(b_ref[...], w_ref[ka:, :], preferred_element_type=F32) + bias_ref[...]
    x1 = x_ref[...] + mod_ref[0, 2:3, :] * y
    x1_ref[...] = x1
    h2 = _rms_mod(x1, g_ref[...], mod_ref[0, 4:5, :], mod_ref[0, 3:4, :])
    _store_packed_rows(h2_ref, h2, tm)
    hi = h2.astype(BF16)
    lo = (h2 - hi.astype(F32)).astype(BF16)
    s = jnp.dot(hi, rw1_ref[...], preferred_element_type=F32) + jnp.dot(lo, rw2_ref[...], preferred_element_type=F32)
    st = s.T
    logits_t = st[0:N_EXPERTS, :] + st[N_EXPERTS:2 * N_EXPERTS, :]
    cls_ref[...] = _route(logits_t, rb_ref[...])


def outproj_residual_route(a, a_col, b, b_col, ln_g, ln_b, w, bias, x2d, mod, ffn_g, router_w, router_b,
                           rows_per_mod, tm, ln_swish):
    m, d = x2d.shape
    ka = kb = w.shape[0] // 2
    kin = ka + kb
    n_slab = d // (2 * LANES)
    rw_hi = router_w.astype(BF16)
    rw_lo = (router_w - rw_hi.astype(F32)).astype(BF16)
    pad = jnp.zeros((d, LANES - 2 * N_EXPERTS), BF16)
    rw1 = jnp.concatenate([rw_hi, rw_lo, pad], axis=1)
    rw2 = jnp.concatenate([rw_hi, jnp.zeros((d, N_EXPERTS), BF16), pad], axis=1)
    kern = functools.partial(_outproj_kernel, ln_swish=ln_swish)
    return pl.pallas_call(
        kern,
        out_shape=(
            jax.ShapeDtypeStruct((m, d), F32),
            jax.ShapeDtypeStruct((m * n_slab, LANES), I32),
            jax.ShapeDtypeStruct((1, m), I32),
        ),
        grid=(m // tm,),
        in_specs=[
            pl.BlockSpec((tm, ka), lambda i: (i, a_col)),
            pl.BlockSpec((tm, kb), lambda i: (i, b_col)),
            pl.BlockSpec((1, kin), lambda i: (0, 0)),
            pl.BlockSpec((1, kin), lambda i: (0, 0)),
            pl.BlockSpec((kin, d), lambda i: (0, 0), pipeline_mode=pl.Buffered(1)),
            pl.BlockSpec((1, d), lambda i: (0, 0)),
            pl.BlockSpec((tm, d), lambda i: (i, 0)),
            pl.BlockSpec((1, N_MOD, d), lambda i: ((i * tm) // rows_per_mod, 0, 0)),
            pl.BlockSpec((1, d), lambda i: (0, 0)),
            pl.BlockSpec((d, LANES), lambda i: (0, 0)),
            pl.BlockSpec((d, LANES), lambda i: (0, 0)),
            pl.BlockSpec((N_EXPERTS, 1), lambda i: (0, 0)),
        ],
        out_specs=(
            pl.BlockSpec((tm, d), lambda i: (i, 0)),
            pl.BlockSpec((tm * n_slab, LANES), lambda i: (i, 0)),
            pl.BlockSpec((1, tm), lambda i: (0, i)),
        ),
        compiler_params=_cparams(("parallel",), 56),
        name="outproj_residual_route",
    )(a, b, ln_g.reshape(1, kin), ln_b.reshape(1, kin), w, bias.reshape(1, d), x2d, mod, ffn_g.reshape(1, d),
      rw1, rw2, router_b.astype(F32).reshape(N_EXPERTS, 1))


def _class_onehot(cls_ref):
    blk = cls_ref.shape[1]
    return (lax.broadcasted_iota(I32, (CLASS_PAD, blk), 0) == cls_ref[...]).astype(F32)


def _count_kernel(cls_ref, cnt_ref):
    @pl.when(pl.program_id(0) == 0)
    def _():
        cnt_ref[...] = jnp.zeros_like(cnt_ref)

    cnt_ref[...] += jnp.sum(_class_onehot(cls_ref), axis=1, keepdims=True)


def _dest_kernel(cls_ref, start_ref, dest_ref, carry_scr):
    @pl.when(pl.program_id(0) == 0)
    def _():
        carry_scr[...] = jnp.zeros_like(carry_scr)

    blk = cls_ref.shape[1]
    onehot = _class_onehot(cls_ref)
    tri = (lax.broadcasted_iota(I32, (blk, blk), 0) <= lax.broadcasted_iota(I32, (blk, blk), 1)).astype(BF16)
    cum = jnp.dot(onehot.astype(BF16), tri, preferred_element_type=F32)
    pos = cum - 1.0 + carry_scr[...] + start_ref[...]
    dest_ref[...] = jnp.sum(onehot * pos, axis=0, keepdims=True).astype(I32)
    carry_scr[...] += jnp.sum(onehot, axis=1, keepdims=True)


def sort_tokens(cls, tile):
    n = cls.shape[1]
    nb = n // SORT_BLK
    counts = pl.pallas_call(
        _count_kernel,
        out_shape=jax.ShapeDtypeStruct((CLASS_PAD, 1), F32),
        grid=(nb,),
        in_specs=[pl.BlockSpec((1, SORT_BLK), lambda j: (0, j))],
        out_specs=pl.BlockSpec((CLASS_PAD, 1), lambda j: (0, 0)),
        compiler_params=_cparams(("arbitrary",), 32),
        name="class_counts",
    )(cls)
    counts = counts.reshape(CLASS_PAD).astype(I32)
    padded = ((counts + tile - 1) // tile) * tile
    starts = (jnp.cumsum(padded) - padded).astype(F32).reshape(CLASS_PAD, 1)
    dest = pl.pallas_call(
        _dest_kernel,
        out_shape=jax.ShapeDtypeStruct((1, n), I32),
        grid=(nb,),
        in_specs=[pl.BlockSpec((1, SORT_BLK), lambda j: (0, j)),
                  pl.BlockSpec((CLASS_PAD, 1), lambda j: (0, 0))],
        out_specs=pl.BlockSpec((1, SORT_BLK), lambda j: (0, j)),
        scratch_shapes=[pltpu.VMEM((CLASS_PAD, 1), F32)],
        compiler_params=_cparams(("arbitrary",), 32),
        name="sorted_positions",
    )(cls, starts)
    return dest, counts


def _inverse_kernel(dest_ref, src_ref, *, n_tok, n_rows):
    def clear(k, carry):
        for u in range(SCALAR_UNROLL):
            src_ref[k * SCALAR_UNROLL + u] = 0
        return carry

    lax.fori_loop(0, n_rows // SCALAR_UNROLL, clear, 0)

    def scatter(k, carry):
        for u in range(SCALAR_UNROLL):
            t = k * SCALAR_UNROLL + u
            src_ref[dest_ref[t]] = t
        return carry

    lax.fori_loop(0, n_tok // SCALAR_UNROLL, scatter, 0)


def inverse_map(dest, n_rows):
    n_tok = dest.shape[0]
    kern = functools.partial(_inverse_kernel, n_tok=n_tok, n_rows=n_rows)
    return pl.pallas_call(
        kern,
        out_shape=jax.ShapeDtypeStruct((n_rows,), I32),
        in_specs=[pl.BlockSpec(memory_space=pltpu.SMEM)],
        out_specs=pl.BlockSpec(memory_space=pltpu.SMEM),
        name="inverse_map",
    )(dest)


def _start_row_gather(idx_ref, base, src_hbm, buf, sem, slot, tm, s):
    def body(k, carry):
        for u in range(SCALAR_UNROLL):
            r = k * SCALAR_UNROLL + u
            pltpu.make_async_copy(
                src_hbm.at[pl.ds(pl.multiple_of(idx_ref[base + r] * s, s), s), :],
                buf.at[slot, pl.ds(pl.multiple_of(r * s, s), s), :],
                sem.at[slot]).start()
        return carry

    lax.fori_loop(0, tm // SCALAR_UNROLL, body, 0)


def _wait_row_gather(src_hbm, buf, sem, slot, tm, s):
    pltpu.make_async_copy(src_hbm.at[pl.ds(0, tm * s), :], buf.at[slot], sem.at[slot]).wait()


HI_HALF = -65536


def _store_packed_rows(ref, v, tm):
    half = v.shape[1] // 2
    n_slab = half // LANES
    for c in range(n_slab):
        a = v[:, c * LANES:(c + 1) * LANES].astype(BF16).astype(F32)
        b = v[:, half + c * LANES:half + (c + 1) * LANES].astype(BF16).astype(F32)
        word = lax.bitcast_convert_type(a, I32) | lax.shift_right_logical(lax.bitcast_convert_type(b, I32), 16)
        ref[pl.ds(c, tm, stride=n_slab), :] = word


def _load_packed_rows(ref, tm, n_slab):
    words = [ref[pl.ds(c, tm, stride=n_slab), :] for c in range(n_slab)]
    hi = [lax.bitcast_convert_type(w & HI_HALF, F32) for w in words]
    lo = [lax.bitcast_convert_type(lax.shift_left(w, 16), F32) for w in words]
    return jnp.concatenate(hi + lo, axis=-1)


def _moe_kernel(src_ref, e1_ref, e2_ref, valid_ref, h_hbm, wg1_ref, wg2_ref, wu1_ref, wu2_ref,
                wd1_ref, wd2_ref, rwt_ref, ys_ref, buf, sem, *, tm, n_slab, n_tiles):
    i = pl.program_id(0)
    slot = i % 2

    @pl.when(i == 0)
    def _():
        _start_row_gather(src_ref, 0, h_hbm, buf, sem, 0, tm, n_slab)

    nxt = jnp.minimum(i + 1, n_tiles - 1)

    @pl.when((i + 1 < n_tiles) & (valid_ref[nxt] == 1))
    def _():
        _start_row_gather(src_ref, nxt * tm, h_hbm, buf, sem, 1 - slot, tm, n_slab)

    @pl.when(valid_ref[i] == 0)
    def _():
        ys_ref[...] = jnp.zeros_like(ys_ref)

    @pl.when(valid_ref[i] == 1)
    def _():
        _wait_row_gather(h_hbm, buf, sem, slot, tm, n_slab)
        xr = _load_packed_rows(buf.at[slot], tm, n_slab)
        x = xr.astype(BF16)
        l1 = jnp.sum(xr * rwt_ref[pl.ds(e1_ref[i], 1), :], axis=-1, keepdims=True)
        l2 = jnp.sum(xr * rwt_ref[pl.ds(e2_ref[i], 1), :], axis=-1, keepdims=True)
        s1 = _sigmoid(l1)
        s2 = _sigmoid(l2)
        tot = s1 + s2

        def expert(wg_ref, wu_ref, weight):
            gt = jnp.dot(x, wg_ref[0], preferred_element_type=F32)
            up = jnp.dot(x, wu_ref[0], preferred_element_type=F32)
            return (gt * _sigmoid(gt) * up * weight).astype(BF16)

        a1 = expert(wg1_ref, wu1_ref, s1 / tot)
        a2 = expert(wg2_ref, wu2_ref, s2 / tot)
        y = jnp.dot(a1, wd1_ref[0], preferred_element_type=F32)
        y = y + jnp.dot(a2, wd2_ref[0], preferred_element_type=F32)
        _store_packed_rows(ys_ref, y, tm)


def moe_ffn(src, tile_e1, tile_e2, tile_valid, h_rows, w_gate, w_up, w_down, router_w_t, d, tm):
    n_tiles = tile_e1.shape[0]
    n_slab = d // (2 * LANES)
    fe = w_gate.shape[2]
    kern = functools.partial(_moe_kernel, tm=tm, n_slab=n_slab, n_tiles=n_tiles)

    def w_in_spec(which):
        return pl.BlockSpec((1, d, fe), lambda i, src, e1, e2, valid: ((e1, e2)[which][i], 0, 0))

    def w_out_spec(which):
        return pl.BlockSpec((1, fe, d), lambda i, src, e1, e2, valid: ((e1, e2)[which][i], 0, 0))

    return pl.pallas_call(
        kern,
        out_shape=jax.ShapeDtypeStruct((n_tiles * tm * n_slab, LANES), I32),
        grid_spec=pltpu.PrefetchScalarGridSpec(
            num_scalar_prefetch=4,
            grid=(n_tiles,),
            in_specs=[
                pl.BlockSpec(memory_space=pl.ANY),
                w_in_spec(0), w_in_spec(1), w_in_spec(0), w_in_spec(1), w_out_spec(0), w_out_spec(1),
                pl.BlockSpec((N_EXPERTS, d), lambda i, src, e1, e2, valid: (0, 0)),
            ],
            out_specs=pl.BlockSpec((tm * n_slab, LANES), lambda i, src, e1, e2, valid: (i, 0)),
            scratch_shapes=[pltpu.VMEM((2, tm * n_slab, LANES), I32), pltpu.SemaphoreType.DMA((2,))],
        ),
        compiler_params=_cparams(("arbitrary",), 56),
        name="moe_ffn",
    )(src, tile_e1, tile_e2, tile_valid, h_rows, w_gate, w_gate, w_up, w_up, w_down, w_down, router_w_t)


def _combine_kernel(dest_ref, ys_hbm, x1_ref, mod_ref, g_ref, modn_ref, *rest, tm, s, n_tiles, final):
    if final:
        o_ref, buf, sem = rest
    else:
        o_ref, hn_ref, buf, sem = rest
    i = pl.program_id(0)
    slot = i % 2

    @pl.when(i == 0)
    def _():
        _start_row_gather(dest_ref, 0, ys_hbm, buf, sem, 0, tm, s)

    @pl.when(i + 1 < n_tiles)
    def _():
        _start_row_gather(dest_ref, (i + 1) * tm, ys_hbm, buf, sem, 1 - slot, tm, s)

    _wait_row_gather(ys_hbm, buf, sem, slot, tm, s)
    y = _load_packed_rows(buf.at[slot], tm, s)
    x2 = x1_ref[...] + mod_ref[0, 5:6, :] * y
    if final:
        ms = jnp.mean(x2 * x2, axis=-1, keepdims=True)
        o_ref[...] = x2 * lax.rsqrt(ms + EPS) * g_ref[...]
    else:
        o_ref[...] = x2
        hn_ref[...] = _rms_mod(x2, g_ref[...], modn_ref[0, 1:2, :], modn_ref[0, 0:1, :]).astype(hn_ref.dtype)


def combine_residual(dest, ys, x1, mod, g, mod_next, rows_per_mod, tm, final):
    m, d = x1.shape
    s = d // (2 * LANES)
    n_tiles = m // tm
    kern = functools.partial(_combine_kernel, tm=tm, s=s, n_tiles=n_tiles, final=final)
    row_spec = pl.BlockSpec((tm, d), lambda i, dest: (i, 0))
    mod_spec = pl.BlockSpec((1, N_MOD, d), lambda i, dest: ((i * tm) // rows_per_mod, 0, 0))
    if final:
        out_shape, out_specs = jax.ShapeDtypeStruct((m, d), F32), row_spec
    else:
        out_shape = (jax.ShapeDtypeStruct((m, d), F32), jax.ShapeDtypeStruct((m, d), BF16))
        out_specs = (row_spec, row_spec)
    return pl.pallas_call(
        kern,
        out_shape=out_shape,
        grid_spec=pltpu.PrefetchScalarGridSpec(
            num_scalar_prefetch=1,
            grid=(n_tiles,),
            in_specs=[
                pl.BlockSpec(memory_space=pl.ANY),
                row_spec,
                mod_spec,
                pl.BlockSpec((1, d), lambda i, dest: (0, 0)),
                mod_spec,
            ],
            out_specs=out_specs,
            scratch_shapes=[pltpu.VMEM((2, tm * s, LANES), I32), pltpu.SemaphoreType.DMA((2,))],
        ),
        compiler_params=_cparams(("arbitrary",), 40),
        name="combine_residual",
    )(dest, ys, x1, mod, g.reshape(1, d), mod_next)


def _tile_tables(counts, n_tiles, tm):
    tiles_per_class = (counts[:N_CLASSES] + tm - 1) // tm
    tile_end = jnp.cumsum(tiles_per_class)
    total = tile_end[-1]
    idx = jnp.arange(n_tiles, dtype=I32)
    valid = idx < total
    blk = jnp.where(valid, idx, total - 1)
    cls = jnp.minimum(jnp.sum((tile_end[None, :] <= blk[:, None]).astype(I32), axis=1), N_CLASSES - 1)
    pair = jnp.asarray(PAIRS, I32)
    grp = cls // len(PAIRS)
    e1 = grp * EXPERTS_PER_GROUP + pair[cls % len(PAIRS), 0]
    e2 = grp * EXPERTS_PER_GROUP + pair[cls % len(PAIRS), 1]
    return e1.astype(I32), e2.astype(I32), valid.astype(I32)


def grouped_moe_residual(x1, h2_rows, cls, mod, router_w, w_gate, w_up, w_down, layer, g, mod_next,
                         rows_per_mod, final):
    m, d = x1.shape
    n_tiles = m // MOE_TM + N_CLASSES
    dest, counts = sort_tokens(cls, MOE_TM)
    dest = dest.reshape(m)
    e1, e2, valid = _tile_tables(counts, n_tiles, MOE_TM)
    src = inverse_map(dest, n_tiles * MOE_TM)
    ys = moe_ffn(src, e1, e2, valid, h2_rows, cast_experts_bf16(w_gate, layer), cast_experts_bf16(w_up, layer),
                 cast_experts_bf16(w_down, layer), router_w.T.astype(F32), d, MOE_TM)
    return combine_residual(dest, ys, x1, mod, g, mod_next, rows_per_mod, GATHER_TM, final)


def _forward(dims, x, c, ctx, c_ctx, ada_w, ada_b, mix_norm_g, ffn_norm_g, ev_w_in, ev_rpb, ev_w_out,
             od_w_in, od_b_in, od_dw_w, od_dw_b, od_ln_g, od_ln_b, od_w_out, od_b_out,
             router_w, router_b, moe_w_gate, moe_w_up, moe_w_down, final_norm_g, *, tiles):
    b, seq, d = dims.batch, dims.seq, dims.d
    n = b * seq
    depth = ada_w.shape[0]
    assert depth == 2, "layer 0 is the Fourier/attention mixer, layer 1 the Conformer mixer"
    x2d = x.reshape(n, d)
    ctx2d = ctx.reshape(b * dims.ctx, d)

    cvec = jnp.zeros((16, d), F32).at[:b].set(c).at[b].set(c_ctx)
    mod_all = adaln_mod(cvec, ada_w, ada_b, tiles["mod_tn"]).reshape(depth, 16, N_MOD, d)

    mod0 = mod_all[0, :b]
    modc0 = mod_all[0, b:b + 1]
    w_in = ev_w_in[0].astype(BF16)
    p = norm_mod_matmul(x2d, mix_norm_g[0], mod0, w_in, seq, 0, 1, tiles["in_tm"], w_in.shape[1], 0)
    kvc = norm_mod_matmul(ctx2d, mix_norm_g[0], modc0, w_in, b * dims.ctx, 0, 1,
                          min(tiles["in_tm"], b * dims.ctx), 2 * dims.naw, 1)
    fo = fourier_mix(p, dims, tiles["four_tm"])
    ao = neighbourhood_attention(p, kvc, ev_rpb[0], dims)
    mod1 = mod_all[1, :b]
    mix_w = ev_w_out.shape[1]
    x1, h2_rows, cls = outproj_residual_route(
        fo, 0, ao, 0, jnp.ones((mix_w,), F32), jnp.zeros((mix_w,), F32), ev_w_out[0].astype(BF16),
        jnp.zeros((d,), F32), x2d, mod0, ffn_norm_g[0], router_w, router_b, seq, tiles["out_tm"], False)
    x2, hmix1 = grouped_moe_residual(x1, h2_rows, cls, mod0, router_w, moe_w_gate, moe_w_up, moe_w_down, 0,
                                     mix_norm_g[1], mod1, seq, False)

    yc = glu_conv(hmix1, od_w_in[0].astype(BF16), od_b_in[0], od_dw_w[0], od_dw_b[0], dims, tiles["glu_tn"])
    x3, h4_rows, cls1 = outproj_residual_route(
        yc, 0, yc, 1, od_ln_g[0], od_ln_b[0], od_w_out[0].astype(BF16), od_b_out[0], x2, mod1, ffn_norm_g[1],
        router_w, router_b, seq, tiles["out_tm"], True)
    out = grouped_moe_residual(x3, h4_rows, cls1, mod1, router_w, moe_w_gate, moe_w_up, moe_w_down, 1,
                               final_norm_g, mod1, seq, True)
    return out.reshape(b, seq, d)


TILES = dict(mod_tn=1024, in_tm=256, glu_tn=256, four_tm=512, out_tm=256)


def kernel(x, c, ctx, c_ctx, ada_w, ada_b, mix_norm_g, ffn_norm_g, ev_w_in, ev_rpb, ev_w_out, od_w_in, od_b_in,
           od_dw_w, od_dw_b, od_ln_g, od_ln_b, od_w_out, od_b_out, router_w, router_b, moe_w_gate, moe_w_up,
           moe_w_down, final_norm_g):
    b, seq, d = x.shape
    dims = Dims(batch=b, seq=seq, d=d, ctx=ctx.shape[1], fw=ev_w_out.shape[1] - NA_HEADS * NA_HEAD_DIM,
                naw=NA_HEADS * NA_HEAD_DIM, fe=moe_w_gate.shape[3])
    return _forward(dims, x, c, ctx, c_ctx, ada_w, ada_b, mix_norm_g, ffn_norm_g, ev_w_in, ev_rpb, ev_w_out,
                    od_w_in, od_b_in, od_dw_w, od_dw_b, od_ln_g, od_ln_b, od_w_out, od_b_out,
                    router_w, router_b, moe_w_gate, moe_w_up, moe_w_down, final_norm_g, tiles=TILES)
```

```python
import functools
import math
from typing import NamedTuple

import jax
import jax.numpy as jnp
import numpy as np
from jax import lax
from jax.experimental import pallas as pl
from jax.experimental.pallas import tpu as pltpu

F32 = jnp.float32
BF16 = jnp.bfloat16
I32 = jnp.int32

LANES = 128
NEG_BIG = -1e30
EPS = 1e-6

N_MOD = 6
N_FOURIER_GROUPS = 4
NA_HEADS = 16
NA_HEAD_DIM = 64
NA_ROWS = 8
NA_COLS = 16
GRID_W = 64
CONV_K = 31
N_EXPERTS = 16
N_GROUPS = 4
EXPERTS_PER_GROUP = 4
PAIRS = ((0, 1), (0, 2), (0, 3), (1, 3), (1, 2), (2, 3))
N_CLASSES = N_GROUPS * len(PAIRS)
CLASS_PAD = 32
MOE_TM = 256
SORT_BLK = 512
GATHER_TM = 256
SCALAR_UNROLL = 8


class Dims(NamedTuple):
    batch: int
    seq: int
    d: int
    ctx: int
    fw: int
    naw: int
    fe: int


def _cparams(sem, vmem_mb):
    return pltpu.CompilerParams(dimension_semantics=sem, vmem_limit_bytes=vmem_mb << 20)


def _sigmoid(x):
    return 1.0 / (1.0 + jnp.exp(-x))


def _rms_mod(x, g, scale, shift):
    ms = jnp.mean(x * x, axis=-1, keepdims=True)
    return (x * lax.rsqrt(ms + EPS) * g) * (1.0 + scale) + shift


def _mod_kernel(c_ref, w_ref, b_ref, o_ref):
    c = c_ref[...]
    s = (c * _sigmoid(c)).astype(BF16)
    o_ref[0] = jnp.dot(s, w_ref[0].astype(BF16), preferred_element_type=F32) + b_ref[0]


def adaln_mod(cvec, ada_w, ada_b, tn):
    depth, d, n = ada_w.shape
    rows = cvec.shape[0]
    return pl.pallas_call(
        _mod_kernel,
        out_shape=jax.ShapeDtypeStruct((depth, rows, n), F32),
        grid=(depth, n // tn),
        in_specs=[
            pl.BlockSpec((rows, d), lambda l, j: (0, 0)),
            pl.BlockSpec((1, d, tn), lambda l, j: (l, 0, j)),
            pl.BlockSpec((1, 1, tn), lambda l, j: (l, 0, j)),
        ],
        out_specs=pl.BlockSpec((1, rows, tn), lambda l, j: (l, 0, j)),
        compiler_params=_cparams(("parallel", "parallel"), 40),
        name="adaln_mod",
    )(cvec, ada_w, ada_b.reshape(depth, 1, n))


def _inproj_kernel(x0_ref, xn_ref, g_ref, mod0_ref, modn_ref, w_ref, o_ref, h_scr, *, shift_idx, scale_idx):
    i = pl.program_id(0)

    def norm_into(x_ref, mod_ref, slot):
        h = _rms_mod(x_ref[...], g_ref[...], mod_ref[0, scale_idx:scale_idx + 1, :],
                     mod_ref[0, shift_idx:shift_idx + 1, :])
        h_scr[slot] = h.astype(BF16)

    @pl.when(i == 0)
    def _():
        norm_into(x0_ref, mod0_ref, 0)

    def step(cur, nxt):
        o_ref[...] = jnp.dot(h_scr[cur], w_ref[...], preferred_element_type=F32).astype(o_ref.dtype)
        norm_into(xn_ref, modn_ref, nxt)

    @pl.when(i % 2 == 0)
    def _():
        step(0, 1)

    @pl.when(i % 2 == 1)
    def _():
        step(1, 0)


def norm_mod_matmul(x2d, g, mod, w, rows_per_mod, shift_idx, scale_idx, tm, n, col_blk):
    m, d = x2d.shape
    last = m // tm - 1
    kern = functools.partial(_inproj_kernel, shift_idx=shift_idx, scale_idx=scale_idx)

    def nxt(i):
        return jnp.minimum(i + 1, last)

    return pl.pallas_call(
        kern,
        out_shape=jax.ShapeDtypeStruct((m, n), BF16),
        grid=(m // tm,),
        in_specs=[
            pl.BlockSpec((tm, d), lambda i: (0, 0)),
            pl.BlockSpec((tm, d), lambda i: (nxt(i), 0)),
            pl.BlockSpec((1, d), lambda i: (0, 0)),
            pl.BlockSpec((1, N_MOD, d), lambda i: (0, 0, 0)),
            pl.BlockSpec((1, N_MOD, d), lambda i: ((nxt(i) * tm) // rows_per_mod, 0, 0)),
            pl.BlockSpec((d, n), lambda i: (0, col_blk), pipeline_mode=pl.Buffered(1)),
        ],
        out_specs=pl.BlockSpec((tm, n), lambda i: (i, 0)),
        scratch_shapes=[pltpu.VMEM((2, tm, d), BF16)],
        compiler_params=_cparams(("arbitrary",), 48),
        name="norm_mod_matmul",
    )(x2d, x2d, g.reshape(1, d), mod, mod, w)


def _cast_kernel(x_ref, o_ref):
    o_ref[...] = x_ref[...].astype(o_ref.dtype)


def cast_experts_bf16(w, layer):
    _, e, k, n = w.shape
    return pl.pallas_call(
        _cast_kernel,
        out_shape=jax.ShapeDtypeStruct((e, k, n), BF16),
        grid=(e,),
        in_specs=[pl.BlockSpec((None, 1, k, n), lambda i: (layer, i, 0, 0))],
        out_specs=pl.BlockSpec((1, k, n), lambda i: (i, 0, 0)),
        compiler_params=_cparams(("parallel",), 32),
        name="cast_experts_bf16",
    )(w)


def _fourier_kernel(u_ref, cs_ref, cl_ref, sl_ref, o_ref, xc_scr, xs_scr, *, gc):
    @pl.when(pl.program_id(1) == 0)
    def _():
        for grp in range(N_FOURIER_GROUPS):
            r = jnp.dot(u_ref[:, grp * gc:(grp + 1) * gc], cs_ref[...], preferred_element_type=F32)
            xc_scr[:, grp * gc:(grp + 1) * gc] = r[:, :gc].astype(BF16)
            xs_scr[:, grp * gc:(grp + 1) * gc] = r[:, gc:].astype(BF16)

    acc = jnp.dot(cl_ref[...], xc_scr[...], preferred_element_type=F32)
    acc = acc - jnp.dot(sl_ref[...], xs_scr[...], preferred_element_type=F32)
    o_ref[...] = acc.astype(o_ref.dtype)


def _dft_tables(length, scale):
    kn = np.outer(np.arange(length), np.arange(length)) % length
    ang = kn.astype(np.float64) * (2.0 * math.pi / length)
    return (np.cos(ang) * scale).astype(np.float32), (np.sin(ang) * scale).astype(np.float32)


def fourier_mix(p, dims, tm):
    gc = dims.fw // N_FOURIER_GROUPS
    cc, sc = _dft_tables(gc, 1.0 / math.sqrt(dims.seq * gc))
    cs = jnp.asarray(np.concatenate([cc, sc], axis=1).astype(BF16))
    cl, sl = (jnp.asarray(t.astype(BF16)) for t in _dft_tables(dims.seq, 1.0))
    kern = functools.partial(_fourier_kernel, gc=gc)
    return pl.pallas_call(
        kern,
        out_shape=jax.ShapeDtypeStruct((dims.batch * dims.seq, dims.fw), BF16),
        grid=(dims.batch, dims.seq // tm),
        in_specs=[
            pl.BlockSpec((dims.seq, dims.fw), lambda b, m: (b, 0)),
            pl.BlockSpec((gc, 2 * gc), lambda b, m: (0, 0)),
            pl.BlockSpec((tm, dims.seq), lambda b, m: (m, 0)),
            pl.BlockSpec((tm, dims.seq), lambda b, m: (m, 0)),
        ],
        out_specs=pl.BlockSpec((tm, dims.fw), lambda b, m: (b * (dims.seq // tm) + m, 0)),
        scratch_shapes=[pltpu.VMEM((dims.seq, dims.fw), BF16), pltpu.VMEM((dims.seq, dims.fw), BF16)],
        compiler_params=_cparams(("parallel", "arbitrary"), 48),
        name="fourier_mix",
    )(p, cs, cl, sl)


NA_QROWS = 2


def _na_geometry(n_rows):
    kr = min(NA_ROWS, n_rows)
    n_union = kr + NA_QROWS - 1
    steps = []
    for rp in range(n_rows // NA_QROWS):
        rows = [rp * NA_QROWS + a for a in range(NA_QROWS)]
        starts = [min(max(r - kr // 2, 0), n_rows - kr) for r in rows]
        u = min(starts[0], n_rows - n_union)
        steps.append((u, tuple((r - u, rs - u) for r, rs in zip(rows, starts))))
    variants = sorted(set(v for _, v in steps))
    return kr, n_union, [u for u, _ in steps], [variants.index(v) for _, v in steps], variants


def _na_kernel(dr_ref, ustart_ref, q_ref, k_ref, v_ref, kc_ref, vc_ref, t_ref, o_ref, s_scr, *, n_union):
    rp = pl.program_id(1)
    row0 = pl.multiple_of(ustart_ref[rp] * GRID_W, GRID_W)
    nloc = n_union * GRID_W
    nctx = kc_ref.shape[0]
    low = lax.broadcasted_iota(I32, (GRID_W, LANES), 1) < NA_HEAD_DIM
    nt = (((1,), (1,)), ((), ()))
    tn = (((0,), (0,)), ((), ()))
    scale = NA_HEAD_DIM ** -0.5
    n_pairs = NA_HEADS // 2

    def scores(pr):
        cols = pl.ds(pr * LANES, LANES)
        q2 = q_ref[:, cols] * scale
        parts = []
        for a in range(NA_QROWS):
            qa = q2[a * GRID_W:(a + 1) * GRID_W]
            zero = jnp.zeros_like(qa)
            parts += [jnp.where(low, qa, zero), jnp.where(low, zero, qa)]
        qm = jnp.concatenate(parts, axis=0)
        kp = k_ref[pl.ds(row0, nloc), cols]
        loc = lax.dot_general(kp, qm, nt, preferred_element_type=F32)
        for jj in range(n_union):
            for a in range(NA_QROWS):
                bias = t_ref[pr, dr_ref[(rp * NA_QROWS + a) * n_union + jj]].astype(F32)
                rows = slice(jj * GRID_W, (jj + 1) * GRID_W)
                lanes = slice(a * 2 * GRID_W, (a + 1) * 2 * GRID_W)
                s_scr[pr % 2, rows, lanes] = loc[rows, lanes] + bias
        s_scr[pr % 2, nloc:nloc + nctx, :] = lax.dot_general(kc_ref[:, cols], qm, nt, preferred_element_type=F32)

    def attend(pr):
        cols = pl.ds(pr * LANES, LANES)
        s = s_scr[pr % 2]
        m = jnp.max(s, axis=0, keepdims=True)
        e = jnp.exp(s - m)
        den = jnp.sum(e, axis=0, keepdims=True)
        eb = e.astype(BF16)
        vp = v_ref[pl.ds(row0, nloc), cols]
        ot = lax.dot_general(vp, eb[0:nloc], tn, preferred_element_type=F32)
        ot = ot + lax.dot_general(vc_ref[:, cols], eb[nloc:], tn, preferred_element_type=F32)
        o = (ot / den).T
        for a in range(NA_QROWS):
            base = a * 2 * GRID_W
            o_ref[a * GRID_W:(a + 1) * GRID_W, cols] = jnp.where(
                low, o[base:base + GRID_W], o[base + GRID_W:base + 2 * GRID_W]).astype(o_ref.dtype)

    always = pl.program_id(0) >= 0
    scores(0)
    for pr in range(n_pairs):
        @pl.when(always)
        def _(pr=pr):
            if pr + 1 < n_pairs:
                scores(pr + 1)
            attend(pr)


N_DR = 2 * NA_ROWS - 1


def _colbias_kernel(rpb_ref, onehot_ref, neg_ref, o_ref):
    o_ref[...] = (jnp.dot(rpb_ref[...], onehot_ref[...], preferred_element_type=F32) + neg_ref[...]).astype(o_ref.dtype)


def _na_bias_blocks(rpb):
    col = np.arange(GRID_W)
    col_start = np.clip(col - NA_COLS // 2, 0, GRID_W - NA_COLS)
    col_mask = (col[None, :] >= col_start[:, None]) & (col[None, :] < col_start[:, None] + NA_COLS)
    dc_idx = np.clip(col[None, :] - col[:, None] + NA_COLS - 1, 0, 2 * NA_COLS - 2)
    n_dc = 2 * NA_COLS - 1
    onehot = np.zeros((LANES, GRID_W, GRID_W), np.float32)
    kk, qq = np.meshgrid(col, col, indexing="ij")
    onehot[dc_idx[qq, kk], kk, qq] = col_mask[qq, kk]
    neg = np.where(col_mask.T, 0.0, NEG_BIG).astype(np.float32).reshape(1, GRID_W * GRID_W)
    rows = NA_HEADS * N_DR
    rows_pad = -(-rows // LANES) * LANES
    rpb2d = jnp.zeros((rows_pad, LANES), F32).at[:rows, :n_dc].set(rpb.astype(F32).reshape(rows, n_dc))
    colbias = pl.pallas_call(
        _colbias_kernel,
        out_shape=jax.ShapeDtypeStruct((rows_pad, GRID_W * GRID_W), BF16),
        name="na_colbias",
    )(rpb2d, jnp.asarray(onehot.reshape(LANES, GRID_W * GRID_W), BF16), jnp.asarray(neg))
    colbias = colbias[:rows].reshape(NA_HEADS, N_DR, GRID_W, GRID_W)
    masked = jnp.full((NA_HEADS, 1, GRID_W, GRID_W), NEG_BIG, BF16)
    t = jnp.concatenate([colbias, masked], axis=1).reshape(NA_HEADS // 2, 2, N_DR + 1, GRID_W, GRID_W)
    return t.transpose(0, 2, 3, 1, 4).reshape(NA_HEADS // 2, N_DR + 1, GRID_W, 2 * GRID_W)


def neighbourhood_attention(p, kvc, rpb, dims):
    n_rows = dims.seq // GRID_W
    kr, n_union, ustarts, var_of_step, variants = _na_geometry(n_rows)
    n_steps = n_rows // NA_QROWS
    dr_idx = [[(jj - r_off + NA_ROWS - 1) if rs_off <= jj < rs_off + kr else N_DR
               for jj in range(n_union)] for var in (variants[v] for v in var_of_step) for r_off, rs_off in var]
    blocks = _na_bias_blocks(rpb)
    naw = dims.naw
    qb = dims.fw // naw
    tq = NA_QROWS * GRID_W
    kern = functools.partial(_na_kernel, n_union=n_union)
    return pl.pallas_call(
        kern,
        out_shape=jax.ShapeDtypeStruct((dims.batch * dims.seq, naw), BF16),
        grid_spec=pltpu.PrefetchScalarGridSpec(
            num_scalar_prefetch=2,
            grid=(dims.batch, n_steps),
            in_specs=[
                pl.BlockSpec((tq, naw), lambda b, r, dr, us: (b * n_steps + r, qb)),
                pl.BlockSpec((dims.seq, naw), lambda b, r, dr, us: (b, qb + 1)),
                pl.BlockSpec((dims.seq, naw), lambda b, r, dr, us: (b, qb + 2)),
                pl.BlockSpec((dims.ctx, naw), lambda b, r, dr, us: (b, 0)),
                pl.BlockSpec((dims.ctx, naw), lambda b, r, dr, us: (b, 1)),
                pl.BlockSpec(blocks.shape, lambda b, r, dr, us: (0, 0, 0, 0)),
            ],
            out_specs=pl.BlockSpec((tq, naw), lambda b, r, dr, us: (b * n_steps + r, 0)),
            scratch_shapes=[pltpu.VMEM((2, n_union * GRID_W + dims.ctx, 2 * tq), F32)],
        ),
        compiler_params=_cparams(("parallel", "arbitrary"), 48),
        name="neighbourhood_attention",
    )(jnp.asarray(np.asarray(dr_idx, np.int32).reshape(-1)), jnp.asarray(ustarts, I32), p, p, p, kvc, kvc, blocks)


CONV_HALO = 16
CONV_RB = 128
CONV_MM = 256
SUBLANES = 8


def _glu_conv_kernel(h_ref, wa_ref, wg_ref, ba_ref, bg_ref, dw_ref, db_ref, o_ref, scr, *, seq):
    tn = o_ref.shape[1]
    scr[0:CONV_HALO, :] = jnp.zeros((CONV_HALO, tn), F32)
    scr[CONV_HALO + seq:2 * CONV_HALO + seq, :] = jnp.zeros((CONV_HALO, tn), F32)
    base = CONV_HALO - CONV_K // 2
    n_shift = (CONV_K + base + SUBLANES - 1) // SUBLANES

    def project(c):
        row0 = c * CONV_MM
        h = h_ref[pl.ds(row0, CONV_MM), :]
        a = jnp.dot(h, wa_ref[...], preferred_element_type=F32) + ba_ref[...]
        gt = jnp.dot(h, wg_ref[...], preferred_element_type=F32) + bg_ref[...]
        scr[pl.ds(row0 + CONV_HALO, CONV_MM), :] = a * _sigmoid(gt)

    def conv_block(rb):
        row0 = rb * CONV_RB
        for cb in range(tn // LANES):
            cols = pl.ds(cb * LANES, LANES)
            acc = db_ref[:, cols]
            for ph in range(SUBLANES):
                part = None
                for st in range(n_shift):
                    k = st * SUBLANES + ph - base
                    if 0 <= k < CONV_K:
                        rows = pl.ds(row0 + st * SUBLANES, CONV_RB + SUBLANES)
                        term = scr[rows, cols] * dw_ref[k:k + 1, cols]
                        part = term if part is None else part + term
                acc = acc + part[ph:ph + CONV_RB]
            o_ref[pl.ds(row0, CONV_RB), cols] = acc.astype(o_ref.dtype)

    n_mm = seq // CONV_MM
    n_rb = seq // CONV_RB
    always = pl.program_id(0) >= 0
    project(0)
    done = 0
    for c in range(n_mm):
        ready = ((c + 1) * CONV_MM - CONV_K // 2) // CONV_RB if c + 1 < n_mm else n_rb

        @pl.when(always)
        def _(c=c, done=done, ready=ready):
            if c + 1 < n_mm:
                project(c + 1)
            for rb in range(done, ready):
                conv_block(rb)

        done = ready


def glu_conv(h, w, b, dw_w, dw_b, dims, tn):
    d = h.shape[1]
    half = w.shape[1] // 2
    nblk = half // tn
    b2 = b.reshape(1, 2 * half)
    kern = functools.partial(_glu_conv_kernel, seq=dims.seq)
    return pl.pallas_call(
        kern,
        out_shape=jax.ShapeDtypeStruct((dims.batch * dims.seq, half), BF16),
        grid=(dims.batch, nblk),
        in_specs=[
            pl.BlockSpec((dims.seq, d), lambda s, j: (s, 0)),
            pl.BlockSpec((d, tn), lambda s, j: (0, j)),
            pl.BlockSpec((d, tn), lambda s, j: (0, j + nblk)),
            pl.BlockSpec((1, tn), lambda s, j: (0, j)),
            pl.BlockSpec((1, tn), lambda s, j: (0, j + nblk)),
            pl.BlockSpec((CONV_K, tn), lambda s, j: (0, j)),
            pl.BlockSpec((1, tn), lambda s, j: (0, j)),
        ],
        out_specs=pl.BlockSpec((dims.seq, tn), lambda s, j: (s, j)),
        scratch_shapes=[pltpu.VMEM((dims.seq + 2 * CONV_HALO, tn), F32)],
        compiler_params=_cparams(("parallel", "arbitrary"), 48),
        name="glu_conv",
    )(h, w, w, b2, b2, dw_w, dw_b.reshape(1, half))


def _route(logits_t, rb):
    sel = [_sigmoid(logits_t[e:e + 1, :]) + rb[e:e + 1, :] for e in range(N_EXPERTS)]
    gs = []
    for g in range(N_GROUPS):
        v0, v1, v2, v3 = sel[4 * g:4 * g + 4]
        hi1, lo1 = jnp.maximum(v0, v1), jnp.minimum(v0, v1)
        hi2, lo2 = jnp.maximum(v2, v3), jnp.minimum(v2, v3)
        gs.append(jnp.maximum(hi1, hi2) + jnp.maximum(jnp.minimum(hi1, hi2), jnp.maximum(lo1, lo2)))
    best = gs[0]
    bg = jnp.zeros(best.shape, I32)
    for g in range(1, N_GROUPS):
        upd = gs[g] > best
        bg = jnp.where(upd, g, bg)
        best = jnp.where(upd, gs[g], best)
    v = []
    for i in range(EXPERTS_PER_GROUP):
        vi = sel[i]
        for g in range(1, N_GROUPS):
            vi = jnp.where(bg == g, sel[4 * g + i], vi)
        v.append(vi)
    picked = []
    for i in range(EXPERTS_PER_GROUP):
        rank = jnp.zeros(best.shape, I32)
        for j in range(EXPERTS_PER_GROUP):
            if j == i:
                continue
            ahead = (v[j] > v[i]) | ((v[j] == v[i]) & (j < i))
            rank = rank + ahead.astype(I32)
        picked.append(rank < 2)
    code = jnp.full(best.shape, len(PAIRS) - 1, I32)
    for idx in range(len(PAIRS) - 2, -1, -1):
        a, b = PAIRS[idx]
        code = jnp.where(picked[a] & picked[b], idx, code)
    return bg * len(PAIRS) + code


def _outproj_kernel(a_ref, b_ref, lng_ref, lnb_ref, w_ref, bias_ref, x_ref, mod_ref, g_ref, rw1_ref, rw2_ref,
                    rb_ref, x1_ref, h2_ref, cls_ref, *, ln_swish):
    ka = a_ref.shape[1]
    tm = a_ref.shape[0]
    if ln_swish:
        t = jnp.concatenate([a_ref[...], b_ref[...]], axis=-1).astype(F32)
        mu = jnp.mean(t, axis=-1, keepdims=True)
        tc = t - mu
        var = jnp.mean(tc * tc, axis=-1, keepdims=True)
        z = tc * lax.rsqrt(var + EPS) * lng_ref[...] + lnb_ref[...]
        z = (z * _sigmoid(z)).astype(BF16)
        y = jnp.dot(z, w_ref[...], preferred_element_type=F32) + bias_ref[...]
    else:
        y = jnp.dot(a_ref[...], w_ref[0:ka, :], preferred_element_type=F32)
        y = y + jnp.dot(b_ref[...], w_ref[ka:, :], preferred_element_type=F32) + bias_ref[...]
    x1 = x_ref[...] + mod_ref[0, 2:3, :] * y
    x1_ref[...] = x1
    h2 = _rms_mod(x1, g_ref[...], mod_ref[0, 4:5, :], mod_ref[0, 3:4, :])
    _store_packed_rows(h2_ref, h2, tm)
    hi = h2.astype(BF16)
    lo = (h2 - hi.astype(F32)).astype(BF16)
    s = jnp.dot(hi, rw1_ref[...], preferred_element_type=F32) + jnp.dot(lo, rw2_ref[...], preferred_element_type=F32)
    st = s.T
    logits_t = st[0:N_EXPERTS, :] + st[N_EXPERTS:2 * N_EXPERTS, :]
    cls_ref[...] = _route(logits_t, rb_ref[...])


def outproj_residual_route(a, a_col, b, b_col, ln_g, ln_b, w, bias, x2d, mod, ffn_g, router_w, router_b,
                           rows_per_mod, tm, ln_swish):
    m, d = x2d.shape
    ka = kb = w.shape[0] // 2
    kin = ka + kb
    n_slab = d // (2 * LANES)
    rw_hi = router_w.astype(BF16)
    rw_lo = (router_w - rw_hi.astype(F32)).astype(BF16)
    pad = jnp.zeros((d, LANES - 2 * N_EXPERTS), BF16)
    rw1 = jnp.concatenate([rw_hi, rw_lo, pad], axis=1)
    rw2 = jnp.concatenate([rw_hi, jnp.zeros((d, N_EXPERTS), BF16), pad], axis=1)
    kern = functools.partial(_outproj_kernel, ln_swish=ln_swish)
    return pl.pallas_call(
        kern,
        out_shape=(
            jax.ShapeDtypeStruct((m, d), F32),
            jax.ShapeDtypeStruct((m * n_slab, LANES), I32),
            jax.ShapeDtypeStruct((1, m), I32),
        ),
        grid=(m // tm,),
        in_specs=[
            pl.BlockSpec((tm, ka), lambda i: (i, a_col)),
            pl.BlockSpec((tm, kb), lambda i: (i, b_col)),
            pl.BlockSpec((1, kin), lambda i: (0, 0)),
            pl.BlockSpec((1, kin), lambda i: (0, 0)),
            pl.BlockSpec((kin, d), lambda i: (0, 0), pipeline_mode=pl.Buffered(1)),
            pl.BlockSpec((1, d), lambda i: (0, 0)),
            pl.BlockSpec((tm, d), lambda i: (i, 0)),
            pl.BlockSpec((1, N_MOD, d), lambda i: ((i * tm) // rows_per_mod, 0, 0)),
            pl.BlockSpec((1, d), lambda i: (0, 0)),
            pl.BlockSpec((d, LANES), lambda i: (0, 0)),
            pl.BlockSpec((d, LANES), lambda i: (0, 0)),
            pl.BlockSpec((N_EXPERTS, 1), lambda i: (0, 0)),
        ],
        out_specs=(
            pl.BlockSpec((tm, d), lambda i: (i, 0)),
            pl.BlockSpec((tm * n_slab, LANES), lambda i: (i, 0)),
            pl.BlockSpec((1, tm), lambda i: (0, i)),
        ),
        compiler_params=_cparams(("parallel",), 56),
        name="outproj_residual_route",
    )(a, b, ln_g.reshape(1, kin), ln_b.reshape(1, kin), w, bias.reshape(1, d), x2d, mod, ffn_g.reshape(1, d),
      rw1, rw2, router_b.astype(F32).reshape(N_EXPERTS, 1))


def _class_onehot(cls_ref):
    blk = cls_ref.shape[1]
    return (lax.broadcasted_iota(I32, (CLASS_PAD, blk), 0) == cls_ref[...]).astype(F32)


def _count_kernel(cls_ref, cnt_ref):
    @pl.when(pl.program_id(0) == 0)
    def _():
        cnt_ref[...] = jnp.zeros_like(cnt_ref)

    cnt_ref[...] += jnp.sum(_class_onehot(cls_ref), axis=1, keepdims=True)


def _dest_kernel(cls_ref, start_ref, dest_ref, carry_scr):
    @pl.when(pl.program_id(0) == 0)
    def _():
        carry_scr[...] = jnp.zeros_like(carry_scr)

    blk = cls_ref.shape[1]
    onehot = _class_onehot(cls_ref)
    tri = (lax.broadcasted_iota(I32, (blk, blk), 0) <= lax.broadcasted_iota(I32, (blk, blk), 1)).astype(BF16)
    cum = jnp.dot(onehot.astype(BF16), tri, preferred_element_type=F32)
    pos = cum - 1.0 + carry_scr[...] + start_ref[...]
    dest_ref[...] = jnp.sum(onehot * pos, axis=0, keepdims=True).astype(I32)
    carry_scr[...] += jnp.sum(onehot, axis=1, keepdims=True)


def sort_tokens(cls, tile):
    n = cls.shape[1]
    nb = n // SORT_BLK
    counts = pl.pallas_call(
        _count_kernel,
        out_shape=jax.ShapeDtypeStruct((CLASS_PAD, 1), F32),
        grid=(nb,),
        in_specs=[pl.BlockSpec((1, SORT_BLK), lambda j: (0, j))],
        out_specs=pl.BlockSpec((CLASS_PAD, 1), lambda j: (0, 0)),
        compiler_params=_cparams(("arbitrary",), 32),
        name="class_counts",
    )(cls)
    counts = counts.reshape(CLASS_PAD).astype(I32)
    padded = ((counts + tile - 1) // tile) * tile
    starts = (jnp.cumsum(padded) - padded).astype(F32).reshape(CLASS_PAD, 1)
    dest = pl.pallas_call(
        _dest_kernel,
        out_shape=jax.ShapeDtypeStruct((1, n), I32),
        grid=(nb,),
        in_specs=[pl.BlockSpec((1, SORT_BLK), lambda j: (0, j)),
                  pl.BlockSpec((CLASS_PAD, 1), lambda j: (0, 0))],
        out_specs=pl.BlockSpec((1, SORT_BLK), lambda j: (0, j)),
        scratch_shapes=[pltpu.VMEM((CLASS_PAD, 1), F32)],
        compiler_params=_cparams(("arbitrary",), 32),
        name="sorted_positions",
    )(cls, starts)
    return dest, counts


def _inverse_kernel(dest_ref, pad_lo_ref, pad_hi_ref, src_ref, *, n_tok):
    def clear(r, carry):
        src_ref[r] = 0
        return carry

    for c in range(N_CLASSES + 1):
        lax.fori_loop(pad_lo_ref[c], pad_hi_ref[c], clear, 0)

    def scatter(k, carry):
        for u in range(SCALAR_UNROLL):
            t = k * SCALAR_UNROLL + u
            src_ref[dest_ref[t]] = t
        return carry

    lax.fori_loop(0, n_tok // SCALAR_UNROLL, scatter, 0)


def inverse_map(dest, counts, tile, n_rows):
    n_tok = dest.shape[0]
    padded = ((counts[:N_CLASSES] + tile - 1) // tile) * tile
    ends = jnp.cumsum(padded)
    pad_lo = jnp.concatenate([ends - padded + counts[:N_CLASSES], ends[-1:]]).astype(I32)
    pad_hi = jnp.concatenate([ends, jnp.full((1,), n_rows, I32)]).astype(I32)
    kern = functools.partial(_inverse_kernel, n_tok=n_tok)
    smem = pl.BlockSpec(memory_space=pltpu.SMEM)
    return pl.pallas_call(
        kern,
        out_shape=jax.ShapeDtypeStruct((n_rows,), I32),
        in_specs=[smem, smem, smem],
        out_specs=smem,
        name="inverse_map",
    )(dest, pad_lo, pad_hi)


def _start_row_gather(idx_ref, base, src_hbm, buf, sem, slot, tm, s):
    def body(k, carry):
        for u in range(SCALAR_UNROLL):
            r = k * SCALAR_UNROLL + u
            pltpu.make_async_copy(
                src_hbm.at[pl.ds(pl.multiple_of(idx_ref[base + r] * s, s), s), :],
                buf.at[slot, pl.ds(pl.multiple_of(r * s, s), s), :],
                sem.at[slot]).start()
        return carry

    lax.fori_loop(0, tm // SCALAR_UNROLL, body, 0)


def _wait_row_gather(src_hbm, buf, sem, slot, tm, s):
    pltpu.make_async_copy(src_hbm.at[pl.ds(0, tm * s), :], buf.at[slot], sem.at[slot]).wait()


HI_HALF = -65536


def _store_packed_rows(ref, v, tm):
    half = v.shape[1] // 2
    n_slab = half // LANES
    for c in range(n_slab):
        a = v[:, c * LANES:(c + 1) * LANES].astype(BF16).astype(F32)
        b = v[:, half + c * LANES:half + (c + 1) * LANES].astype(BF16).astype(F32)
        word = lax.bitcast_convert_type(a, I32) | lax.shift_right_logical(lax.bitcast_convert_type(b, I32), 16)
        ref[pl.ds(c, tm, stride=n_slab), :] = word


def _load_packed_rows(ref, tm, n_slab):
    words = [ref[pl.ds(c, tm, stride=n_slab), :] for c in range(n_slab)]
    hi = [lax.bitcast_convert_type(w & HI_HALF, F32) for w in words]
    lo = [lax.bitcast_convert_type(lax.shift_left(w, 16), F32) for w in words]
    return jnp.concatenate(hi + lo, axis=-1)


def _moe_kernel(src_ref, e1_ref, e2_ref, valid_ref, h_hbm, wg1_ref, wg2_ref, wu1_ref, wu2_ref,
                wd1_ref, wd2_ref, rwt_ref, ys_ref, buf, sem, *, tm, n_slab, n_tiles):
    i = pl.program_id(0)
    slot = i % 2

    @pl.when(i == 0)
    def _():
        _start_row_gather(src_ref, 0, h_hbm, buf, sem, 0, tm, n_slab)

    nxt = jnp.minimum(i + 1, n_tiles - 1)

    @pl.when((i + 1 < n_tiles) & (valid_ref[nxt] == 1))
    def _():
        _start_row_gather(src_ref, nxt * tm, h_hbm, buf, sem, 1 - slot, tm, n_slab)

    @pl.when(valid_ref[i] == 0)
    def _():
        ys_ref[...] = jnp.zeros_like(ys_ref)

    @pl.when(valid_ref[i] == 1)
    def _():
        _wait_row_gather(h_hbm, buf, sem, slot, tm, n_slab)
        xr = _load_packed_rows(buf.at[slot], tm, n_slab)
        x = xr.astype(BF16)
        l1 = jnp.sum(xr * rwt_ref[pl.ds(e1_ref[i], 1), :], axis=-1, keepdims=True)
        l2 = jnp.sum(xr * rwt_ref[pl.ds(e2_ref[i], 1), :], axis=-1, keepdims=True)
        s1 = _sigmoid(l1)
        s2 = _sigmoid(l2)
        tot = s1 + s2

        def expert(wg_ref, wu_ref, weight):
            gt = jnp.dot(x, wg_ref[0], preferred_element_type=F32)
            up = jnp.dot(x, wu_ref[0], preferred_element_type=F32)
            return gt * _sigmoid(gt) * up * weight

        a1 = expert(wg1_ref, wu1_ref, s1 / tot)
        a2 = expert(wg2_ref, wu2_ref, s2 / tot)
        y = jnp.dot(a1, wd1_ref[0], preferred_element_type=F32)
        y = y + jnp.dot(a2, wd2_ref[0], preferred_element_type=F32)
        _store_packed_rows(ys_ref, y, tm)


def moe_ffn(src, tile_e1, tile_e2, tile_valid, h_rows, w_gate, w_up, w_down, layer, router_w_t, d, tm):
    n_tiles = tile_e1.shape[0]
    n_slab = d // (2 * LANES)
    fe = w_gate.shape[2]
    kern = functools.partial(_moe_kernel, tm=tm, n_slab=n_slab, n_tiles=n_tiles)

    def w_in_spec(which):
        return pl.BlockSpec((1, d, fe), lambda i, src, e1, e2, valid: ((e1, e2)[which][i], 0, 0))

    def w_out_spec(which):
        return pl.BlockSpec((None, 1, fe, d), lambda i, src, e1, e2, valid: (layer, (e1, e2)[which][i], 0, 0))

    return pl.pallas_call(
        kern,
        out_shape=jax.ShapeDtypeStruct((n_tiles * tm * n_slab, LANES), I32),
        grid_spec=pltpu.PrefetchScalarGridSpec(
            num_scalar_prefetch=4,
            grid=(n_tiles,),
            in_specs=[
                pl.BlockSpec(memory_space=pl.ANY),
                w_in_spec(0), w_in_spec(1), w_in_spec(0), w_in_spec(1), w_out_spec(0), w_out_spec(1),
                pl.BlockSpec((N_EXPERTS, d), lambda i, src, e1, e2, valid: (0, 0)),
            ],
            out_specs=pl.BlockSpec((tm * n_slab, LANES), lambda i, src, e1, e2, valid: (i, 0)),
            scratch_shapes=[pltpu.VMEM((2, tm * n_slab, LANES), I32), pltpu.SemaphoreType.DMA((2,))],
        ),
        compiler_params=_cparams(("arbitrary",), 56),
        name="moe_ffn",
    )(src, tile_e1, tile_e2, tile_valid, h_rows, w_gate, w_gate, w_up, w_up, w_down, w_down, router_w_t)


def _combine_kernel(dest_ref, ys_hbm, x1_ref, mod_ref, g_ref, modn_ref, *rest, tm, s, n_tiles, final):
    if final:
        o_ref, buf, sem = rest
    else:
        o_ref, hn_ref, buf, sem = rest
    i = pl.program_id(0)
    slot = i % 2

    @pl.when(i == 0)
    def _():
        _start_row_gather(dest_ref, 0, ys_hbm, buf, sem, 0, tm, s)

    @pl.when(i + 1 < n_tiles)
    def _():
        _start_row_gather(dest_ref, (i + 1) * tm, ys_hbm, buf, sem, 1 - slot, tm, s)

    _wait_row_gather(ys_hbm, buf, sem, slot, tm, s)
    y = _load_packed_rows(buf.at[slot], tm, s)
    x2 = x1_ref[...] + mod_ref[0, 5:6, :] * y
    if final:
        ms = jnp.mean(x2 * x2, axis=-1, keepdims=True)
        o_ref[...] = x2 * lax.rsqrt(ms + EPS) * g_ref[...]
    else:
        o_ref[...] = x2
        hn_ref[...] = _rms_mod(x2, g_ref[...], modn_ref[0, 1:2, :], modn_ref[0, 0:1, :]).astype(hn_ref.dtype)


def combine_residual(dest, ys, x1, mod, g, mod_next, rows_per_mod, tm, final):
    m, d = x1.shape
    s = d // (2 * LANES)
    n_tiles = m // tm
    kern = functools.partial(_combine_kernel, tm=tm, s=s, n_tiles=n_tiles, final=final)
    row_spec = pl.BlockSpec((tm, d), lambda i, dest: (i, 0))
    mod_spec = pl.BlockSpec((1, N_MOD, d), lambda i, dest: ((i * tm) // rows_per_mod, 0, 0))
    if final:
        out_shape, out_specs = jax.ShapeDtypeStruct((m, d), F32), row_spec
    else:
        out_shape = (jax.ShapeDtypeStruct((m, d), F32), jax.ShapeDtypeStruct((m, d), BF16))
        out_specs = (row_spec, row_spec)
    return pl.pallas_call(
        kern,
        out_shape=out_shape,
        grid_spec=pltpu.PrefetchScalarGridSpec(
            num_scalar_prefetch=1,
            grid=(n_tiles,),
            in_specs=[
                pl.BlockSpec(memory_space=pl.ANY),
                row_spec,
                mod_spec,
                pl.BlockSpec((1, d), lambda i, dest: (0, 0)),
                mod_spec,
            ],
            out_specs=out_specs,
            scratch_shapes=[pltpu.VMEM((2, tm * s, LANES), I32), pltpu.SemaphoreType.DMA((2,))],
        ),
        compiler_params=_cparams(("arbitrary",), 40),
        name="combine_residual",
    )(dest, ys, x1, mod, g.reshape(1, d), mod_next)


def _tile_tables(counts, n_tiles, tm):
    tiles_per_class = (counts[:N_CLASSES] + tm - 1) // tm
    tile_end = jnp.cumsum(tiles_per_class)
    total = tile_end[-1]
    idx = jnp.arange(n_tiles, dtype=I32)
    valid = idx < total
    blk = jnp.where(valid, idx, total - 1)
    cls = jnp.minimum(jnp.sum((tile_end[None, :] <= blk[:, None]).astype(I32), axis=1), N_CLASSES - 1)
    pair = jnp.asarray(PAIRS, I32)
    grp = cls // len(PAIRS)
    e1 = grp * EXPERTS_PER_GROUP + pair[cls % len(PAIRS), 0]
    e2 = grp * EXPERTS_PER_GROUP + pair[cls % len(PAIRS), 1]
    return e1.astype(I32), e2.astype(I32), valid.astype(I32)


def grouped_moe_residual(x1, h2_rows, cls, mod, router_w, w_gate, w_up, w_down, layer, g, mod_next,
                         rows_per_mod, final):
    m, d = x1.shape
    n_tiles = m // MOE_TM + N_CLASSES
    dest, counts = sort_tokens(cls, MOE_TM)
    dest = dest.reshape(m)
    e1, e2, valid = _tile_tables(counts, n_tiles, MOE_TM)
    src = inverse_map(dest, counts, MOE_TM, n_tiles * MOE_TM)
    ys = moe_ffn(src, e1, e2, valid, h2_rows, cast_experts_bf16(w_gate, layer), cast_experts_bf16(w_up, layer),
                 w_down, layer, router_w.T.astype(F32), d, MOE_TM)
    return combine_residual(dest, ys, x1, mod, g, mod_next, rows_per_mod, GATHER_TM, final)


def _forward(dims, x, c, ctx, c_ctx, ada_w, ada_b, mix_norm_g, ffn_norm_g, ev_w_in, ev_rpb, ev_w_out,
             od_w_in, od_b_in, od_dw_w, od_dw_b, od_ln_g, od_ln_b, od_w_out, od_b_out,
             router_w, router_b, moe_w_gate, moe_w_up, moe_w_down, final_norm_g, *, tiles):
    b, seq, d = dims.batch, dims.seq, dims.d
    n = b * seq
    depth = ada_w.shape[0]
    assert depth == 2, "layer 0 is the Fourier/attention mixer, layer 1 the Conformer mixer"
    x2d = x.reshape(n, d)
    ctx2d = ctx.reshape(b * dims.ctx, d)

    cvec = jnp.zeros((16, d), F32).at[:b].set(c).at[b].set(c_ctx)
    mod_all = adaln_mod(cvec, ada_w, ada_b, tiles["mod_tn"]).reshape(depth, 16, N_MOD, d)

    mod0 = mod_all[0, :b]
    modc0 = mod_all[0, b:b + 1]
    w_in = ev_w_in[0].astype(BF16)
    p = norm_mod_matmul(x2d, mix_norm_g[0], mod0, w_in, seq, 0, 1, tiles["in_tm"], w_in.shape[1], 0)
    kvc = norm_mod_matmul(ctx2d, mix_norm_g[0], modc0, w_in, b * dims.ctx, 0, 1,
                          min(tiles["in_tm"], b * dims.ctx), 2 * dims.naw, 1)
    fo = fourier_mix(p, dims, tiles["four_tm"])
    ao = neighbourhood_attention(p, kvc, ev_rpb[0], dims)
    mod1 = mod_all[1, :b]
    mix_w = ev_w_out.shape[1]
    x1, h2_rows, cls = outproj_residual_route(
        fo, 0, ao, 0, jnp.ones((mix_w,), F32), jnp.zeros((mix_w,), F32), ev_w_out[0].astype(BF16),
        jnp.zeros((d,), F32), x2d, mod0, ffn_norm_g[0], router_w, router_b, seq, tiles["out_tm"], False)
    x2, hmix1 = grouped_moe_residual(x1, h2_rows, cls, mod0, router_w, moe_w_gate, moe_w_up, moe_w_down, 0,
                                     mix_norm_g[1], mod1, seq, False)

    yc = glu_conv(hmix1, od_w_in[0].astype(BF16), od_b_in[0], od_dw_w[0], od_dw_b[0], dims, tiles["glu_tn"])
    x3, h4_rows, cls1 = outproj_residual_route(
        yc, 0, yc, 1, od_ln_g[0], od_ln_b[0], od_w_out[0].astype(BF16), od_b_out[0], x2, mod1, ffn_norm_g[1],
        router_w, router_b, seq, tiles["out_tm"], True)
    out = grouped_moe_residual(x3, h4_rows, cls1, mod1, router_w, moe_w_gate, moe_w_up, moe_w_down, 1,
                               final_norm_g, mod1, seq, True)
    return out.reshape(b, seq, d)


TILES = dict(mod_tn=1024, in_tm=256, glu_tn=256, four_tm=512, out_tm=256)


def kernel(x, c, ctx, c_ctx, ada_w, ada_b, mix_norm_g, ffn_norm_g, ev_w_in, ev_rpb, ev_w_out, od_w_in, od_b_in,
           od_dw_w, od_dw_b, od_ln_g, od_ln_b, od_w_out, od_b_out, router_w, router_b, moe_w_gate, moe_w_up,
           moe_w_down, final_norm_g):
    b, seq, d = x.shape
    dims = Dims(batch=b, seq=seq, d=d, ctx=ctx.shape[1], fw=ev_w_out.shape[1] - NA_HEADS * NA_HEAD_DIM,
                naw=NA_HEADS * NA_HEAD_DIM, fe=moe_w_gate.shape[3])
    return _forward(dims, x, c, ctx, c_ctx, ada_w, ada_b, mix_norm_g, ffn_norm_g, ev_w_in, ev_rpb, ev_w_out,
                    od_w_in, od_b_in, od_dw_w, od_dw_b, od_ln_g, od_ln_b, od_w_out, od_b_out,
                    router_w, router_b, moe_w_gate, moe_w_up, moe_w_down, final_norm_g, tiles=TILES)
```

```python
import functools
import math
from typing import NamedTuple

import jax
import jax.numpy as jnp
import numpy as np
from jax import lax
from jax.experimental import pallas as pl
from jax.experimental.pallas import tpu as pltpu

F32 = jnp.float32
BF16 = jnp.bfloat16
I32 = jnp.int32

LANES = 128
NEG_BIG = -1e30
EPS = 1e-6
LOG2E = math.log2(math.e)

N_MOD = 6
N_FOURIER_GROUPS = 4
NA_HEADS = 16
NA_HEAD_DIM = 64
NA_ROWS = 8
NA_COLS = 16
GRID_W = 64
CONV_K = 31
N_EXPERTS = 16
N_GROUPS = 4
EXPERTS_PER_GROUP = 4
PAIRS = ((0, 1), (0, 2), (0, 3), (1, 3), (1, 2), (2, 3))
N_CLASSES = N_GROUPS * len(PAIRS)
CLASS_PAD = 32
MOE_TM = 256
SORT_BLK = 512
GATHER_TM = 512
SCALAR_UNROLL = 8


class Dims(NamedTuple):
    batch: int
    seq: int
    d: int
    ctx: int
    fw: int
    naw: int
    fe: int


def _cparams(sem, vmem_mb):
    return pltpu.CompilerParams(dimension_semantics=sem, vmem_limit_bytes=vmem_mb << 20)


def _sigmoid(x):
    return 1.0 / (1.0 + jnp.exp(-x))


def _rms_mod(x, g, scale, shift):
    ms = jnp.mean(x * x, axis=-1, keepdims=True)
    return (x * lax.rsqrt(ms + EPS) * g) * (1.0 + scale) + shift


def _mod_kernel(c_ref, w_ref, b_ref, o_ref):
    c = c_ref[...]
    s = (c * _sigmoid(c)).astype(BF16)
    o_ref[0] = jnp.dot(s, w_ref[0].astype(BF16), preferred_element_type=F32) + b_ref[0]


def adaln_mod(cvec, ada_w, ada_b, tn):
    depth, d, n = ada_w.shape
    rows = cvec.shape[0]
    return pl.pallas_call(
        _mod_kernel,
        out_shape=jax.ShapeDtypeStruct((depth, rows, n), F32),
        grid=(depth, n // tn),
        in_specs=[
            pl.BlockSpec((rows, d), lambda l, j: (0, 0)),
            pl.BlockSpec((1, d, tn), lambda l, j: (l, 0, j)),
            pl.BlockSpec((1, 1, tn), lambda l, j: (l, 0, j)),
        ],
        out_specs=pl.BlockSpec((1, rows, tn), lambda l, j: (l, 0, j)),
        compiler_params=_cparams(("parallel", "parallel"), 40),
        name="adaln_mod",
    )(cvec, ada_w, ada_b.reshape(depth, 1, n))


def _inproj_kernel(x0_ref, xn_ref, g_ref, mod0_ref, modn_ref, w_ref, o_ref, h_scr, *, shift_idx, scale_idx):
    i = pl.program_id(0)

    def norm_into(x_ref, mod_ref, slot):
        h = _rms_mod(x_ref[...], g_ref[...], mod_ref[0, scale_idx:scale_idx + 1, :],
                     mod_ref[0, shift_idx:shift_idx + 1, :])
        h_scr[slot] = h.astype(BF16)

    @pl.when(i == 0)
    def _():
        norm_into(x0_ref, mod0_ref, 0)

    def step(cur, nxt):
        o_ref[...] = jnp.dot(h_scr[cur], w_ref[...], preferred_element_type=F32).astype(o_ref.dtype)
        norm_into(xn_ref, modn_ref, nxt)

    @pl.when(i % 2 == 0)
    def _():
        step(0, 1)

    @pl.when(i % 2 == 1)
    def _():
        step(1, 0)


def norm_mod_matmul(x2d, g, mod, w, rows_per_mod, shift_idx, scale_idx, tm, n, col_blk):
    m, d = x2d.shape
    last = m // tm - 1
    kern = functools.partial(_inproj_kernel, shift_idx=shift_idx, scale_idx=scale_idx)

    def nxt(i):
        return jnp.minimum(i + 1, last)

    return pl.pallas_call(
        kern,
        out_shape=jax.ShapeDtypeStruct((m, n), BF16),
        grid=(m // tm,),
        in_specs=[
            pl.BlockSpec((tm, d), lambda i: (0, 0)),
            pl.BlockSpec((tm, d), lambda i: (nxt(i), 0)),
            pl.BlockSpec((1, d), lambda i: (0, 0)),
            pl.BlockSpec((1, N_MOD, d), lambda i: (0, 0, 0)),
            pl.BlockSpec((1, N_MOD, d), lambda i: ((nxt(i) * tm) // rows_per_mod, 0, 0)),
            pl.BlockSpec((d, n), lambda i: (0, col_blk), pipeline_mode=pl.Buffered(1)),
        ],
        out_specs=pl.BlockSpec((tm, n), lambda i: (i, 0)),
        scratch_shapes=[pltpu.VMEM((2, tm, d), BF16)],
        compiler_params=_cparams(("arbitrary",), 48),
        name="norm_mod_matmul",
    )(x2d, x2d, g.reshape(1, d), mod, mod, w)


def _cast_kernel(x_ref, o_ref):
    o_ref[...] = x_ref[...].astype(o_ref.dtype)


def cast_experts_bf16(w, layer):
    _, e, k, n = w.shape
    return pl.pallas_call(
        _cast_kernel,
        out_shape=jax.ShapeDtypeStruct((e, k, n), BF16),
        grid=(e,),
        in_specs=[pl.BlockSpec((None, 1, k, n), lambda i: (layer, i, 0, 0))],
        out_specs=pl.BlockSpec((1, k, n), lambda i: (i, 0, 0)),
        compiler_params=_cparams(("parallel",), 32),
        name="cast_experts_bf16",
    )(w)


def _fourier_kernel(u_ref, cs_ref, cl_ref, sl_ref, o_ref, xc_scr, xs_scr, *, gc):
    @pl.when(pl.program_id(1) == 0)
    def _():
        for grp in range(N_FOURIER_GROUPS):
            r = jnp.dot(u_ref[:, grp * gc:(grp + 1) * gc], cs_ref[...], preferred_element_type=F32)
            xc_scr[:, grp * gc:(grp + 1) * gc] = r[:, :gc].astype(BF16)
            xs_scr[:, grp * gc:(grp + 1) * gc] = r[:, gc:].astype(BF16)

    acc = jnp.dot(cl_ref[...], xc_scr[...], preferred_element_type=F32)
    acc = acc - jnp.dot(sl_ref[...], xs_scr[...], preferred_element_type=F32)
    o_ref[...] = acc.astype(o_ref.dtype)


def _dft_tables(length, scale):
    kn = np.outer(np.arange(length), np.arange(length)) % length
    ang = kn.astype(np.float64) * (2.0 * math.pi / length)
    return (np.cos(ang) * scale).astype(np.float32), (np.sin(ang) * scale).astype(np.float32)


def fourier_mix(p, dims, tm):
    gc = dims.fw // N_FOURIER_GROUPS
    cc, sc = _dft_tables(gc, 1.0 / math.sqrt(dims.seq * gc))
    cs = jnp.asarray(np.concatenate([cc, sc], axis=1).astype(BF16))
    cl, sl = (jnp.asarray(t.astype(BF16)) for t in _dft_tables(dims.seq, 1.0))
    kern = functools.partial(_fourier_kernel, gc=gc)
    return pl.pallas_call(
        kern,
        out_shape=jax.ShapeDtypeStruct((dims.batch * dims.seq, dims.fw), BF16),
        grid=(dims.batch, dims.seq // tm),
        in_specs=[
            pl.BlockSpec((dims.seq, dims.fw), lambda b, m: (b, 0)),
            pl.BlockSpec((gc, 2 * gc), lambda b, m: (0, 0)),
            pl.BlockSpec((tm, dims.seq), lambda b, m: (m, 0)),
            pl.BlockSpec((tm, dims.seq), lambda b, m: (m, 0)),
        ],
        out_specs=pl.BlockSpec((tm, dims.fw), lambda b, m: (b * (dims.seq // tm) + m, 0)),
        scratch_shapes=[pltpu.VMEM((dims.seq, dims.fw), BF16), pltpu.VMEM((dims.seq, dims.fw), BF16)],
        compiler_params=_cparams(("parallel", "arbitrary"), 48),
        name="fourier_mix",
    )(p, cs, cl, sl)


NA_QROWS = 2


def _na_geometry(n_rows):
    kr = min(NA_ROWS, n_rows)
    n_union = kr + NA_QROWS - 1
    steps = []
    for rp in range(n_rows // NA_QROWS):
        rows = [rp * NA_QROWS + a for a in range(NA_QROWS)]
        starts = [min(max(r - kr // 2, 0), n_rows - kr) for r in rows]
        u = min(starts[0], n_rows - n_union)
        steps.append((u, tuple((r - u, rs - u) for r, rs in zip(rows, starts))))
    variants = sorted(set(v for _, v in steps))
    return kr, n_union, [u for u, _ in steps], [variants.index(v) for _, v in steps], variants


def _na_kernel(dr_ref, ustart_ref, q_ref, k_ref, v_ref, kc_ref, vc_ref, t_ref, o_ref, s_scr, *, n_union):
    rp = pl.program_id(1)
    row0 = pl.multiple_of(ustart_ref[rp] * GRID_W, GRID_W)
    nloc = n_union * GRID_W
    nctx = kc_ref.shape[0]
    low = lax.broadcasted_iota(I32, (GRID_W, LANES), 1) < NA_HEAD_DIM
    nt = (((1,), (1,)), ((), ()))
    tn = (((0,), (0,)), ((), ()))
    scale = NA_HEAD_DIM ** -0.5
    n_pairs = NA_HEADS // 2

    def scores(pr):
        cols = pl.ds(pr * LANES, LANES)
        q2 = (q_ref[:, cols].astype(F32) * (scale * LOG2E)).astype(BF16)
        parts = []
        for a in range(NA_QROWS):
            qa = q2[a * GRID_W:(a + 1) * GRID_W]
            zero = jnp.zeros_like(qa)
            parts += [jnp.where(low, qa, zero), jnp.where(low, zero, qa)]
        qm = jnp.concatenate(parts, axis=0)
        kp = k_ref[pl.ds(row0, nloc), cols]
        loc = lax.dot_general(kp, qm, nt, preferred_element_type=F32)
        for jj in range(n_union):
            for a in range(NA_QROWS):
                bias = t_ref[pr, dr_ref[(rp * NA_QROWS + a) * n_union + jj]].astype(F32)
                rows = slice(jj * GRID_W, (jj + 1) * GRID_W)
                lanes = slice(a * 2 * GRID_W, (a + 1) * 2 * GRID_W)
                s_scr[pr % 2, rows, lanes] = loc[rows, lanes] + bias
        s_scr[pr % 2, nloc:nloc + nctx, :] = lax.dot_general(kc_ref[:, cols], qm, nt, preferred_element_type=F32)

    def attend(pr):
        cols = pl.ds(pr * LANES, LANES)
        s = s_scr[pr % 2]
        m = jnp.max(s, axis=0, keepdims=True)
        e = jnp.exp2(s - m)
        den = jnp.sum(e, axis=0, keepdims=True)
        eb = e.astype(BF16)
        vp = v_ref[pl.ds(row0, nloc), cols]
        ot = lax.dot_general(vp, eb[0:nloc], tn, preferred_element_type=F32)
        ot = ot + lax.dot_general(vc_ref[:, cols], eb[nloc:], tn, preferred_element_type=F32)
        o = (ot / den).T
        for a in range(NA_QROWS):
            base = a * 2 * GRID_W
            o_ref[a * GRID_W:(a + 1) * GRID_W, cols] = jnp.where(
                low, o[base:base + GRID_W], o[base + GRID_W:base + 2 * GRID_W]).astype(o_ref.dtype)

    always = pl.program_id(0) >= 0
    scores(0)
    for pr in range(n_pairs):
        @pl.when(always)
        def _(pr=pr):
            if pr + 1 < n_pairs:
                scores(pr + 1)
            attend(pr)


N_DR = 2 * NA_ROWS - 1


def _colbias_kernel(rpb_ref, onehot_ref, neg_ref, o_ref):
    o_ref[...] = (jnp.dot(rpb_ref[...], onehot_ref[...], preferred_element_type=F32) + neg_ref[...]).astype(o_ref.dtype)


def _na_bias_blocks(rpb):
    col = np.arange(GRID_W)
    col_start = np.clip(col - NA_COLS // 2, 0, GRID_W - NA_COLS)
    col_mask = (col[None, :] >= col_start[:, None]) & (col[None, :] < col_start[:, None] + NA_COLS)
    dc_idx = np.clip(col[None, :] - col[:, None] + NA_COLS - 1, 0, 2 * NA_COLS - 2)
    n_dc = 2 * NA_COLS - 1
    onehot = np.zeros((LANES, GRID_W, GRID_W), np.float32)
    kk, qq = np.meshgrid(col, col, indexing="ij")
    onehot[dc_idx[qq, kk], kk, qq] = col_mask[qq, kk]
    neg = np.where(col_mask.T, 0.0, NEG_BIG).astype(np.float32).reshape(1, GRID_W * GRID_W)
    rows = NA_HEADS * N_DR
    rows_pad = -(-rows // LANES) * LANES
    rpb2d = jnp.zeros((rows_pad, LANES), F32).at[:rows, :n_dc].set(rpb.astype(F32).reshape(rows, n_dc) * LOG2E)
    colbias = pl.pallas_call(
        _colbias_kernel,
        out_shape=jax.ShapeDtypeStruct((rows_pad, GRID_W * GRID_W), BF16),
        name="na_colbias",
    )(rpb2d, jnp.asarray(onehot.reshape(LANES, GRID_W * GRID_W), BF16), jnp.asarray(neg))
    colbias = colbias[:rows].reshape(NA_HEADS, N_DR, GRID_W, GRID_W)
    masked = jnp.full((NA_HEADS, 1, GRID_W, GRID_W), NEG_BIG, BF16)
    t = jnp.concatenate([colbias, masked], axis=1).reshape(NA_HEADS // 2, 2, N_DR + 1, GRID_W, GRID_W)
    return t.transpose(0, 2, 3, 1, 4).reshape(NA_HEADS // 2, N_DR + 1, GRID_W, 2 * GRID_W)


def neighbourhood_attention(p, kvc, rpb, dims):
    n_rows = dims.seq // GRID_W
    kr, n_union, ustarts, var_of_step, variants = _na_geometry(n_rows)
    n_steps = n_rows // NA_QROWS
    dr_idx = [[(jj - r_off + NA_ROWS - 1) if rs_off <= jj < rs_off + kr else N_DR
               for jj in range(n_union)] for var in (variants[v] for v in var_of_step) for r_off, rs_off in var]
    blocks = _na_bias_blocks(rpb)
    naw = dims.naw
    qb = dims.fw // naw
    tq = NA_QROWS * GRID_W
    kern = functools.partial(_na_kernel, n_union=n_union)
    return pl.pallas_call(
        kern,
        out_shape=jax.ShapeDtypeStruct((dims.batch * dims.seq, naw), BF16),
        grid_spec=pltpu.PrefetchScalarGridSpec(
            num_scalar_prefetch=2,
            grid=(dims.batch, n_steps),
            in_specs=[
                pl.BlockSpec((tq, naw), lambda b, r, dr, us: (b * n_steps + r, qb)),
                pl.BlockSpec((dims.seq, naw), lambda b, r, dr, us: (b, qb + 1)),
                pl.BlockSpec((dims.seq, naw), lambda b, r, dr, us: (b, qb + 2)),
                pl.BlockSpec((dims.ctx, naw), lambda b, r, dr, us: (b, 0)),
                pl.BlockSpec((dims.ctx, naw), lambda b, r, dr, us: (b, 1)),
                pl.BlockSpec(blocks.shape, lambda b, r, dr, us: (0, 0, 0, 0)),
            ],
            out_specs=pl.BlockSpec((tq, naw), lambda b, r, dr, us: (b * n_steps + r, 0)),
            scratch_shapes=[pltpu.VMEM((2, n_union * GRID_W + dims.ctx, 2 * tq), F32)],
        ),
        compiler_params=_cparams(("parallel", "arbitrary"), 48),
        name="neighbourhood_attention",
    )(jnp.asarray(np.asarray(dr_idx, np.int32).reshape(-1)), jnp.asarray(ustarts, I32), p, p, p, kvc, kvc, blocks)


CONV_HALO = 16
CONV_RB = 128
CONV_MM = 256
SUBLANES = 8


def _glu_conv_kernel(h_ref, wa_ref, wg_ref, ba_ref, bg_ref, dw_ref, db_ref, o_ref, scr, *, seq):
    tn = o_ref.shape[1]
    scr[0:CONV_HALO, :] = jnp.zeros((CONV_HALO, tn), F32)
    scr[CONV_HALO + seq:2 * CONV_HALO + seq, :] = jnp.zeros((CONV_HALO, tn), F32)
    base = CONV_HALO - CONV_K // 2
    n_shift = (CONV_K + base + SUBLANES - 1) // SUBLANES

    def project(c):
        row0 = c * CONV_MM
        h = h_ref[pl.ds(row0, CONV_MM), :]
        a = jnp.dot(h, wa_ref[...], preferred_element_type=F32) + ba_ref[...]
        gt = jnp.dot(h, wg_ref[...], preferred_element_type=F32) + bg_ref[...]
        scr[pl.ds(row0 + CONV_HALO, CONV_MM), :] = a * _sigmoid(gt)

    def conv_block(rb):
        row0 = rb * CONV_RB
        for cb in range(tn // LANES):
            cols = pl.ds(cb * LANES, LANES)
            acc = db_ref[:, cols]
            for ph in range(SUBLANES):
                part = None
                for st in range(n_shift):
                    k = st * SUBLANES + ph - base
                    if 0 <= k < CONV_K:
                        rows = pl.ds(row0 + st * SUBLANES, CONV_RB + SUBLANES)
                        term = scr[rows, cols] * dw_ref[k:k + 1, cols]
                        part = term if part is None else part + term
                acc = acc + part[ph:ph + CONV_RB]
            o_ref[pl.ds(row0, CONV_RB), cols] = acc.astype(o_ref.dtype)

    n_mm = seq // CONV_MM
    n_rb = seq // CONV_RB
    always = pl.program_id(0) >= 0
    project(0)
    done = 0
    for c in range(n_mm):
        ready = ((c + 1) * CONV_MM - CONV_K // 2) // CONV_RB if c + 1 < n_mm else n_rb

        @pl.when(always)
        def _(c=c, done=done, ready=ready):
            if c + 1 < n_mm:
                project(c + 1)
            for rb in range(done, ready):
                conv_block(rb)

        done = ready


def glu_conv(h, w, b, dw_w, dw_b, dims, tn):
    d = h.shape[1]
    half = w.shape[1] // 2
    nblk = half // tn
    b2 = b.reshape(1, 2 * half)
    kern = functools.partial(_glu_conv_kernel, seq=dims.seq)
    return pl.pallas_call(
        kern,
        out_shape=jax.ShapeDtypeStruct((dims.batch * dims.seq, half), BF16),
        grid=(dims.batch, nblk),
        in_specs=[
            pl.BlockSpec((dims.seq, d), lambda s, j: (s, 0)),
            pl.BlockSpec((d, tn), lambda s, j: (0, j)),
            pl.BlockSpec((d, tn), lambda s, j: (0, j + nblk)),
            pl.BlockSpec((1, tn), lambda s, j: (0, j)),
            pl.BlockSpec((1, tn), lambda s, j: (0, j + nblk)),
            pl.BlockSpec((CONV_K, tn), lambda s, j: (0, j)),
            pl.BlockSpec((1, tn), lambda s, j: (0, j)),
        ],
        out_specs=pl.BlockSpec((dims.seq, tn), lambda s, j: (s, j)),
        scratch_shapes=[pltpu.VMEM((dims.seq + 2 * CONV_HALO, tn), F32)],
        compiler_params=_cparams(("parallel", "arbitrary"), 48),
        name="glu_conv",
    )(h, w, w, b2, b2, dw_w, dw_b.reshape(1, half))


def _route(logits_t, rb):
    sel = [_sigmoid(logits_t[e:e + 1, :]) + rb[e:e + 1, :] for e in range(N_EXPERTS)]
    gs = []
    for g in range(N_GROUPS):
        v0, v1, v2, v3 = sel[4 * g:4 * g + 4]
        hi1, lo1 = jnp.maximum(v0, v1), jnp.minimum(v0, v1)
        hi2, lo2 = jnp.maximum(v2, v3), jnp.minimum(v2, v3)
        gs.append(jnp.maximum(hi1, hi2) + jnp.maximum(jnp.minimum(hi1, hi2), jnp.maximum(lo1, lo2)))
    best = gs[0]
    bg = jnp.zeros(best.shape, I32)
    for g in range(1, N_GROUPS):
        upd = gs[g] > best
        bg = jnp.where(upd, g, bg)
        best = jnp.where(upd, gs[g], best)
    v = []
    for i in range(EXPERTS_PER_GROUP):
        vi = sel[i]
        for g in range(1, N_GROUPS):
            vi = jnp.where(bg == g, sel[4 * g + i], vi)
        v.append(vi)
    picked = []
    for i in range(EXPERTS_PER_GROUP):
        rank = jnp.zeros(best.shape, I32)
        for j in range(EXPERTS_PER_GROUP):
            if j == i:
                continue
            ahead = (v[j] > v[i]) | ((v[j] == v[i]) & (j < i))
            rank = rank + ahead.astype(I32)
        picked.append(rank < 2)
    code = jnp.full(best.shape, len(PAIRS) - 1, I32)
    for idx in range(len(PAIRS) - 2, -1, -1):
        a, b = PAIRS[idx]
        code = jnp.where(picked[a] & picked[b], idx, code)
    return bg * len(PAIRS) + code


def _outproj_kernel(a_ref, b_ref, lng_ref, lnb_ref, w_ref, bias_ref, x_ref, mod_ref, g_ref, rw1_ref, rw2_ref,
                    rb_ref, x1_ref, h2_ref, cls_ref, *, ln_swish):
    ka = a_ref.shape[1]
    tm = a_ref.shape[0]
    if ln_swish:
        t = jnp.concatenate([a_ref[...], b_ref[...]], axis=-1).astype(F32)
        mu = jnp.mean(t, axis=-1, keepdims=True)
        tc = t - mu
        var = jnp.mean(tc * tc, axis=-1, keepdims=True)
        z = tc * lax.rsqrt(var + EPS) * lng_ref[...] + lnb_ref[...]
        z = (z * _sigmoid(z)).astype(BF16)
        y = jnp.dot(z, w_ref[...], preferred_element_type=F32) + bias_ref[...]
    else:
        y = jnp.dot(a_ref[...], w_ref[0:ka, :], preferred_element_type=F32)
        y = y + jnp.dot(b_ref[...], w_ref[ka:, :], preferred_element_type=F32) + bias_ref[...]
    x1 = x_ref[...] + mod_ref[0, 2:3, :] * y
    x1_ref[...] = x1
    h2 = _rms_mod(x1, g_ref[...], mod_ref[0, 4:5, :], mod_ref[0, 3:4, :])
    _store_packed_rows(h2_ref, h2, tm)
    hi = h2.astype(BF16)
    lo = (h2 - hi.astype(F32)).astype(BF16)
    s = jnp.dot(hi, rw1_ref[...], preferred_element_type=F32) + jnp.dot(lo, rw2_ref[...], preferred_element_type=F32)
    st = s.T
    logits_t = st[0:N_EXPERTS, :] + st[N_EXPERTS:2 * N_EXPERTS, :]
    cls_ref[...] = _route(logits_t, rb_ref[...])


def outproj_residual_route(a, a_col, b, b_col, ln_g, ln_b, w, bias, x2d, mod, ffn_g, router_w, router_b,
                           rows_per_mod, tm, ln_swish):
    m, d = x2d.shape
    ka = kb = w.shape[0] // 2
    kin = ka + kb
    n_slab = d // (2 * LANES)
    rw_hi = router_w.astype(BF16)
    rw_lo = (router_w - rw_hi.astype(F32)).astype(BF16)
    pad = jnp.zeros((d, LANES - 2 * N_EXPERTS), BF16)
    rw1 = jnp.concatenate([rw_hi, rw_lo, pad], axis=1)
    rw2 = jnp.concatenate([rw_hi, jnp.zeros((d, N_EXPERTS), BF16), pad], axis=1)
    kern = functools.partial(_outproj_kernel, ln_swish=ln_swish)
    return pl.pallas_call(
        kern,
        out_shape=(
            jax.ShapeDtypeStruct((m, d), F32),
            jax.ShapeDtypeStruct((m * n_slab, LANES), I32),
            jax.ShapeDtypeStruct((1, m), I32),
        ),
        grid=(m // tm,),
        in_specs=[
            pl.BlockSpec((tm, ka), lambda i: (i, a_col)),
            pl.BlockSpec((tm, kb), lambda i: (i, b_col)),
            pl.BlockSpec((1, kin), lambda i: (0, 0)),
            pl.BlockSpec((1, kin), lambda i: (0, 0)),
            pl.BlockSpec((kin, d), lambda i: (0, 0), pipeline_mode=pl.Buffered(1)),
            pl.BlockSpec((1, d), lambda i: (0, 0)),
            pl.BlockSpec((tm, d), lambda i: (i, 0)),
            pl.BlockSpec((1, N_MOD, d), lambda i: ((i * tm) // rows_per_mod, 0, 0)),
            pl.BlockSpec((1, d), lambda i: (0, 0)),
            pl.BlockSpec((d, LANES), lambda i: (0, 0)),
            pl.BlockSpec((d, LANES), lambda i: (0, 0)),
            pl.BlockSpec((N_EXPERTS, 1), lambda i: (0, 0)),
        ],
        out_specs=(
            pl.BlockSpec((tm, d), lambda i: (i, 0)),
            pl.BlockSpec((tm * n_slab, LANES), lambda i: (i, 0)),
            pl.BlockSpec((1, tm), lambda i: (0, i)),
        ),
        compiler_params=_cparams(("parallel",), 56),
        name="outproj_residual_route",
    )(a, b, ln_g.reshape(1, kin), ln_b.reshape(1, kin), w, bias.reshape(1, d), x2d, mod, ffn_g.reshape(1, d),
      rw1, rw2, router_b.astype(F32).reshape(N_EXPERTS, 1))


def _class_onehot(cls_ref):
    blk = cls_ref.shape[1]
    return (lax.broadcasted_iota(I32, (CLASS_PAD, blk), 0) == cls_ref[...]).astype(F32)


def _count_kernel(cls_ref, cnt_ref):
    @pl.when(pl.program_id(0) == 0)
    def _():
        cnt_ref[...] = jnp.zeros_like(cnt_ref)

    cnt_ref[...] += jnp.sum(_class_onehot(cls_ref), axis=1, keepdims=True)


def _dest_kernel(cls_ref, start_ref, dest_ref, carry_scr):
    @pl.when(pl.program_id(0) == 0)
    def _():
        carry_scr[...] = jnp.zeros_like(carry_scr)

    blk = cls_ref.shape[1]
    onehot = _class_onehot(cls_ref)
    tri = (lax.broadcasted_iota(I32, (blk, blk), 0) <= lax.broadcasted_iota(I32, (blk, blk), 1)).astype(BF16)
    cum = jnp.dot(onehot.astype(BF16), tri, preferred_element_type=F32)
    pos = cum - 1.0 + carry_scr[...] + start_ref[...]
    dest_ref[...] = jnp.sum(onehot * pos, axis=0, keepdims=True).astype(I32)
    carry_scr[...] += jnp.sum(onehot, axis=1, keepdims=True)


def sort_tokens(cls, tile):
    n = cls.shape[1]
    nb = n // SORT_BLK
    counts = pl.pallas_call(
        _count_kernel,
        out_shape=jax.ShapeDtypeStruct((CLASS_PAD, 1), F32),
        grid=(nb,),
        in_specs=[pl.BlockSpec((1, SORT_BLK), lambda j: (0, j))],
        out_specs=pl.BlockSpec((CLASS_PAD, 1), lambda j: (0, 0)),
        compiler_params=_cparams(("arbitrary",), 32),
        name="class_counts",
    )(cls)
    counts = counts.reshape(CLASS_PAD).astype(I32)
    padded = ((counts + tile - 1) // tile) * tile
    starts = (jnp.cumsum(padded) - padded).astype(F32).reshape(CLASS_PAD, 1)
    dest = pl.pallas_call(
        _dest_kernel,
        out_shape=jax.ShapeDtypeStruct((1, n), I32),
        grid=(nb,),
        in_specs=[pl.BlockSpec((1, SORT_BLK), lambda j: (0, j)),
                  pl.BlockSpec((CLASS_PAD, 1), lambda j: (0, 0))],
        out_specs=pl.BlockSpec((1, SORT_BLK), lambda j: (0, j)),
        scratch_shapes=[pltpu.VMEM((CLASS_PAD, 1), F32)],
        compiler_params=_cparams(("arbitrary",), 32),
        name="sorted_positions",
    )(cls, starts)
    return dest, counts


def _inverse_kernel(dest_ref, src_ref, *, n_tok, n_rows):
    def clear(k, carry):
        for u in range(SCALAR_UNROLL):
            src_ref[k * SCALAR_UNROLL + u] = 0
        return carry

    lax.fori_loop(0, n_rows // SCALAR_UNROLL, clear, 0)

    def scatter(k, carry):
        for u in range(SCALAR_UNROLL):
            t = k * SCALAR_UNROLL + u
            src_ref[dest_ref[t]] = t
        return carry

    lax.fori_loop(0, n_tok // SCALAR_UNROLL, scatter, 0)


def inverse_map(dest, n_rows):
    n_tok = dest.shape[0]
    kern = functools.partial(_inverse_kernel, n_tok=n_tok, n_rows=n_rows)
    return pl.pallas_call(
        kern,
        out_shape=jax.ShapeDtypeStruct((n_rows,), I32),
        in_specs=[pl.BlockSpec(memory_space=pltpu.SMEM)],
        out_specs=pl.BlockSpec(memory_space=pltpu.SMEM),
        name="inverse_map",
    )(dest)


def _start_row_gather(idx_ref, base, src_hbm, buf, sem, slot, tm, s):
    def body(k, carry):
        for u in range(SCALAR_UNROLL):
            r = k * SCALAR_UNROLL + u
            pltpu.make_async_copy(
                src_hbm.at[pl.ds(pl.multiple_of(idx_ref[base + r] * s, s), s), :],
                buf.at[slot, pl.ds(pl.multiple_of(r * s, s), s), :],
                sem.at[slot]).start(priority=u % 2)
        return carry

    lax.fori_loop(0, tm // SCALAR_UNROLL, body, 0)


def _wait_row_gather(src_hbm, buf, sem, slot, tm, s):
    pltpu.make_async_copy(src_hbm.at[pl.ds(0, tm * s), :], buf.at[slot], sem.at[slot]).wait()


HI_HALF = -65536


def _store_packed_rows(ref, v, tm):
    half = v.shape[1] // 2
    n_slab = half // LANES
    for c in range(n_slab):
        a = v[:, c * LANES:(c + 1) * LANES].astype(BF16).astype(F32)
        b = v[:, half + c * LANES:half + (c + 1) * LANES].astype(BF16).astype(F32)
        word = lax.bitcast_convert_type(a, I32) | lax.shift_right_logical(lax.bitcast_convert_type(b, I32), 16)
        ref[pl.ds(c, tm, stride=n_slab), :] = word


def _load_packed_rows(ref, tm, n_slab):
    words = [ref[pl.ds(c, tm, stride=n_slab), :] for c in range(n_slab)]
    hi = [lax.bitcast_convert_type(w & HI_HALF, F32) for w in words]
    lo = [lax.bitcast_convert_type(lax.shift_left(w, 16), F32) for w in words]
    return jnp.concatenate(hi + lo, axis=-1)


def _moe_kernel(src_ref, e1_ref, e2_ref, valid_ref, h_hbm, wg1_ref, wg2_ref, wu1_ref, wu2_ref,
                wd1_ref, wd2_ref, rwt_ref, ys_ref, buf, sem, *, tm, n_slab, n_tiles):
    i = pl.program_id(0)
    slot = i % 2

    @pl.when(i == 0)
    def _():
        _start_row_gather(src_ref, 0, h_hbm, buf, sem, 0, tm, n_slab)

    nxt = jnp.minimum(i + 1, n_tiles - 1)

    @pl.when((i + 1 < n_tiles) & (valid_ref[nxt] == 1))
    def _():
        _start_row_gather(src_ref, nxt * tm, h_hbm, buf, sem, 1 - slot, tm, n_slab)

    @pl.when(valid_ref[i] == 0)
    def _():
        ys_ref[...] = jnp.zeros_like(ys_ref)

    @pl.when(valid_ref[i] == 1)
    def _():
        _wait_row_gather(h_hbm, buf, sem, slot, tm, n_slab)
        xr = _load_packed_rows(buf.at[slot], tm, n_slab)
        x = xr.astype(BF16)
        l1 = jnp.sum(xr * rwt_ref[pl.ds(e1_ref[i], 1), :], axis=-1, keepdims=True)
        l2 = jnp.sum(xr * rwt_ref[pl.ds(e2_ref[i], 1), :], axis=-1, keepdims=True)
        s1 = _sigmoid(l1)
        s2 = _sigmoid(l2)
        tot = s1 + s2

        def expert(wg_ref, wu_ref, weight):
            gt = jnp.dot(x, wg_ref[0], preferred_element_type=F32)
            up = jnp.dot(x, wu_ref[0], preferred_element_type=F32)
            return gt * _sigmoid(gt) * up * weight

        a1 = expert(wg1_ref, wu1_ref, s1 / tot)
        a2 = expert(wg2_ref, wu2_ref, s2 / tot)
        y = jnp.dot(a1, wd1_ref[0], preferred_element_type=F32)
        y = y + jnp.dot(a2, wd2_ref[0], preferred_element_type=F32)
        _store_packed_rows(ys_ref, y, tm)


def moe_ffn(src, tile_e1, tile_e2, tile_valid, h_rows, w_gate, w_up, w_down, layer, router_w_t, d, tm):
    n_tiles = tile_e1.shape[0]
    n_slab = d // (2 * LANES)
    fe = w_gate.shape[2]
    kern = functools.partial(_moe_kernel, tm=tm, n_slab=n_slab, n_tiles=n_tiles)

    def w_in_spec(which):
        return pl.BlockSpec((1, d, fe), lambda i, src, e1, e2, valid: ((e1, e2)[which][i], 0, 0))

    def w_out_spec(which):
        return pl.BlockSpec((None, 1, fe, d), lambda i, src, e1, e2, valid: (layer, (e1, e2)[which][i], 0, 0))

    return pl.pallas_call(
        kern,
        out_shape=jax.ShapeDtypeStruct((n_tiles * tm * n_slab, LANES), I32),
        grid_spec=pltpu.PrefetchScalarGridSpec(
            num_scalar_prefetch=4,
            grid=(n_tiles,),
            in_specs=[
                pl.BlockSpec(memory_space=pl.ANY),
                w_in_spec(0), w_in_spec(1), w_in_spec(0), w_in_spec(1), w_out_spec(0), w_out_spec(1),
                pl.BlockSpec((N_EXPERTS, d), lambda i, src, e1, e2, valid: (0, 0)),
            ],
            out_specs=pl.BlockSpec((tm * n_slab, LANES), lambda i, src, e1, e2, valid: (i, 0)),
            scratch_shapes=[pltpu.VMEM((2, tm * n_slab, LANES), I32), pltpu.SemaphoreType.DMA((2,))],
        ),
        compiler_params=_cparams(("arbitrary",), 56),
        name="moe_ffn",
    )(src, tile_e1, tile_e2, tile_valid, h_rows, w_gate, w_gate, w_up, w_up, w_down, w_down, router_w_t)


def _combine_kernel(dest_ref, ys_hbm, x1_ref, mod_ref, g_ref, modn_ref, *rest, tm, s, n_tiles, final):
    if final:
        o_ref, buf, sem = rest
    else:
        o_ref, hn_ref, buf, sem = rest
    i = pl.program_id(0)
    slot = i % 2

    @pl.when(i == 0)
    def _():
        _start_row_gather(dest_ref, 0, ys_hbm, buf, sem, 0, tm, s)

    @pl.when(i + 1 < n_tiles)
    def _():
        _start_row_gather(dest_ref, (i + 1) * tm, ys_hbm, buf, sem, 1 - slot, tm, s)

    _wait_row_gather(ys_hbm, buf, sem, slot, tm, s)
    y = _load_packed_rows(buf.at[slot], tm, s)
    x2 = x1_ref[...] + mod_ref[0, 5:6, :] * y
    if final:
        ms = jnp.mean(x2 * x2, axis=-1, keepdims=True)
        o_ref[...] = x2 * lax.rsqrt(ms + EPS) * g_ref[...]
    else:
        o_ref[...] = x2
        hn_ref[...] = _rms_mod(x2, g_ref[...], modn_ref[0, 1:2, :], modn_ref[0, 0:1, :]).astype(hn_ref.dtype)


def combine_residual(dest, ys, x1, mod, g, mod_next, rows_per_mod, tm, final):
    m, d = x1.shape
    s = d // (2 * LANES)
    n_tiles = m // tm
    kern = functools.partial(_combine_kernel, tm=tm, s=s, n_tiles=n_tiles, final=final)
    row_spec = pl.BlockSpec((tm, d), lambda i, dest: (i, 0))
    mod_spec = pl.BlockSpec((1, N_MOD, d), lambda i, dest: ((i * tm) // rows_per_mod, 0, 0))
    if final:
        out_shape, out_specs = jax.ShapeDtypeStruct((m, d), F32), row_spec
    else:
        out_shape = (jax.ShapeDtypeStruct((m, d), F32), jax.ShapeDtypeStruct((m, d), BF16))
        out_specs = (row_spec, row_spec)
    return pl.pallas_call(
        kern,
        out_shape=out_shape,
        grid_spec=pltpu.PrefetchScalarGridSpec(
            num_scalar_prefetch=1,
            grid=(n_tiles,),
            in_specs=[
                pl.BlockSpec(memory_space=pl.ANY),
                row_spec,
                mod_spec,
                pl.BlockSpec((1, d), lambda i, dest: (0, 0)),
                mod_spec,
            ],
            out_specs=out_specs,
            scratch_shapes=[pltpu.VMEM((2, tm * s, LANES), I32), pltpu.SemaphoreType.DMA((2,))],
        ),
        compiler_params=_cparams(("arbitrary",), 40),
        name="combine_residual",
    )(dest, ys, x1, mod, g.reshape(1, d), mod_next)


def _tile_tables(counts, n_tiles, tm):
    tiles_per_class = (counts[:N_CLASSES] + tm - 1) // tm
    tile_end = jnp.cumsum(tiles_per_class)
    total = tile_end[-1]
    idx = jnp.arange(n_tiles, dtype=I32)
    valid = idx < total
    blk = jnp.where(valid, idx, total - 1)
    cls = jnp.minimum(jnp.sum((tile_end[None, :] <= blk[:, None]).astype(I32), axis=1), N_CLASSES - 1)
    pair = jnp.asarray(PAIRS, I32)
    grp = cls // len(PAIRS)
    e1 = grp * EXPERTS_PER_GROUP + pair[cls % len(PAIRS), 0]
    e2 = grp * EXPERTS_PER_GROUP + pair[cls % len(PAIRS), 1]
    return e1.astype(I32), e2.astype(I32), valid.astype(I32)


def grouped_moe_residual(x1, h2_rows, cls, mod, router_w, w_gate, w_up, w_down, layer, g, mod_next,
                         rows_per_mod, final):
    m, d = x1.shape
    n_tiles = m // MOE_TM + N_CLASSES
    dest, counts = sort_tokens(cls, MOE_TM)
    dest = dest.reshape(m)
    e1, e2, valid = _tile_tables(counts, n_tiles, MOE_TM)
    src = inverse_map(dest, n_tiles * MOE_TM)
    ys = moe_ffn(src, e1, e2, valid, h2_rows, cast_experts_bf16(w_gate, layer), cast_experts_bf16(w_up, layer),
                 w_down, layer, router_w.T.astype(F32), d, MOE_TM)
    return combine_residual(dest, ys, x1, mod, g, mod_next, rows_per_mod, GATHER_TM, final)


def _forward(dims, x, c, ctx, c_ctx, ada_w, ada_b, mix_norm_g, ffn_norm_g, ev_w_in, ev_rpb, ev_w_out,
             od_w_in, od_b_in, od_dw_w, od_dw_b, od_ln_g, od_ln_b, od_w_out, od_b_out,
             router_w, router_b, moe_w_gate, moe_w_up, moe_w_down, final_norm_g, *, tiles):
    b, seq, d = dims.batch, dims.seq, dims.d
    n = b * seq
    depth = ada_w.shape[0]
    assert depth == 2, "layer 0 is the Fourier/attention mixer, layer 1 the Conformer mixer"
    x2d = x.reshape(n, d)
    ctx2d = ctx.reshape(b * dims.ctx, d)

    cvec = jnp.zeros((16, d), F32).at[:b].set(c).at[b].set(c_ctx)
    mod_all = adaln_mod(cvec, ada_w, ada_b, tiles["mod_tn"]).reshape(depth, 16, N_MOD, d)

    mod0 = mod_all[0, :b]
    modc0 = mod_all[0, b:b + 1]
    w_in = ev_w_in[0].astype(BF16)
    p = norm_mod_matmul(x2d, mix_norm_g[0], mod0, w_in, seq, 0, 1, tiles["in_tm"], w_in.shape[1], 0)
    kvc = norm_mod_matmul(ctx2d, mix_norm_g[0], modc0, w_in, b * dims.ctx, 0, 1,
                          min(tiles["in_tm"], b * dims.ctx), 2 * dims.naw, 1)
    fo = fourier_mix(p, dims, tiles["four_tm"])
    ao = neighbourhood_attention(p, kvc, ev_rpb[0], dims)
    mod1 = mod_all[1, :b]
    mix_w = ev_w_out.shape[1]
    x1, h2_rows, cls = outproj_residual_route(
        fo, 0, ao, 0, jnp.ones((mix_w,), F32), jnp.zeros((mix_w,), F32), ev_w_out[0].astype(BF16),
        jnp.zeros((d,), F32), x2d, mod0, ffn_norm_g[0], router_w, router_b, seq, tiles["out_tm"], False)
    x2, hmix1 = grouped_moe_residual(x1, h2_rows, cls, mod0, router_w, moe_w_gate, moe_w_up, moe_w_down, 0,
                                     mix_norm_g[1], mod1, seq, False)

    yc = glu_conv(hmix1, od_w_in[0].astype(BF16), od_b_in[0], od_dw_w[0], od_dw_b[0], dims, tiles["glu_tn"])
    x3, h4_rows, cls1 = outproj_residual_route(
        yc, 0, yc, 1, od_ln_g[0], od_ln_b[0], od_w_out[0].astype(BF16), od_b_out[0], x2, mod1, ffn_norm_g[1],
        router_w, router_b, seq, tiles["out_tm"], True)
    out = grouped_moe_residual(x3, h4_rows, cls1, mod1, router_w, moe_w_gate, moe_w_up, moe_w_down, 1,
                               final_norm_g, mod1, seq, True)
    return out.reshape(b, seq, d)


TILES = dict(mod_tn=1024, in_tm=256, glu_tn=256, four_tm=512, out_tm=256)


def kernel(x, c, ctx, c_ctx, ada_w, ada_b, mix_norm_g, ffn_norm_g, ev_w_in, ev_rpb, ev_w_out, od_w_in, od_b_in,
           od_dw_w, od_dw_b, od_ln_g, od_ln_b, od_w_out, od_b_out, router_w, router_b, moe_w_gate, moe_w_up,
           moe_w_down, final_norm_g):
    b, seq, d = x.shape
    dims = Dims(batch=b, seq=seq, d=d, ctx=ctx.shape[1], fw=ev_w_out.shape[1] - NA_HEADS * NA_HEAD_DIM,
                naw=NA_HEADS * NA_HEAD_DIM, fe=moe_w_gate.shape[3])
    return _forward(dims, x, c, ctx, c_ctx, ada_w, ada_b, mix_norm_g, ffn_norm_g, ev_w_in, ev_rpb, ev_w_out,
                    od_w_in, od_b_in, od_dw_w, od_dw_b, od_ln_g, od_ln_b, od_w_out, od_b_out,
                    router_w, router_b, moe_w_gate, moe_w_up, moe_w_down, final_norm_g, tiles=TILES)
```

```python
import functools
import math
from typing import NamedTuple

import jax
import jax.numpy as jnp
import numpy as np
from jax import lax
from jax.experimental import pallas as pl
from jax.experimental.pallas import tpu as pltpu

F32 = jnp.float32
BF16 = jnp.bfloat16
I32 = jnp.int32

LANES = 128
NEG_BIG = -1e30
EPS = 1e-6
LOG2E = math.log2(math.e)

N_MOD = 6
N_FOURIER_GROUPS = 4
NA_HEADS = 16
NA_HEAD_DIM = 64
NA_ROWS = 8
NA_COLS = 16
GRID_W = 64
CONV_K = 31
N_EXPERTS = 16
N_GROUPS = 4
EXPERTS_PER_GROUP = 4
PAIRS = ((0, 1), (0, 2), (0, 3), (1, 3), (1, 2), (2, 3))
N_CLASSES = N_GROUPS * len(PAIRS)
CLASS_PAD = 32
MOE_TM = 256
SORT_BLK = 512
GATHER_TM = 512
SCALAR_UNROLL = 8


class Dims(NamedTuple):
    batch: int
    seq: int
    d: int
    ctx: int
    fw: int
    naw: int
    fe: int


def _cparams(sem, vmem_mb):
    return pltpu.CompilerParams(dimension_semantics=sem, vmem_limit_bytes=vmem_mb << 20)


def _sigmoid(x):
    return 1.0 / (1.0 + jnp.exp(-x))


def _rms_mod(x, g, scale, shift):
    ms = jnp.mean(x * x, axis=-1, keepdims=True)
    return (x * lax.rsqrt(ms + EPS) * g) * (1.0 + scale) + shift


def _mod_kernel(c_ref, w_ref, b_ref, o_ref):
    c = c_ref[...]
    s = (c * _sigmoid(c)).astype(BF16)
    o_ref[0] = jnp.dot(s, w_ref[0].astype(BF16), preferred_element_type=F32) + b_ref[0]


def adaln_mod(cvec, ada_w, ada_b, tn):
    depth, d, n = ada_w.shape
    rows = cvec.shape[0]
    return pl.pallas_call(
        _mod_kernel,
        out_shape=jax.ShapeDtypeStruct((depth, rows, n), F32),
        grid=(depth, n // tn),
        in_specs=[
            pl.BlockSpec((rows, d), lambda l, j: (0, 0)),
            pl.BlockSpec((1, d, tn), lambda l, j: (l, 0, j)),
            pl.BlockSpec((1, 1, tn), lambda l, j: (l, 0, j)),
        ],
        out_specs=pl.BlockSpec((1, rows, tn), lambda l, j: (l, 0, j)),
        compiler_params=_cparams(("parallel", "parallel"), 40),
        name="adaln_mod",
    )(cvec, ada_w, ada_b.reshape(depth, 1, n))


def _inproj_kernel(x0_ref, xn_ref, g_ref, mod0_ref, modn_ref, w_ref, o_ref, h_scr, *, shift_idx, scale_idx):
    i = pl.program_id(0)

    def norm_into(x_ref, mod_ref, slot):
        h = _rms_mod(x_ref[...], g_ref[...], mod_ref[0, scale_idx:scale_idx + 1, :],
                     mod_ref[0, shift_idx:shift_idx + 1, :])
        h_scr[slot] = h.astype(BF16)

    @pl.when(i == 0)
    def _():
        norm_into(x0_ref, mod0_ref, 0)

    def step(cur, nxt):
        o_ref[...] = jnp.dot(h_scr[cur], w_ref[...], preferred_element_type=F32).astype(o_ref.dtype)
        norm_into(xn_ref, modn_ref, nxt)

    @pl.when(i % 2 == 0)
    def _():
        step(0, 1)

    @pl.when(i % 2 == 1)
    def _():
        step(1, 0)


def norm_mod_matmul(x2d, g, mod, w, rows_per_mod, shift_idx, scale_idx, tm, n, col_blk):
    m, d = x2d.shape
    last = m // tm - 1
    kern = functools.partial(_inproj_kernel, shift_idx=shift_idx, scale_idx=scale_idx)

    def nxt(i):
        return jnp.minimum(i + 1, last)

    return pl.pallas_call(
        kern,
        out_shape=jax.ShapeDtypeStruct((m, n), BF16),
        grid=(m // tm,),
        in_specs=[
            pl.BlockSpec((tm, d), lambda i: (0, 0)),
            pl.BlockSpec((tm, d), lambda i: (nxt(i), 0)),
            pl.BlockSpec((1, d), lambda i: (0, 0)),
            pl.BlockSpec((1, N_MOD, d), lambda i: (0, 0, 0)),
            pl.BlockSpec((1, N_MOD, d), lambda i: ((nxt(i) * tm) // rows_per_mod, 0, 0)),
            pl.BlockSpec((d, n), lambda i: (0, col_blk), pipeline_mode=pl.Buffered(1)),
        ],
        out_specs=pl.BlockSpec((tm, n), lambda i: (i, 0)),
        scratch_shapes=[pltpu.VMEM((2, tm, d), BF16)],
        compiler_params=_cparams(("arbitrary",), 48),
        name="norm_mod_matmul",
    )(x2d, x2d, g.reshape(1, d), mod, mod, w)


def _fourier_kernel(u_ref, cs_ref, cl_ref, sl_ref, o_ref, xc_scr, xs_scr, *, gc):
    @pl.when(pl.program_id(1) == 0)
    def _():
        for grp in range(N_FOURIER_GROUPS):
            r = jnp.dot(u_ref[:, grp * gc:(grp + 1) * gc], cs_ref[...], preferred_element_type=F32)
            xc_scr[:, grp * gc:(grp + 1) * gc] = r[:, :gc].astype(BF16)
            xs_scr[:, grp * gc:(grp + 1) * gc] = r[:, gc:].astype(BF16)

    acc = jnp.dot(cl_ref[...], xc_scr[...], preferred_element_type=F32)
    acc = acc - jnp.dot(sl_ref[...], xs_scr[...], preferred_element_type=F32)
    o_ref[...] = acc.astype(o_ref.dtype)


def _dft_tables(length, scale):
    kn = np.outer(np.arange(length), np.arange(length)) % length
    ang = kn.astype(np.float64) * (2.0 * math.pi / length)
    return (np.cos(ang) * scale).astype(np.float32), (np.sin(ang) * scale).astype(np.float32)


def fourier_mix(p, dims, tm):
    gc = dims.fw // N_FOURIER_GROUPS
    cc, sc = _dft_tables(gc, 1.0 / math.sqrt(dims.seq * gc))
    cs = jnp.asarray(np.concatenate([cc, sc], axis=1).astype(BF16))
    cl, sl = (jnp.asarray(t.astype(BF16)) for t in _dft_tables(dims.seq, 1.0))
    kern = functools.partial(_fourier_kernel, gc=gc)
    return pl.pallas_call(
        kern,
        out_shape=jax.ShapeDtypeStruct((dims.batch * dims.seq, dims.fw), BF16),
        grid=(dims.batch, dims.seq // tm),
        in_specs=[
            pl.BlockSpec((dims.seq, dims.fw), lambda b, m: (b, 0)),
            pl.BlockSpec((gc, 2 * gc), lambda b, m: (0, 0)),
            pl.BlockSpec((tm, dims.seq), lambda b, m: (m, 0)),
            pl.BlockSpec((tm, dims.seq), lambda b, m: (m, 0)),
        ],
        out_specs=pl.BlockSpec((tm, dims.fw), lambda b, m: (b * (dims.seq // tm) + m, 0)),
        scratch_shapes=[pltpu.VMEM((dims.seq, dims.fw), BF16), pltpu.VMEM((dims.seq, dims.fw), BF16)],
        compiler_params=_cparams(("parallel", "arbitrary"), 48),
        name="fourier_mix",
    )(p, cs, cl, sl)


NA_QROWS = 2


def _na_geometry(n_rows):
    kr = min(NA_ROWS, n_rows)
    n_union = kr + NA_QROWS - 1
    steps = []
    for rp in range(n_rows // NA_QROWS):
        rows = [rp * NA_QROWS + a for a in range(NA_QROWS)]
        starts = [min(max(r - kr // 2, 0), n_rows - kr) for r in rows]
        u = min(starts[0], n_rows - n_union)
        steps.append((u, tuple((r - u, rs - u) for r, rs in zip(rows, starts))))
    variants = sorted(set(v for _, v in steps))
    return kr, n_union, [u for u, _ in steps], [variants.index(v) for _, v in steps], variants


def _na_kernel(dr_ref, ustart_ref, q_ref, k_ref, v_ref, kc_ref, vc_ref, t_ref, o_ref, s_scr, *, n_union):
    rp = pl.program_id(1)
    row0 = pl.multiple_of(ustart_ref[rp] * GRID_W, GRID_W)
    nloc = n_union * GRID_W
    nctx = kc_ref.shape[0]
    low = lax.broadcasted_iota(I32, (GRID_W, LANES), 1) < NA_HEAD_DIM
    nt = (((1,), (1,)), ((), ()))
    tn = (((0,), (0,)), ((), ()))
    scale = NA_HEAD_DIM ** -0.5
    n_pairs = NA_HEADS // 2

    def scores(pr):
        cols = pl.ds(pr * LANES, LANES)
        q2 = (q_ref[:, cols].astype(F32) * (scale * LOG2E)).astype(BF16)
        parts = []
        for a in range(NA_QROWS):
            qa = q2[a * GRID_W:(a + 1) * GRID_W]
            zero = jnp.zeros_like(qa)
            parts += [jnp.where(low, qa, zero), jnp.where(low, zero, qa)]
        qm = jnp.concatenate(parts, axis=0)
        kp = k_ref[pl.ds(row0, nloc), cols]
        loc = lax.dot_general(kp, qm, nt, preferred_element_type=F32)
        for jj in range(n_union):
            for a in range(NA_QROWS):
                bias = t_ref[pr, dr_ref[(rp * NA_QROWS + a) * n_union + jj]].astype(F32)
                rows = slice(jj * GRID_W, (jj + 1) * GRID_W)
                lanes = slice(a * 2 * GRID_W, (a + 1) * 2 * GRID_W)
                s_scr[pr % 2, rows, lanes] = loc[rows, lanes] + bias
        s_scr[pr % 2, nloc:nloc + nctx, :] = lax.dot_general(kc_ref[:, cols], qm, nt, preferred_element_type=F32)

    def attend(pr):
        cols = pl.ds(pr * LANES, LANES)
        s = s_scr[pr % 2]
        m = jnp.max(s, axis=0, keepdims=True)
        e = jnp.exp2(s - m)
        den = jnp.sum(e, axis=0, keepdims=True)
        eb = e.astype(BF16)
        vp = v_ref[pl.ds(row0, nloc), cols]
        ot = lax.dot_general(vp, eb[0:nloc], tn, preferred_element_type=F32)
        ot = ot + lax.dot_general(vc_ref[:, cols], eb[nloc:], tn, preferred_element_type=F32)
        o = (ot / den).T
        for a in range(NA_QROWS):
            base = a * 2 * GRID_W
            o_ref[a * GRID_W:(a + 1) * GRID_W, cols] = jnp.where(
                low, o[base:base + GRID_W], o[base + GRID_W:base + 2 * GRID_W]).astype(o_ref.dtype)

    always = pl.program_id(0) >= 0
    scores(0)
    for pr in range(n_pairs):
        @pl.when(always)
        def _(pr=pr):
            if pr + 1 < n_pairs:
                scores(pr + 1)
            attend(pr)


N_DR = 2 * NA_ROWS - 1


def _colbias_kernel(rpb_ref, onehot_ref, neg_ref, o_ref):
    o_ref[...] = (jnp.dot(rpb_ref[...], onehot_ref[...], preferred_element_type=F32) + neg_ref[...]).astype(o_ref.dtype)


def _na_bias_blocks(rpb):
    col = np.arange(GRID_W)
    col_start = np.clip(col - NA_COLS // 2, 0, GRID_W - NA_COLS)
    col_mask = (col[None, :] >= col_start[:, None]) & (col[None, :] < col_start[:, None] + NA_COLS)
    dc_idx = np.clip(col[None, :] - col[:, None] + NA_COLS - 1, 0, 2 * NA_COLS - 2)
    n_dc = 2 * NA_COLS - 1
    onehot = np.zeros((LANES, GRID_W, GRID_W), np.float32)
    kk, qq = np.meshgrid(col, col, indexing="ij")
    onehot[dc_idx[qq, kk], kk, qq] = col_mask[qq, kk]
    neg = np.where(col_mask.T, 0.0, NEG_BIG).astype(np.float32).reshape(1, GRID_W * GRID_W)
    rows = NA_HEADS * N_DR
    rows_pad = -(-rows // LANES) * LANES
    rpb2d = jnp.zeros((rows_pad, LANES), F32).at[:rows, :n_dc].set(rpb.astype(F32).reshape(rows, n_dc) * LOG2E)
    colbias = pl.pallas_call(
        _colbias_kernel,
        out_shape=jax.ShapeDtypeStruct((rows_pad, GRID_W * GRID_W), BF16),
        name="na_colbias",
    )(rpb2d, jnp.asarray(onehot.reshape(LANES, GRID_W * GRID_W), BF16), jnp.asarray(neg))
    colbias = colbias[:rows].reshape(NA_HEADS, N_DR, GRID_W, GRID_W)
    masked = jnp.full((NA_HEADS, 1, GRID_W, GRID_W), NEG_BIG, BF16)
    t = jnp.concatenate([colbias, masked], axis=1).reshape(NA_HEADS // 2, 2, N_DR + 1, GRID_W, GRID_W)
    return t.transpose(0, 2, 3, 1, 4).reshape(NA_HEADS // 2, N_DR + 1, GRID_W, 2 * GRID_W)


def neighbourhood_attention(p, kvc, rpb, dims):
    n_rows = dims.seq // GRID_W
    kr, n_union, ustarts, var_of_step, variants = _na_geometry(n_rows)
    n_steps = n_rows // NA_QROWS
    dr_idx = [[(jj - r_off + NA_ROWS - 1) if rs_off <= jj < rs_off + kr else N_DR
               for jj in range(n_union)] for var in (variants[v] for v in var_of_step) for r_off, rs_off in var]
    blocks = _na_bias_blocks(rpb)
    naw = dims.naw
    qb = dims.fw // naw
    tq = NA_QROWS * GRID_W
    kern = functools.partial(_na_kernel, n_union=n_union)
    return pl.pallas_call(
        kern,
        out_shape=jax.ShapeDtypeStruct((dims.batch * dims.seq, naw), BF16),
        grid_spec=pltpu.PrefetchScalarGridSpec(
            num_scalar_prefetch=2,
            grid=(dims.batch, n_steps),
            in_specs=[
                pl.BlockSpec((tq, naw), lambda b, r, dr, us: (b * n_steps + r, qb)),
                pl.BlockSpec((dims.seq, naw), lambda b, r, dr, us: (b, qb + 1)),
                pl.BlockSpec((dims.seq, naw), lambda b, r, dr, us: (b, qb + 2)),
                pl.BlockSpec((dims.ctx, naw), lambda b, r, dr, us: (b, 0)),
                pl.BlockSpec((dims.ctx, naw), lambda b, r, dr, us: (b, 1)),
                pl.BlockSpec(blocks.shape, lambda b, r, dr, us: (0, 0, 0, 0)),
            ],
            out_specs=pl.BlockSpec((tq, naw), lambda b, r, dr, us: (b * n_steps + r, 0)),
            scratch_shapes=[pltpu.VMEM((2, n_union * GRID_W + dims.ctx, 2 * tq), F32)],
        ),
        compiler_params=_cparams(("parallel", "arbitrary"), 48),
        name="neighbourhood_attention",
    )(jnp.asarray(np.asarray(dr_idx, np.int32).reshape(-1)), jnp.asarray(ustarts, I32), p, p, p, kvc, kvc, blocks)


CONV_HALO = 16
CONV_RB = 128
CONV_MM = 256
SUBLANES = 8


def _glu_conv_kernel(h_ref, wa_ref, wg_ref, ba_ref, bg_ref, dw_ref, db_ref, o_ref, scr, *, seq):
    tn = o_ref.shape[1]
    scr[0:CONV_HALO, :] = jnp.zeros((CONV_HALO, tn), F32)
    scr[CONV_HALO + seq:2 * CONV_HALO + seq, :] = jnp.zeros((CONV_HALO, tn), F32)
    base = CONV_HALO - CONV_K // 2
    n_shift = (CONV_K + base + SUBLANES - 1) // SUBLANES

    def project(c):
        row0 = c * CONV_MM
        h = h_ref[pl.ds(row0, CONV_MM), :]
        a = jnp.dot(h, wa_ref[...], preferred_element_type=F32) + ba_ref[...]
        gt = jnp.dot(h, wg_ref[...], preferred_element_type=F32) + bg_ref[...]
        scr[pl.ds(row0 + CONV_HALO, CONV_MM), :] = a * _sigmoid(gt)

    def conv_block(rb):
        row0 = rb * CONV_RB
        for cb in range(tn // LANES):
            cols = pl.ds(cb * LANES, LANES)
            acc = db_ref[:, cols]
            for ph in range(SUBLANES):
                part = None
                for st in range(n_shift):
                    k = st * SUBLANES + ph - base
                    if 0 <= k < CONV_K:
                        rows = pl.ds(row0 + st * SUBLANES, CONV_RB + SUBLANES)
                        term = scr[rows, cols] * dw_ref[k:k + 1, cols]
                        part = term if part is None else part + term
                acc = acc + part[ph:ph + CONV_RB]
            o_ref[pl.ds(row0, CONV_RB), cols] = acc.astype(o_ref.dtype)

    n_mm = seq // CONV_MM
    n_rb = seq // CONV_RB
    always = pl.program_id(0) >= 0
    project(0)
    done = 0
    for c in range(n_mm):
        ready = ((c + 1) * CONV_MM - CONV_K // 2) // CONV_RB if c + 1 < n_mm else n_rb

        @pl.when(always)
        def _(c=c, done=done, ready=ready):
            if c + 1 < n_mm:
                project(c + 1)
            for rb in range(done, ready):
                conv_block(rb)

        done = ready


def glu_conv(h, w, b, dw_w, dw_b, dims, tn):
    d = h.shape[1]
    half = w.shape[1] // 2
    nblk = half // tn
    b2 = b.reshape(1, 2 * half)
    kern = functools.partial(_glu_conv_kernel, seq=dims.seq)
    return pl.pallas_call(
        kern,
        out_shape=jax.ShapeDtypeStruct((dims.batch * dims.seq, half), BF16),
        grid=(dims.batch, nblk),
        in_specs=[
            pl.BlockSpec((dims.seq, d), lambda s, j: (s, 0)),
            pl.BlockSpec((d, tn), lambda s, j: (0, j)),
            pl.BlockSpec((d, tn), lambda s, j: (0, j + nblk)),
            pl.BlockSpec((1, tn), lambda s, j: (0, j)),
            pl.BlockSpec((1, tn), lambda s, j: (0, j + nblk)),
            pl.BlockSpec((CONV_K, tn), lambda s, j: (0, j)),
            pl.BlockSpec((1, tn), lambda s, j: (0, j)),
        ],
        out_specs=pl.BlockSpec((dims.seq, tn), lambda s, j: (s, j)),
        scratch_shapes=[pltpu.VMEM((dims.seq + 2 * CONV_HALO, tn), F32)],
        compiler_params=_cparams(("parallel", "arbitrary"), 48),
        name="glu_conv",
    )(h, w, w, b2, b2, dw_w, dw_b.reshape(1, half))


def _route(logits_t, rb):
    sel = [_sigmoid(logits_t[e:e + 1, :]) + rb[e:e + 1, :] for e in range(N_EXPERTS)]
    gs = []
    for g in range(N_GROUPS):
        v0, v1, v2, v3 = sel[4 * g:4 * g + 4]
        hi1, lo1 = jnp.maximum(v0, v1), jnp.minimum(v0, v1)
        hi2, lo2 = jnp.maximum(v2, v3), jnp.minimum(v2, v3)
        gs.append(jnp.maximum(hi1, hi2) + jnp.maximum(jnp.minimum(hi1, hi2), jnp.maximum(lo1, lo2)))
    best = gs[0]
    bg = jnp.zeros(best.shape, I32)
    for g in range(1, N_GROUPS):
        upd = gs[g] > best
        bg = jnp.where(upd, g, bg)
        best = jnp.where(upd, gs[g], best)
    v = []
    for i in range(EXPERTS_PER_GROUP):
        vi = sel[i]
        for g in range(1, N_GROUPS):
            vi = jnp.where(bg == g, sel[4 * g + i], vi)
        v.append(vi)
    picked = []
    for i in range(EXPERTS_PER_GROUP):
        rank = jnp.zeros(best.shape, I32)
        for j in range(EXPERTS_PER_GROUP):
            if j == i:
                continue
            ahead = (v[j] > v[i]) | ((v[j] == v[i]) & (j < i))
            rank = rank + ahead.astype(I32)
        picked.append(rank < 2)
    code = jnp.full(best.shape, len(PAIRS) - 1, I32)
    for idx in range(len(PAIRS) - 2, -1, -1):
        a, b = PAIRS[idx]
        code = jnp.where(picked[a] & picked[b], idx, code)
    return bg * len(PAIRS) + code


def _outproj_kernel(a_ref, b_ref, lng_ref, lnb_ref, w_ref, bias_ref, x_ref, mod_ref, g_ref, rw1_ref, rw2_ref,
                    rb_ref, x1_ref, h2_ref, cls_ref, *, ln_swish):
    ka = a_ref.shape[1]
    tm = a_ref.shape[0]
    if ln_swish:
        t = jnp.concatenate([a_ref[...], b_ref[...]], axis=-1).astype(F32)
        mu = jnp.mean(t, axis=-1, keepdims=True)
        tc = t - mu
        var = jnp.mean(tc * tc, axis=-1, keepdims=True)
        z = tc * lax.rsqrt(var + EPS) * lng_ref[...] + lnb_ref[...]
        z = (z * _sigmoid(z)).astype(BF16)
        y = jnp.dot(z, w_ref[...], preferred_element_type=F32) + bias_ref[...]
    else:
        y = jnp.dot(a_ref[...], w_ref[0:ka, :], preferred_element_type=F32)
        y = y + jnp.dot(b_ref[...], w_ref[ka:, :], preferred_element_type=F32) + bias_ref[...]
    x1 = x_ref[...] + mod_ref[0, 2:3, :] * y
    x1_ref[...] = x1
    h2 = _rms_mod(x1, g_ref[...], mod_ref[0, 4:5, :], mod_ref[0, 3:4, :])
    _store_packed_rows(h2_ref, h2, tm)
    hi = h2.astype(BF16)
    lo = (h2 - hi.astype(F32)).astype(BF16)
    s = jnp.dot(hi, rw1_ref[...], preferred_element_type=F32) + jnp.dot(lo, rw2_ref[...], preferred_element_type=F32)
    st = s.T
    logits_t = st[0:N_EXPERTS, :] + st[N_EXPERTS:2 * N_EXPERTS, :]
    cls_ref[...] = _route(logits_t, rb_ref[...])


def outproj_residual_route(a, a_col, b, b_col, ln_g, ln_b, w, bias, x2d, mod, ffn_g, router_w, router_b,
                           rows_per_mod, tm, ln_swish):
    m, d = x2d.shape
    ka = kb = w.shape[0] // 2
    kin = ka + kb
    n_slab = d // (2 * LANES)
    rw_hi = router_w.astype(BF16)
    rw_lo = (router_w - rw_hi.astype(F32)).astype(BF16)
    pad = jnp.zeros((d, LANES - 2 * N_EXPERTS), BF16)
    rw1 = jnp.concatenate([rw_hi, rw_lo, pad], axis=1)
    rw2 = jnp.concatenate([rw_hi, jnp.zeros((d, N_EXPERTS), BF16), pad], axis=1)
    kern = functools.partial(_outproj_kernel, ln_swish=ln_swish)
    return pl.pallas_call(
        kern,
        out_shape=(
            jax.ShapeDtypeStruct((m, d), F32),
            jax.ShapeDtypeStruct((m * n_slab, LANES), I32),
            jax.ShapeDtypeStruct((1, m), I32),
        ),
        grid=(m // tm,),
        in_specs=[
            pl.BlockSpec((tm, ka), lambda i: (i, a_col)),
            pl.BlockSpec((tm, kb), lambda i: (i, b_col)),
            pl.BlockSpec((1, kin), lambda i: (0, 0)),
            pl.BlockSpec((1, kin), lambda i: (0, 0)),
            pl.BlockSpec((kin, d), lambda i: (0, 0), pipeline_mode=pl.Buffered(1)),
            pl.BlockSpec((1, d), lambda i: (0, 0)),
            pl.BlockSpec((tm, d), lambda i: (i, 0)),
            pl.BlockSpec((1, N_MOD, d), lambda i: ((i * tm) // rows_per_mod, 0, 0)),
            pl.BlockSpec((1, d), lambda i: (0, 0)),
            pl.BlockSpec((d, LANES), lambda i: (0, 0)),
            pl.BlockSpec((d, LANES), lambda i: (0, 0)),
            pl.BlockSpec((N_EXPERTS, 1), lambda i: (0, 0)),
        ],
        out_specs=(
            pl.BlockSpec((tm, d), lambda i: (i, 0)),
            pl.BlockSpec((tm * n_slab, LANES), lambda i: (i, 0)),
            pl.BlockSpec((1, tm), lambda i: (0, i)),
        ),
        compiler_params=_cparams(("parallel",), 56),
        name="outproj_residual_route",
    )(a, b, ln_g.reshape(1, kin), ln_b.reshape(1, kin), w, bias.reshape(1, d), x2d, mod, ffn_g.reshape(1, d),
      rw1, rw2, router_b.astype(F32).reshape(N_EXPERTS, 1))


def _class_onehot(cls_ref):
    blk = cls_ref.shape[1]
    return (lax.broadcasted_iota(I32, (CLASS_PAD, blk), 0) == cls_ref[...]).astype(F32)


def _count_kernel(cls_ref, cnt_ref):
    @pl.when(pl.program_id(0) == 0)
    def _():
        cnt_ref[...] = jnp.zeros_like(cnt_ref)

    cnt_ref[...] += jnp.sum(_class_onehot(cls_ref), axis=1, keepdims=True)


def _dest_kernel(cls_ref, start_ref, dest_ref, carry_scr):
    @pl.when(pl.program_id(0) == 0)
    def _():
        carry_scr[...] = jnp.zeros_like(carry_scr)

    blk = cls_ref.shape[1]
    onehot = _class_onehot(cls_ref)
    tri = (lax.broadcasted_iota(I32, (blk, blk), 0) <= lax.broadcasted_iota(I32, (blk, blk), 1)).astype(BF16)
    cum = jnp.dot(onehot.astype(BF16), tri, preferred_element_type=F32)
    pos = cum - 1.0 + carry_scr[...] + start_ref[...]
    dest_ref[...] = jnp.sum(onehot * pos, axis=0, keepdims=True).astype(I32)
    carry_scr[...] += jnp.sum(onehot, axis=1, keepdims=True)


def sort_tokens(cls, tile):
    n = cls.shape[1]
    nb = n // SORT_BLK
    counts = pl.pallas_call(
        _count_kernel,
        out_shape=jax.ShapeDtypeStruct((CLASS_PAD, 1), F32),
        grid=(nb,),
        in_specs=[pl.BlockSpec((1, SORT_BLK), lambda j: (0, j))],
        out_specs=pl.BlockSpec((CLASS_PAD, 1), lambda j: (0, 0)),
        compiler_params=_cparams(("arbitrary",), 32),
        name="class_counts",
    )(cls)
    counts = counts.reshape(CLASS_PAD).astype(I32)
    padded = ((counts + tile - 1) // tile) * tile
    starts = (jnp.cumsum(padded) - padded).astype(F32).reshape(CLASS_PAD, 1)
    dest = pl.pallas_call(
        _dest_kernel,
        out_shape=jax.ShapeDtypeStruct((1, n), I32),
        grid=(nb,),
        in_specs=[pl.BlockSpec((1, SORT_BLK), lambda j: (0, j)),
                  pl.BlockSpec((CLASS_PAD, 1), lambda j: (0, 0))],
        out_specs=pl.BlockSpec((1, SORT_BLK), lambda j: (0, j)),
        scratch_shapes=[pltpu.VMEM((CLASS_PAD, 1), F32)],
        compiler_params=_cparams(("arbitrary",), 32),
        name="sorted_positions",
    )(cls, starts)
    return dest, counts


def _inverse_kernel(dest_ref, src_ref, *, n_tok, n_rows):
    def clear(k, carry):
        for u in range(SCALAR_UNROLL):
            src_ref[k * SCALAR_UNROLL + u] = 0
        return carry

    lax.fori_loop(0, n_rows // SCALAR_UNROLL, clear, 0)

    def scatter(k, carry):
        for u in range(SCALAR_UNROLL):
            t = k * SCALAR_UNROLL + u
            src_ref[dest_ref[t]] = t
        return carry

    lax.fori_loop(0, n_tok // SCALAR_UNROLL, scatter, 0)


def inverse_map(dest, n_rows):
    n_tok = dest.shape[0]
    kern = functools.partial(_inverse_kernel, n_tok=n_tok, n_rows=n_rows)
    return pl.pallas_call(
        kern,
        out_shape=jax.ShapeDtypeStruct((n_rows,), I32),
        in_specs=[pl.BlockSpec(memory_space=pltpu.SMEM)],
        out_specs=pl.BlockSpec(memory_space=pltpu.SMEM),
        name="inverse_map",
    )(dest)


def _start_row_gather(idx_ref, base, src_hbm, buf, sem, slot, tm, s):
    def body(k, carry):
        for u in range(SCALAR_UNROLL):
            r = k * SCALAR_UNROLL + u
            pltpu.make_async_copy(
                src_hbm.at[pl.ds(pl.multiple_of(idx_ref[base + r] * s, s), s), :],
                buf.at[slot, pl.ds(pl.multiple_of(r * s, s), s), :],
                sem.at[slot]).start(priority=u % 2)
        return carry

    lax.fori_loop(0, tm // SCALAR_UNROLL, body, 0)


def _wait_row_gather(src_hbm, buf, sem, slot, tm, s):
    pltpu.make_async_copy(src_hbm.at[pl.ds(0, tm * s), :], buf.at[slot], sem.at[slot]).wait()


HI_HALF = -65536


def _store_packed_rows(ref, v, tm):
    half = v.shape[1] // 2
    n_slab = half // LANES
    for c in range(n_slab):
        a = v[:, c * LANES:(c + 1) * LANES].astype(BF16).astype(F32)
        b = v[:, half + c * LANES:half + (c + 1) * LANES].astype(BF16).astype(F32)
        word = lax.bitcast_convert_type(a, I32) | lax.shift_right_logical(lax.bitcast_convert_type(b, I32), 16)
        ref[pl.ds(c, tm, stride=n_slab), :] = word


def _load_packed_rows(ref, tm, n_slab):
    words = [ref[pl.ds(c, tm, stride=n_slab), :] for c in range(n_slab)]
    hi = [lax.bitcast_convert_type(w & HI_HALF, F32) for w in words]
    lo = [lax.bitcast_convert_type(lax.shift_left(w, 16), F32) for w in words]
    return jnp.concatenate(hi + lo, axis=-1)


def _moe_kernel(src_ref, e1_ref, e2_ref, valid_ref, h_hbm, wg1_ref, wg2_ref, wu1_ref, wu2_ref,
                wd1_ref, wd2_ref, rwt_ref, ys_ref, buf, sem, *, tm, n_slab, n_tiles):
    i = pl.program_id(0)
    slot = i % 2

    @pl.when(i == 0)
    def _():
        _start_row_gather(src_ref, 0, h_hbm, buf, sem, 0, tm, n_slab)

    nxt = jnp.minimum(i + 1, n_tiles - 1)

    @pl.when((i + 1 < n_tiles) & (valid_ref[nxt] == 1))
    def _():
        _start_row_gather(src_ref, nxt * tm, h_hbm, buf, sem, 1 - slot, tm, n_slab)

    @pl.when(valid_ref[i] == 0)
    def _():
        ys_ref[...] = jnp.zeros_like(ys_ref)

    @pl.when(valid_ref[i] == 1)
    def _():
        _wait_row_gather(h_hbm, buf, sem, slot, tm, n_slab)
        xr = _load_packed_rows(buf.at[slot], tm, n_slab)
        x = xr
        l1 = jnp.sum(xr * rwt_ref[pl.ds(e1_ref[i], 1), :], axis=-1, keepdims=True)
        l2 = jnp.sum(xr * rwt_ref[pl.ds(e2_ref[i], 1), :], axis=-1, keepdims=True)
        s1 = _sigmoid(l1)
        s2 = _sigmoid(l2)
        tot = s1 + s2

        def expert(wg_ref, wu_ref, weight):
            gt = jnp.dot(x, wg_ref[0], preferred_element_type=F32)
            up = jnp.dot(x, wu_ref[0], preferred_element_type=F32)
            return gt * _sigmoid(gt) * up * weight

        a1 = expert(wg1_ref, wu1_ref, s1 / tot)
        a2 = expert(wg2_ref, wu2_ref, s2 / tot)
        y = jnp.dot(a1, wd1_ref[0], preferred_element_type=F32)
        y = y + jnp.dot(a2, wd2_ref[0], preferred_element_type=F32)
        _store_packed_rows(ys_ref, y, tm)


def moe_ffn(src, tile_e1, tile_e2, tile_valid, h_rows, w_gate, w_up, w_down, layer, router_w_t, d, tm):
    n_tiles = tile_e1.shape[0]
    n_slab = d // (2 * LANES)
    fe = w_gate.shape[3]
    kern = functools.partial(_moe_kernel, tm=tm, n_slab=n_slab, n_tiles=n_tiles)

    def w_in_spec(which):
        return pl.BlockSpec((None, 1, d, fe), lambda i, src, e1, e2, valid: (layer, (e1, e2)[which][i], 0, 0))

    def w_out_spec(which):
        return pl.BlockSpec((None, 1, fe, d), lambda i, src, e1, e2, valid: (layer, (e1, e2)[which][i], 0, 0),
                            pipeline_mode=pl.Buffered(1))

    return pl.pallas_call(
        kern,
        out_shape=jax.ShapeDtypeStruct((n_tiles * tm * n_slab, LANES), I32),
        grid_spec=pltpu.PrefetchScalarGridSpec(
            num_scalar_prefetch=4,
            grid=(n_tiles,),
            in_specs=[
                pl.BlockSpec(memory_space=pl.ANY),
                w_in_spec(0), w_in_spec(1), w_in_spec(0), w_in_spec(1), w_out_spec(0), w_out_spec(1),
                pl.BlockSpec((N_EXPERTS, d), lambda i, src, e1, e2, valid: (0, 0)),
            ],
            out_specs=pl.BlockSpec((tm * n_slab, LANES), lambda i, src, e1, e2, valid: (i, 0)),
            scratch_shapes=[pltpu.VMEM((2, tm * n_slab, LANES), I32), pltpu.SemaphoreType.DMA((2,))],
        ),
        compiler_params=_cparams(("arbitrary",), 58),
        name="moe_ffn",
    )(src, tile_e1, tile_e2, tile_valid, h_rows, w_gate, w_gate, w_up, w_up, w_down, w_down, router_w_t)


def _combine_kernel(dest_ref, ys_hbm, x1_ref, mod_ref, g_ref, modn_ref, *rest, tm, s, n_tiles, final):
    if final:
        o_ref, buf, sem = rest
    else:
        o_ref, hn_ref, buf, sem = rest
    i = pl.program_id(0)
    slot = i % 2

    @pl.when(i == 0)
    def _():
        _start_row_gather(dest_ref, 0, ys_hbm, buf, sem, 0, tm, s)

    @pl.when(i + 1 < n_tiles)
    def _():
        _start_row_gather(dest_ref, (i + 1) * tm, ys_hbm, buf, sem, 1 - slot, tm, s)

    _wait_row_gather(ys_hbm, buf, sem, slot, tm, s)
    y = _load_packed_rows(buf.at[slot], tm, s)
    x2 = x1_ref[...] + mod_ref[0, 5:6, :] * y
    if final:
        ms = jnp.mean(x2 * x2, axis=-1, keepdims=True)
        o_ref[...] = x2 * lax.rsqrt(ms + EPS) * g_ref[...]
    else:
        o_ref[...] = x2
        hn_ref[...] = _rms_mod(x2, g_ref[...], modn_ref[0, 1:2, :], modn_ref[0, 0:1, :]).astype(hn_ref.dtype)


def combine_residual(dest, ys, x1, mod, g, mod_next, rows_per_mod, tm, final):
    m, d = x1.shape
    s = d // (2 * LANES)
    n_tiles = m // tm
    kern = functools.partial(_combine_kernel, tm=tm, s=s, n_tiles=n_tiles, final=final)
    row_spec = pl.BlockSpec((tm, d), lambda i, dest: (i, 0))
    mod_spec = pl.BlockSpec((1, N_MOD, d), lambda i, dest: ((i * tm) // rows_per_mod, 0, 0))
    if final:
        out_shape, out_specs = jax.ShapeDtypeStruct((m, d), F32), row_spec
    else:
        out_shape = (jax.ShapeDtypeStruct((m, d), F32), jax.ShapeDtypeStruct((m, d), BF16))
        out_specs = (row_spec, row_spec)
    return pl.pallas_call(
        kern,
        out_shape=out_shape,
        grid_spec=pltpu.PrefetchScalarGridSpec(
            num_scalar_prefetch=1,
            grid=(n_tiles,),
            in_specs=[
                pl.BlockSpec(memory_space=pl.ANY),
                row_spec,
                mod_spec,
                pl.BlockSpec((1, d), lambda i, dest: (0, 0)),
                mod_spec,
            ],
            out_specs=out_specs,
            scratch_shapes=[pltpu.VMEM((2, tm * s, LANES), I32), pltpu.SemaphoreType.DMA((2,))],
        ),
        compiler_params=_cparams(("arbitrary",), 40),
        name="combine_residual",
    )(dest, ys, x1, mod, g.reshape(1, d), mod_next)


def _tile_tables(counts, n_tiles, tm):
    tiles_per_class = (counts[:N_CLASSES] + tm - 1) // tm
    tile_end = jnp.cumsum(tiles_per_class)
    total = tile_end[-1]
    idx = jnp.arange(n_tiles, dtype=I32)
    valid = idx < total
    blk = jnp.where(valid, idx, total - 1)
    cls = jnp.minimum(jnp.sum((tile_end[None, :] <= blk[:, None]).astype(I32), axis=1), N_CLASSES - 1)
    pair = jnp.asarray(PAIRS, I32)
    grp = cls // len(PAIRS)
    e1 = grp * EXPERTS_PER_GROUP + pair[cls % len(PAIRS), 0]
    e2 = grp * EXPERTS_PER_GROUP + pair[cls % len(PAIRS), 1]
    return e1.astype(I32), e2.astype(I32), valid.astype(I32)


def grouped_moe_residual(x1, h2_rows, cls, mod, router_w, w_gate, w_up, w_down, layer, g, mod_next,
                         rows_per_mod, final):
    m, d = x1.shape
    n_tiles = m // MOE_TM + N_CLASSES
    dest, counts = sort_tokens(cls, MOE_TM)
    dest = dest.reshape(m)
    e1, e2, valid = _tile_tables(counts, n_tiles, MOE_TM)
    src = inverse_map(dest, n_tiles * MOE_TM)
    ys = moe_ffn(src, e1, e2, valid, h2_rows, w_gate, w_up, w_down, layer, router_w.T.astype(F32), d, MOE_TM)
    return combine_residual(dest, ys, x1, mod, g, mod_next, rows_per_mod, GATHER_TM, final)


def _forward(dims, x, c, ctx, c_ctx, ada_w, ada_b, mix_norm_g, ffn_norm_g, ev_w_in, ev_rpb, ev_w_out,
             od_w_in, od_b_in, od_dw_w, od_dw_b, od_ln_g, od_ln_b, od_w_out, od_b_out,
             router_w, router_b, moe_w_gate, moe_w_up, moe_w_down, final_norm_g, *, tiles):
    b, seq, d = dims.batch, dims.seq, dims.d
    n = b * seq
    depth = ada_w.shape[0]
    assert depth == 2, "layer 0 is the Fourier/attention mixer, layer 1 the Conformer mixer"
    x2d = x.reshape(n, d)
    ctx2d = ctx.reshape(b * dims.ctx, d)

    cvec = jnp.zeros((16, d), F32).at[:b].set(c).at[b].set(c_ctx)
    mod_all = adaln_mod(cvec, ada_w, ada_b, tiles["mod_tn"]).reshape(depth, 16, N_MOD, d)

    mod0 = mod_all[0, :b]
    modc0 = mod_all[0, b:b + 1]
    w_in = ev_w_in[0].astype(BF16)
    p = norm_mod_matmul(x2d, mix_norm_g[0], mod0, w_in, seq, 0, 1, tiles["in_tm"], w_in.shape[1], 0)
    kvc = norm_mod_matmul(ctx2d, mix_norm_g[0], modc0, w_in, b * dims.ctx, 0, 1,
                          min(tiles["in_tm"], b * dims.ctx), 2 * dims.naw, 1)
    fo = fourier_mix(p, dims, tiles["four_tm"])
    ao = neighbourhood_attention(p, kvc, ev_rpb[0], dims)
    mod1 = mod_all[1, :b]
    mix_w = ev_w_out.shape[1]
    x1, h2_rows, cls = outproj_residual_route(
        fo, 0, ao, 0, jnp.ones((mix_w,), F32), jnp.zeros((mix_w,), F32), ev_w_out[0].astype(BF16),
        jnp.zeros((d,), F32), x2d, mod0, ffn_norm_g[0], router_w, router_b, seq, tiles["out_tm"], False)
    x2, hmix1 = grouped_moe_residual(x1, h2_rows, cls, mod0, router_w, moe_w_gate, moe_w_up, moe_w_down, 0,
                                     mix_norm_g[1], mod1, seq, False)

    yc = glu_conv(hmix1, od_w_in[0].astype(BF16), od_b_in[0], od_dw_w[0], od_dw_b[0], dims, tiles["glu_tn"])
    x3, h4_rows, cls1 = outproj_residual_route(
        yc, 0, yc, 1, od_ln_g[0], od_ln_b[0], od_w_out[0].astype(BF16), od_b_out[0], x2, mod1, ffn_norm_g[1],
        router_w, router_b, seq, tiles["out_tm"], True)
    out = grouped_moe_residual(x3, h4_rows, cls1, mod1, router_w, moe_w_gate, moe_w_up, moe_w_down, 1,
                               final_norm_g, mod1, seq, True)
    return out.reshape(b, seq, d)


TILES = dict(mod_tn=1024, in_tm=256, glu_tn=256, four_tm=512, out_tm=256)


def kernel(x, c, ctx, c_ctx, ada_w, ada_b, mix_norm_g, ffn_norm_g, ev_w_in, ev_rpb, ev_w_out, od_w_in, od_b_in,
           od_dw_w, od_dw_b, od_ln_g, od_ln_b, od_w_out, od_b_out, router_w, router_b, moe_w_gate, moe_w_up,
           moe_w_down, final_norm_g):
    b, seq, d = x.shape
    dims = Dims(batch=b, seq=seq, d=d, ctx=ctx.shape[1], fw=ev_w_out.shape[1] - NA_HEADS * NA_HEAD_DIM,
                naw=NA_HEADS * NA_HEAD_DIM, fe=moe_w_gate.shape[3])
    return _forward(dims, x, c, ctx, c_ctx, ada_w, ada_b, mix_norm_g, ffn_norm_g, ev_w_in, ev_rpb, ev_w_out,
                    od_w_in, od_b_in, od_dw_w, od_dw_b, od_ln_g, od_ln_b, od_w_out, od_b_out,
                    router_w, router_b, moe_w_gate, moe_w_up, moe_w_down, final_norm_g, tiles=TILES)
```

```python
import functools
import math
from typing import NamedTuple

import jax
import jax.numpy as jnp
import numpy as np
from jax import lax
from jax.experimental import pallas as pl
from jax.experimental.pallas import tpu as pltpu

F32 = jnp.float32
BF16 = jnp.bfloat16
I32 = jnp.int32

LANES = 128
NEG_BIG = -1e30
EPS = 1e-6
LOG2E = math.log2(math.e)

N_MOD = 6
N_FOURIER_GROUPS = 4
NA_HEADS = 16
NA_HEAD_DIM = 64
NA_ROWS = 8
NA_COLS = 16
GRID_W = 64
CONV_K = 31
N_EXPERTS = 16
N_GROUPS = 4
EXPERTS_PER_GROUP = 4
PAIRS = ((0, 1), (0, 2), (0, 3), (1, 3), (1, 2), (2, 3))
N_CLASSES = N_GROUPS * len(PAIRS)
CLASS_PAD = 32
MOE_TM = 256
SORT_BLK = 512
GATHER_TM = 512
SCALAR_UNROLL = 8


class Dims(NamedTuple):
    batch: int
    seq: int
    d: int
    ctx: int
    fw: int
    naw: int
    fe: int


def _cparams(sem, vmem_mb):
    return pltpu.CompilerParams(dimension_semantics=sem, vmem_limit_bytes=vmem_mb << 20)


def _sigmoid(x):
    return 1.0 / (1.0 + jnp.exp(-x))


def _rms_mod(x, g, scale, shift):
    ms = jnp.mean(x * x, axis=-1, keepdims=True)
    return (x * lax.rsqrt(ms + EPS) * g) * (1.0 + scale) + shift


def _mod_kernel(c_ref, w_ref, b_ref, o_ref):
    c = c_ref[...]
    s = (c * _sigmoid(c)).astype(BF16)
    o_ref[0] = jnp.dot(s, w_ref[0].astype(BF16), preferred_element_type=F32) + b_ref[0]


def adaln_mod(cvec, ada_w, ada_b, tn):
    depth, d, n = ada_w.shape
    rows = cvec.shape[0]
    return pl.pallas_call(
        _mod_kernel,
        out_shape=jax.ShapeDtypeStruct((depth, rows, n), F32),
        grid=(depth, n // tn),
        in_specs=[
            pl.BlockSpec((rows, d), lambda l, j: (0, 0)),
            pl.BlockSpec((1, d, tn), lambda l, j: (l, 0, j)),
            pl.BlockSpec((1, 1, tn), lambda l, j: (l, 0, j)),
        ],
        out_specs=pl.BlockSpec((1, rows, tn), lambda l, j: (l, 0, j)),
        compiler_params=_cparams(("parallel", "parallel"), 40),
        name="adaln_mod",
    )(cvec, ada_w, ada_b.reshape(depth, 1, n))


def _inproj_kernel(x0_ref, xn_ref, g_ref, mod0_ref, modn_ref, w_ref, o_ref, h_scr, *, shift_idx, scale_idx):
    i = pl.program_id(0)

    def norm_into(x_ref, mod_ref, slot):
        h = _rms_mod(x_ref[...], g_ref[...], mod_ref[0, scale_idx:scale_idx + 1, :],
                     mod_ref[0, shift_idx:shift_idx + 1, :])
        h_scr[slot] = h.astype(BF16)

    @pl.when(i == 0)
    def _():
        norm_into(x0_ref, mod0_ref, 0)

    def step(cur, nxt):
        o_ref[...] = jnp.dot(h_scr[cur], w_ref[...], preferred_element_type=F32).astype(o_ref.dtype)
        norm_into(xn_ref, modn_ref, nxt)

    @pl.when(i % 2 == 0)
    def _():
        step(0, 1)

    @pl.when(i % 2 == 1)
    def _():
        step(1, 0)


def norm_mod_matmul(x2d, g, mod, w, rows_per_mod, shift_idx, scale_idx, tm, n, col_blk):
    m, d = x2d.shape
    last = m // tm - 1
    kern = functools.partial(_inproj_kernel, shift_idx=shift_idx, scale_idx=scale_idx)

    def nxt(i):
        return jnp.minimum(i + 1, last)

    return pl.pallas_call(
        kern,
        out_shape=jax.ShapeDtypeStruct((m, n), BF16),
        grid=(m // tm,),
        in_specs=[
            pl.BlockSpec((tm, d), lambda i: (0, 0)),
            pl.BlockSpec((tm, d), lambda i: (nxt(i), 0)),
            pl.BlockSpec((1, d), lambda i: (0, 0)),
            pl.BlockSpec((1, N_MOD, d), lambda i: (0, 0, 0)),
            pl.BlockSpec((1, N_MOD, d), lambda i: ((nxt(i) * tm) // rows_per_mod, 0, 0)),
            pl.BlockSpec((d, n), lambda i: (0, col_blk), pipeline_mode=pl.Buffered(1)),
        ],
        out_specs=pl.BlockSpec((tm, n), lambda i: (i, 0)),
        scratch_shapes=[pltpu.VMEM((2, tm, d), BF16)],
        compiler_params=_cparams(("arbitrary",), 48),
        name="norm_mod_matmul",
    )(x2d, x2d, g.reshape(1, d), mod, mod, w)


def _cast_kernel(x_ref, o_ref):
    o_ref[...] = x_ref[...].astype(o_ref.dtype)


def cast_experts_bf16(w, layer):
    _, e, k, n = w.shape
    return pl.pallas_call(
        _cast_kernel,
        out_shape=jax.ShapeDtypeStruct((e, k, n), BF16),
        grid=(e,),
        in_specs=[pl.BlockSpec((None, 1, k, n), lambda i: (layer, i, 0, 0))],
        out_specs=pl.BlockSpec((1, k, n), lambda i: (i, 0, 0)),
        compiler_params=_cparams(("parallel",), 32),
        name="cast_experts_bf16",
    )(w)


def _fourier_kernel(u_ref, cs_ref, cl_ref, sl_ref, o_ref, xc_scr, xs_scr, *, gc):
    @pl.when(pl.program_id(1) == 0)
    def _():
        for grp in range(N_FOURIER_GROUPS):
            r = jnp.dot(u_ref[:, grp * gc:(grp + 1) * gc], cs_ref[...], preferred_element_type=F32)
            xc_scr[:, grp * gc:(grp + 1) * gc] = r[:, :gc].astype(BF16)
            xs_scr[:, grp * gc:(grp + 1) * gc] = r[:, gc:].astype(BF16)

    acc = jnp.dot(cl_ref[...], xc_scr[...], preferred_element_type=F32)
    acc = acc - jnp.dot(sl_ref[...], xs_scr[...], preferred_element_type=F32)
    o_ref[...] = acc.astype(o_ref.dtype)


def _dft_tables(length, scale):
    kn = np.outer(np.arange(length), np.arange(length)) % length
    ang = kn.astype(np.float64) * (2.0 * math.pi / length)
    return (np.cos(ang) * scale).astype(np.float32), (np.sin(ang) * scale).astype(np.float32)


def fourier_mix(p, dims, tm):
    gc = dims.fw // N_FOURIER_GROUPS
    cc, sc = _dft_tables(gc, 1.0 / math.sqrt(dims.seq * gc))
    cs = jnp.asarray(np.concatenate([cc, sc], axis=1).astype(BF16))
    cl, sl = (jnp.asarray(t.astype(BF16)) for t in _dft_tables(dims.seq, 1.0))
    kern = functools.partial(_fourier_kernel, gc=gc)
    return pl.pallas_call(
        kern,
        out_shape=jax.ShapeDtypeStruct((dims.batch * dims.seq, dims.fw), BF16),
        grid=(dims.batch, dims.seq // tm),
        in_specs=[
            pl.BlockSpec((dims.seq, dims.fw), lambda b, m: (b, 0)),
            pl.BlockSpec((gc, 2 * gc), lambda b, m: (0, 0)),
            pl.BlockSpec((tm, dims.seq), lambda b, m: (m, 0)),
            pl.BlockSpec((tm, dims.seq), lambda b, m: (m, 0)),
        ],
        out_specs=pl.BlockSpec((tm, dims.fw), lambda b, m: (b * (dims.seq // tm) + m, 0)),
        scratch_shapes=[pltpu.VMEM((dims.seq, dims.fw), BF16), pltpu.VMEM((dims.seq, dims.fw), BF16)],
        compiler_params=_cparams(("parallel", "arbitrary"), 48),
        name="fourier_mix",
    )(p, cs, cl, sl)


NA_QROWS = 2


def _na_geometry(n_rows):
    kr = min(NA_ROWS, n_rows)
    n_union = kr + NA_QROWS - 1
    steps = []
    for rp in range(n_rows // NA_QROWS):
        rows = [rp * NA_QROWS + a for a in range(NA_QROWS)]
        starts = [min(max(r - kr // 2, 0), n_rows - kr) for r in rows]
        u = min(starts[0], n_rows - n_union)
        steps.append((u, tuple((r - u, rs - u) for r, rs in zip(rows, starts))))
    variants = sorted(set(v for _, v in steps))
    return kr, n_union, [u for u, _ in steps], [variants.index(v) for _, v in steps], variants


def _na_kernel(dr_ref, ustart_ref, q_ref, k_ref, v_ref, kc_ref, vc_ref, t_ref, o_ref, s_scr, *, n_union):
    rp = pl.program_id(1)
    row0 = pl.multiple_of(ustart_ref[rp] * GRID_W, GRID_W)
    nloc = n_union * GRID_W
    nctx = kc_ref.shape[0]
    low = lax.broadcasted_iota(I32, (GRID_W, LANES), 1) < NA_HEAD_DIM
    nt = (((1,), (1,)), ((), ()))
    tn = (((0,), (0,)), ((), ()))
    scale = NA_HEAD_DIM ** -0.5
    n_pairs = NA_HEADS // 2

    def scores(pr):
        cols = pl.ds(pr * LANES, LANES)
        q2 = (q_ref[:, cols].astype(F32) * (scale * LOG2E)).astype(BF16)
        parts = []
        for a in range(NA_QROWS):
            qa = q2[a * GRID_W:(a + 1) * GRID_W]
            zero = jnp.zeros_like(qa)
            parts += [jnp.where(low, qa, zero), jnp.where(low, zero, qa)]
        qm = jnp.concatenate(parts, axis=0)
        kp = k_ref[pl.ds(row0, nloc), cols]
        loc = lax.dot_general(kp, qm, nt, preferred_element_type=F32)
        for jj in range(n_union):
            for a in range(NA_QROWS):
                bias = t_ref[pr, dr_ref[(rp * NA_QROWS + a) * n_union + jj]].astype(F32)
                rows = slice(jj * GRID_W, (jj + 1) * GRID_W)
                lanes = slice(a * 2 * GRID_W, (a + 1) * 2 * GRID_W)
                s_scr[pr % 2, rows, lanes] = loc[rows, lanes] + bias
        s_scr[pr % 2, nloc:nloc + nctx, :] = lax.dot_general(kc_ref[:, cols], qm, nt, preferred_element_type=F32)

    def attend(pr):
        cols = pl.ds(pr * LANES, LANES)
        s = s_scr[pr % 2]
        m = jnp.max(s, axis=0, keepdims=True)
        e = jnp.exp2(s - m)
        den = jnp.sum(e, axis=0, keepdims=True)
        eb = e.astype(BF16)
        vp = v_ref[pl.ds(row0, nloc), cols]
        ot = lax.dot_general(vp, eb[0:nloc], tn, preferred_element_type=F32)
        ot = ot + lax.dot_general(vc_ref[:, cols], eb[nloc:], tn, preferred_element_type=F32)
        o = (ot / den).T
        for a in range(NA_QROWS):
            base = a * 2 * GRID_W
            o_ref[a * GRID_W:(a + 1) * GRID_W, cols] = jnp.where(
                low, o[base:base + GRID_W], o[base + GRID_W:base + 2 * GRID_W]).astype(o_ref.dtype)

    always = pl.program_id(0) >= 0
    scores(0)
    for pr in range(n_pairs):
        @pl.when(always)
        def _(pr=pr):
            if pr + 1 < n_pairs:
                scores(pr + 1)
            attend(pr)


N_DR = 2 * NA_ROWS - 1


def _colbias_kernel(rpb_ref, onehot_ref, neg_ref, o_ref):
    o_ref[...] = (jnp.dot(rpb_ref[...], onehot_ref[...], preferred_element_type=F32) + neg_ref[...]).astype(o_ref.dtype)


def _na_bias_blocks(rpb):
    col = np.arange(GRID_W)
    col_start = np.clip(col - NA_COLS // 2, 0, GRID_W - NA_COLS)
    col_mask = (col[None, :] >= col_start[:, None]) & (col[None, :] < col_start[:, None] + NA_COLS)
    dc_idx = np.clip(col[None, :] - col[:, None] + NA_COLS - 1, 0, 2 * NA_COLS - 2)
    n_dc = 2 * NA_COLS - 1
    onehot = np.zeros((LANES, GRID_W, GRID_W), np.float32)
    kk, qq = np.meshgrid(col, col, indexing="ij")
    onehot[dc_idx[qq, kk], kk, qq] = col_mask[qq, kk]
    neg = np.where(col_mask.T, 0.0, NEG_BIG).astype(np.float32).reshape(1, GRID_W * GRID_W)
    rows = NA_HEADS * N_DR
    rows_pad = -(-rows // LANES) * LANES
    rpb2d = jnp.zeros((rows_pad, LANES), F32).at[:rows, :n_dc].set(rpb.astype(F32).reshape(rows, n_dc) * LOG2E)
    colbias = pl.pallas_call(
        _colbias_kernel,
        out_shape=jax.ShapeDtypeStruct((rows_pad, GRID_W * GRID_W), BF16),
        name="na_colbias",
    )(rpb2d, jnp.asarray(onehot.reshape(LANES, GRID_W * GRID_W), BF16), jnp.asarray(neg))
    colbias = colbias[:rows].reshape(NA_HEADS, N_DR, GRID_W, GRID_W)
    masked = jnp.full((NA_HEADS, 1, GRID_W, GRID_W), NEG_BIG, BF16)
    t = jnp.concatenate([colbias, masked], axis=1).reshape(NA_HEADS // 2, 2, N_DR + 1, GRID_W, GRID_W)
    return t.transpose(0, 2, 3, 1, 4).reshape(NA_HEADS // 2, N_DR + 1, GRID_W, 2 * GRID_W)


def neighbourhood_attention(p, kvc, rpb, dims):
    n_rows = dims.seq // GRID_W
    kr, n_union, ustarts, var_of_step, variants = _na_geometry(n_rows)
    n_steps = n_rows // NA_QROWS
    dr_idx = [[(jj - r_off + NA_ROWS - 1) if rs_off <= jj < rs_off + kr else N_DR
               for jj in range(n_union)] for var in (variants[v] for v in var_of_step) for r_off, rs_off in var]
    blocks = _na_bias_blocks(rpb)
    naw = dims.naw
    qb = dims.fw // naw
    tq = NA_QROWS * GRID_W
    kern = functools.partial(_na_kernel, n_union=n_union)
    return pl.pallas_call(
        kern,
        out_shape=jax.ShapeDtypeStruct((dims.batch * dims.seq, naw), BF16),
        grid_spec=pltpu.PrefetchScalarGridSpec(
            num_scalar_prefetch=2,
            grid=(dims.batch, n_steps),
            in_specs=[
                pl.BlockSpec((tq, naw), lambda b, r, dr, us: (b * n_steps + r, qb)),
                pl.BlockSpec((dims.seq, naw), lambda b, r, dr, us: (b, qb + 1)),
                pl.BlockSpec((dims.seq, naw), lambda b, r, dr, us: (b, qb + 2)),
                pl.BlockSpec((dims.ctx, naw), lambda b, r, dr, us: (b, 0)),
                pl.BlockSpec((dims.ctx, naw), lambda b, r, dr, us: (b, 1)),
                pl.BlockSpec(blocks.shape, lambda b, r, dr, us: (0, 0, 0, 0)),
            ],
            out_specs=pl.BlockSpec((tq, naw), lambda b, r, dr, us: (b * n_steps + r, 0)),
            scratch_shapes=[pltpu.VMEM((2, n_union * GRID_W + dims.ctx, 2 * tq), F32)],
        ),
        compiler_params=_cparams(("parallel", "arbitrary"), 48),
        name="neighbourhood_attention",
    )(jnp.asarray(np.asarray(dr_idx, np.int32).reshape(-1)), jnp.asarray(ustarts, I32), p, p, p, kvc, kvc, blocks)


CONV_HALO = 16
CONV_RB = 128
CONV_MM = 256
SUBLANES = 8


def _glu_conv_kernel(h_ref, wa_ref, wg_ref, ba_ref, bg_ref, dw_ref, db_ref, o_ref, scr, *, seq):
    tn = o_ref.shape[1]
    scr[0:CONV_HALO, :] = jnp.zeros((CONV_HALO, tn), F32)
    scr[CONV_HALO + seq:2 * CONV_HALO + seq, :] = jnp.zeros((CONV_HALO, tn), F32)
    base = CONV_HALO - CONV_K // 2
    n_shift = (CONV_K + base + SUBLANES - 1) // SUBLANES

    def project(c):
        row0 = c * CONV_MM
        h = h_ref[pl.ds(row0, CONV_MM), :]
        a = jnp.dot(h, wa_ref[...], preferred_element_type=F32) + ba_ref[...]
        gt = jnp.dot(h, wg_ref[...], preferred_element_type=F32) + bg_ref[...]
        scr[pl.ds(row0 + CONV_HALO, CONV_MM), :] = a * _sigmoid(gt)

    def conv_block(rb):
        row0 = rb * CONV_RB
        for cb in range(tn // LANES):
            cols = pl.ds(cb * LANES, LANES)
            acc = db_ref[:, cols]
            for ph in range(SUBLANES):
                part = None
                for st in range(n_shift):
                    k = st * SUBLANES + ph - base
                    if 0 <= k < CONV_K:
                        rows = pl.ds(row0 + st * SUBLANES, CONV_RB + SUBLANES)
                        term = scr[rows, cols] * dw_ref[k:k + 1, cols]
                        part = term if part is None else part + term
                acc = acc + part[ph:ph + CONV_RB]
            o_ref[pl.ds(row0, CONV_RB), cols] = acc.astype(o_ref.dtype)

    n_mm = seq // CONV_MM
    n_rb = seq // CONV_RB
    always = pl.program_id(0) >= 0
    project(0)
    done = 0
    for c in range(n_mm):
        ready = ((c + 1) * CONV_MM - CONV_K // 2) // CONV_RB if c + 1 < n_mm else n_rb

        @pl.when(always)
        def _(c=c, done=done, ready=ready):
            if c + 1 < n_mm:
                project(c + 1)
            for rb in range(done, ready):
                conv_block(rb)

        done = ready


def glu_conv(h, w, b, dw_w, dw_b, dims, tn):
    d = h.shape[1]
    half = w.shape[1] // 2
    nblk = half // tn
    b2 = b.reshape(1, 2 * half)
    kern = functools.partial(_glu_conv_kernel, seq=dims.seq)
    return pl.pallas_call(
        kern,
        out_shape=jax.ShapeDtypeStruct((dims.batch * dims.seq, half), BF16),
        grid=(dims.batch, nblk),
        in_specs=[
            pl.BlockSpec((dims.seq, d), lambda s, j: (s, 0)),
            pl.BlockSpec((d, tn), lambda s, j: (0, j)),
            pl.BlockSpec((d, tn), lambda s, j: (0, j + nblk)),
            pl.BlockSpec((1, tn), lambda s, j: (0, j)),
            pl.BlockSpec((1, tn), lambda s, j: (0, j + nblk)),
            pl.BlockSpec((CONV_K, tn), lambda s, j: (0, j)),
            pl.BlockSpec((1, tn), lambda s, j: (0, j)),
        ],
        out_specs=pl.BlockSpec((dims.seq, tn), lambda s, j: (s, j)),
        scratch_shapes=[pltpu.VMEM((dims.seq + 2 * CONV_HALO, tn), F32)],
        compiler_params=_cparams(("parallel", "arbitrary"), 48),
        name="glu_conv",
    )(h, w, w, b2, b2, dw_w, dw_b.reshape(1, half))


def _route(logits_t, rb):
    sel = [_sigmoid(logits_t[e:e + 1, :]) + rb[e:e + 1, :] for e in range(N_EXPERTS)]
    gs = []
    for g in range(N_GROUPS):
        v0, v1, v2, v3 = sel[4 * g:4 * g + 4]
        hi1, lo1 = jnp.maximum(v0, v1), jnp.minimum(v0, v1)
        hi2, lo2 = jnp.maximum(v2, v3), jnp.minimum(v2, v3)
        gs.append(jnp.maximum(hi1, hi2) + jnp.maximum(jnp.minimum(hi1, hi2), jnp.maximum(lo1, lo2)))
    best = gs[0]
    bg = jnp.zeros(best.shape, I32)
    for g in range(1, N_GROUPS):
        upd = gs[g] > best
        bg = jnp.where(upd, g, bg)
        best = jnp.where(upd, gs[g], best)
    v = []
    for i in range(EXPERTS_PER_GROUP):
        vi = sel[i]
        for g in range(1, N_GROUPS):
            vi = jnp.where(bg == g, sel[4 * g + i], vi)
        v.append(vi)
    picked = []
    for i in range(EXPERTS_PER_GROUP):
        rank = jnp.zeros(best.shape, I32)
        for j in range(EXPERTS_PER_GROUP):
            if j == i:
                continue
            ahead = (v[j] > v[i]) | ((v[j] == v[i]) & (j < i))
            rank = rank + ahead.astype(I32)
        picked.append(rank < 2)
    code = jnp.full(best.shape, len(PAIRS) - 1, I32)
    for idx in range(len(PAIRS) - 2, -1, -1):
        a, b = PAIRS[idx]
        code = jnp.where(picked[a] & picked[b], idx, code)
    return bg * len(PAIRS) + code


def _outproj_kernel(a_ref, b_ref, lng_ref, lnb_ref, w_ref, bias_ref, x_ref, mod_ref, g_ref, rw1_ref, rw2_ref,
                    rb_ref, x1_ref, h2_ref, cls_ref, *, ln_swish):
    ka = a_ref.shape[1]
    tm = a_ref.shape[0]
    if ln_swish:
        t = jnp.concatenate([a_ref[...], b_ref[...]], axis=-1).astype(F32)
        mu = jnp.mean(t, axis=-1, keepdims=True)
        tc = t - mu
        var = jnp.mean(tc * tc, axis=-1, keepdims=True)
        z = tc * lax.rsqrt(var + EPS) * lng_ref[...] + lnb_ref[...]
        z = (z * _sigmoid(z)).astype(BF16)
        y = jnp.dot(z, w_ref[...], preferred_element_type=F32) + bias_ref[...]
    else:
        y = jnp.dot(a_ref[...], w_ref[0:ka, :], preferred_element_type=F32)
        y = y + jnp.dot(b_ref[...], w_ref[ka:, :], preferred_element_type=F32) + bias_ref[...]
    x1 = x_ref[...] + mod_ref[0, 2:3, :] * y
    x1_ref[...] = x1
    h2 = _rms_mod(x1, g_ref[...], mod_ref[0, 4:5, :], mod_ref[0, 3:4, :])
    _store_packed_rows(h2_ref, h2, tm)
    hi = h2.astype(BF16)
    lo = (h2 - hi.astype(F32)).astype(BF16)
    s = jnp.dot(hi, rw1_ref[...], preferred_element_type=F32) + jnp.dot(lo, rw2_ref[...], preferred_element_type=F32)
    st = s.T
    logits_t = st[0:N_EXPERTS, :] + st[N_EXPERTS:2 * N_EXPERTS, :]
    cls_ref[...] = _route(logits_t, rb_ref[...])


def outproj_residual_route(a, a_col, b, b_col, ln_g, ln_b, w, bias, x2d, mod, ffn_g, router_w, router_b,
                           rows_per_mod, tm, ln_swish):
    m, d = x2d.shape
    ka = kb = w.shape[0] // 2
    kin = ka + kb
    n_slab = d // (2 * LANES)
    rw_hi = router_w.astype(BF16)
    rw_lo = (router_w - rw_hi.astype(F32)).astype(BF16)
    pad = jnp.zeros((d, LANES - 2 * N_EXPERTS), BF16)
    rw1 = jnp.concatenate([rw_hi, rw_lo, pad], axis=1)
    rw2 = jnp.concatenate([rw_hi, jnp.zeros((d, N_EXPERTS), BF16), pad], axis=1)
    kern = functools.partial(_outproj_kernel, ln_swish=ln_swish)
    return pl.pallas_call(
        kern,
        out_shape=(
            jax.ShapeDtypeStruct((m, d), F32),
            jax.ShapeDtypeStruct((m * n_slab, LANES), I32),
            jax.ShapeDtypeStruct((1, m), I32),
        ),
        grid=(m // tm,),
        in_specs=[
            pl.BlockSpec((tm, ka), lambda i: (i, a_col)),
            pl.BlockSpec((tm, kb), lambda i: (i, b_col)),
            pl.BlockSpec((1, kin), lambda i: (0, 0)),
            pl.BlockSpec((1, kin), lambda i: (0, 0)),
            pl.BlockSpec((kin, d), lambda i: (0, 0), pipeline_mode=pl.Buffered(1)),
            pl.BlockSpec((1, d), lambda i: (0, 0)),
            pl.BlockSpec((tm, d), lambda i: (i, 0)),
            pl.BlockSpec((1, N_MOD, d), lambda i: ((i * tm) // rows_per_mod, 0, 0)),
            pl.BlockSpec((1, d), lambda i: (0, 0)),
            pl.BlockSpec((d, LANES), lambda i: (0, 0)),
            pl.BlockSpec((d, LANES), lambda i: (0, 0)),
            pl.BlockSpec((N_EXPERTS, 1), lambda i: (0, 0)),
        ],
        out_specs=(
            pl.BlockSpec((tm, d), lambda i: (i, 0)),
            pl.BlockSpec((tm * n_slab, LANES), lambda i: (i, 0)),
            pl.BlockSpec((1, tm), lambda i: (0, i)),
        ),
        compiler_params=_cparams(("parallel",), 56),
        name="outproj_residual_route",
    )(a, b, ln_g.reshape(1, kin), ln_b.reshape(1, kin), w, bias.reshape(1, d), x2d, mod, ffn_g.reshape(1, d),
      rw1, rw2, router_b.astype(F32).reshape(N_EXPERTS, 1))


def _class_onehot(cls_ref):
    blk = cls_ref.shape[1]
    return (lax.broadcasted_iota(I32, (CLASS_PAD, blk), 0) == cls_ref[...]).astype(F32)


def _count_kernel(cls_ref, cnt_ref):
    @pl.when(pl.program_id(0) == 0)
    def _():
        cnt_ref[...] = jnp.zeros_like(cnt_ref)

    cnt_ref[...] += jnp.sum(_class_onehot(cls_ref), axis=1, keepdims=True)


def _dest_kernel(cls_ref, start_ref, dest_ref, carry_scr):
    @pl.when(pl.program_id(0) == 0)
    def _():
        carry_scr[...] = jnp.zeros_like(carry_scr)

    blk = cls_ref.shape[1]
    onehot = _class_onehot(cls_ref)
    tri = (lax.broadcasted_iota(I32, (blk, blk), 0) <= lax.broadcasted_iota(I32, (blk, blk), 1)).astype(BF16)
    cum = jnp.dot(onehot.astype(BF16), tri, preferred_element_type=F32)
    pos = cum - 1.0 + carry_scr[...] + start_ref[...]
    dest_ref[...] = jnp.sum(onehot * pos, axis=0, keepdims=True).astype(I32)
    carry_scr[...] += jnp.sum(onehot, axis=1, keepdims=True)


def sort_tokens(cls, tile):
    n = cls.shape[1]
    nb = n // SORT_BLK
    counts = pl.pallas_call(
        _count_kernel,
        out_shape=jax.ShapeDtypeStruct((CLASS_PAD, 1), F32),
        grid=(nb,),
        in_specs=[pl.BlockSpec((1, SORT_BLK), lambda j: (0, j))],
        out_specs=pl.BlockSpec((CLASS_PAD, 1), lambda j: (0, 0)),
        compiler_params=_cparams(("arbitrary",), 32),
        name="class_counts",
    )(cls)
    counts = counts.reshape(CLASS_PAD).astype(I32)
    padded = ((counts + tile - 1) // tile) * tile
    starts = (jnp.cumsum(padded) - padded).astype(F32).reshape(CLASS_PAD, 1)
    dest = pl.pallas_call(
        _dest_kernel,
        out_shape=jax.ShapeDtypeStruct((1, n), I32),
        grid=(nb,),
        in_specs=[pl.BlockSpec((1, SORT_BLK), lambda j: (0, j)),
                  pl.BlockSpec((CLASS_PAD, 1), lambda j: (0, 0))],
        out_specs=pl.BlockSpec((1, SORT_BLK), lambda j: (0, j)),
        scratch_shapes=[pltpu.VMEM((CLASS_PAD, 1), F32)],
        compiler_params=_cparams(("arbitrary",), 32),
        name="sorted_positions",
    )(cls, starts)
    return dest, counts


def _inverse_kernel(dest_ref, src_ref, *, n_tok, n_rows):
    def clear(k, carry):
        for u in range(SCALAR_UNROLL):
            src_ref[k * SCALAR_UNROLL + u] = 0
        return carry

    lax.fori_loop(0, n_rows // SCALAR_UNROLL, clear, 0)

    def scatter(k, carry):
        for u in range(SCALAR_UNROLL):
            t = k * SCALAR_UNROLL + u
            src_ref[dest_ref[t]] = t
        return carry

    lax.fori_loop(0, n_tok // SCALAR_UNROLL, scatter, 0)


def inverse_map(dest, n_rows):
    n_tok = dest.shape[0]
    kern = functools.partial(_inverse_kernel, n_tok=n_tok, n_rows=n_rows)
    return pl.pallas_call(
        kern,
        out_shape=jax.ShapeDtypeStruct((n_rows,), I32),
        in_specs=[pl.BlockSpec(memory_space=pltpu.SMEM)],
        out_specs=pl.BlockSpec(memory_space=pltpu.SMEM),
        name="inverse_map",
    )(dest)


def _start_row_gather(idx_ref, base, src_hbm, buf, sem, slot, tm, s):
    def body(k, carry):
        for u in range(SCALAR_UNROLL):
            r = k * SCALAR_UNROLL + u
            pltpu.make_async_copy(
                src_hbm.at[pl.ds(pl.multiple_of(idx_ref[base + r] * s, s), s), :],
                buf.at[slot, pl.ds(pl.multiple_of(r * s, s), s), :],
                sem.at[slot]).start(priority=u % 2)
        return carry

    lax.fori_loop(0, tm // SCALAR_UNROLL, body, 0)


def _wait_row_gather(src_hbm, buf, sem, slot, tm, s):
    pltpu.make_async_copy(src_hbm.at[pl.ds(0, tm * s), :], buf.at[slot], sem.at[slot]).wait()


HI_HALF = -65536


def _store_packed_rows(ref, v, tm):
    half = v.shape[1] // 2
    n_slab = half // LANES
    for c in range(n_slab):
        a = v[:, c * LANES:(c + 1) * LANES].astype(BF16).astype(F32)
        b = v[:, half + c * LANES:half + (c + 1) * LANES].astype(BF16).astype(F32)
        word = lax.bitcast_convert_type(a, I32) | lax.shift_right_logical(lax.bitcast_convert_type(b, I32), 16)
        ref[pl.ds(c, tm, stride=n_slab), :] = word


def _load_packed_rows(ref, tm, n_slab):
    words = [ref[pl.ds(c, tm, stride=n_slab), :] for c in range(n_slab)]
    hi = [lax.bitcast_convert_type(w & HI_HALF, F32) for w in words]
    lo = [lax.bitcast_convert_type(lax.shift_left(w, 16), F32) for w in words]
    return jnp.concatenate(hi + lo, axis=-1)


def _moe_kernel(src_ref, e1_ref, e2_ref, valid_ref, h_hbm, wg1_ref, wg2_ref, wu1_ref, wu2_ref,
                wd1_ref, wd2_ref, rwt_ref, ys_ref, buf, sem, *, tm, n_slab, n_tiles):
    i = pl.program_id(0)
    slot = i % 2

    @pl.when(i == 0)
    def _():
        _start_row_gather(src_ref, 0, h_hbm, buf, sem, 0, tm, n_slab)

    nxt = jnp.minimum(i + 1, n_tiles - 1)

    @pl.when((i + 1 < n_tiles) & (valid_ref[nxt] == 1))
    def _():
        _start_row_gather(src_ref, nxt * tm, h_hbm, buf, sem, 1 - slot, tm, n_slab)

    @pl.when(valid_ref[i] == 0)
    def _():
        ys_ref[...] = jnp.zeros_like(ys_ref)

    @pl.when(valid_ref[i] == 1)
    def _():
        _wait_row_gather(h_hbm, buf, sem, slot, tm, n_slab)
        xr = _load_packed_rows(buf.at[slot], tm, n_slab)
        x = xr.astype(BF16)
        l1 = jnp.sum(xr * rwt_ref[pl.ds(e1_ref[i], 1), :], axis=-1, keepdims=True)
        l2 = jnp.sum(xr * rwt_ref[pl.ds(e2_ref[i], 1), :], axis=-1, keepdims=True)
        s1 = _sigmoid(l1)
        s2 = _sigmoid(l2)
        tot = s1 + s2

        def expert(wg_ref, wu_ref, weight):
            gt = jnp.dot(xr, wg_ref[0], preferred_element_type=F32)
            up = jnp.dot(x, wu_ref[0], preferred_element_type=F32)
            return gt * _sigmoid(gt) * up * weight

        a1 = expert(wg1_ref, wu1_ref, s1 / tot)
        a2 = expert(wg2_ref, wu2_ref, s2 / tot)
        y = jnp.dot(a1, wd1_ref[0], preferred_element_type=F32)
        y = y + jnp.dot(a2, wd2_ref[0], preferred_element_type=F32)
        _store_packed_rows(ys_ref, y, tm)


def moe_ffn(src, tile_e1, tile_e2, tile_valid, h_rows, w_gate, w_up, w_down, layer, router_w_t, d, tm):
    n_tiles = tile_e1.shape[0]
    n_slab = d // (2 * LANES)
    fe = w_gate.shape[3]
    kern = functools.partial(_moe_kernel, tm=tm, n_slab=n_slab, n_tiles=n_tiles)

    def w_in_spec(which):
        return pl.BlockSpec((None, 1, d, fe), lambda i, src, e1, e2, valid: (layer, (e1, e2)[which][i], 0, 0))

    def w_up_spec(which):
        return pl.BlockSpec((1, d, fe), lambda i, src, e1, e2, valid: ((e1, e2)[which][i], 0, 0))

    def w_out_spec(which):
        return pl.BlockSpec((None, 1, fe, d), lambda i, src, e1, e2, valid: (layer, (e1, e2)[which][i], 0, 0))

    return pl.pallas_call(
        kern,
        out_shape=jax.ShapeDtypeStruct((n_tiles * tm * n_slab, LANES), I32),
        grid_spec=pltpu.PrefetchScalarGridSpec(
            num_scalar_prefetch=4,
            grid=(n_tiles,),
            in_specs=[
                pl.BlockSpec(memory_space=pl.ANY),
                w_in_spec(0), w_in_spec(1), w_up_spec(0), w_up_spec(1), w_out_spec(0), w_out_spec(1),
                pl.BlockSpec((N_EXPERTS, d), lambda i, src, e1, e2, valid: (0, 0)),
            ],
            out_specs=pl.BlockSpec((tm * n_slab, LANES), lambda i, src, e1, e2, valid: (i, 0)),
            scratch_shapes=[pltpu.VMEM((2, tm * n_slab, LANES), I32), pltpu.SemaphoreType.DMA((2,))],
        ),
        compiler_params=_cparams(("arbitrary",), 58),
        name="moe_ffn",
    )(src, tile_e1, tile_e2, tile_valid, h_rows, w_gate, w_gate, w_up, w_up, w_down, w_down, router_w_t)


def _combine_kernel(dest_ref, ys_hbm, x1_ref, mod_ref, g_ref, modn_ref, *rest, tm, s, n_tiles, final):
    if final:
        o_ref, buf, sem = rest
    else:
        o_ref, hn_ref, buf, sem = rest
    i = pl.program_id(0)
    slot = i % 2

    @pl.when(i == 0)
    def _():
        _start_row_gather(dest_ref, 0, ys_hbm, buf, sem, 0, tm, s)

    @pl.when(i + 1 < n_tiles)
    def _():
        _start_row_gather(dest_ref, (i + 1) * tm, ys_hbm, buf, sem, 1 - slot, tm, s)

    _wait_row_gather(ys_hbm, buf, sem, slot, tm, s)
    y = _load_packed_rows(buf.at[slot], tm, s)
    x2 = x1_ref[...] + mod_ref[0, 5:6, :] * y
    if final:
        ms = jnp.mean(x2 * x2, axis=-1, keepdims=True)
        o_ref[...] = x2 * lax.rsqrt(ms + EPS) * g_ref[...]
    else:
        o_ref[...] = x2
        hn_ref[...] = _rms_mod(x2, g_ref[...], modn_ref[0, 1:2, :], modn_ref[0, 0:1, :]).astype(hn_ref.dtype)


def combine_residual(dest, ys, x1, mod, g, mod_next, rows_per_mod, tm, final):
    m, d = x1.shape
    s = d // (2 * LANES)
    n_tiles = m // tm
    kern = functools.partial(_combine_kernel, tm=tm, s=s, n_tiles=n_tiles, final=final)
    row_spec = pl.BlockSpec((tm, d), lambda i, dest: (i, 0))
    mod_spec = pl.BlockSpec((1, N_MOD, d), lambda i, dest: ((i * tm) // rows_per_mod, 0, 0))
    if final:
        out_shape, out_specs = jax.ShapeDtypeStruct((m, d), F32), row_spec
    else:
        out_shape = (jax.ShapeDtypeStruct((m, d), F32), jax.ShapeDtypeStruct((m, d), BF16))
        out_specs = (row_spec, row_spec)
    return pl.pallas_call(
        kern,
        out_shape=out_shape,
        grid_spec=pltpu.PrefetchScalarGridSpec(
            num_scalar_prefetch=1,
            grid=(n_tiles,),
            in_specs=[
                pl.BlockSpec(memory_space=pl.ANY),
                row_spec,
                mod_spec,
                pl.BlockSpec((1, d), lambda i, dest: (0, 0)),
                mod_spec,
            ],
            out_specs=out_specs,
            scratch_shapes=[pltpu.VMEM((2, tm * s, LANES), I32), pltpu.SemaphoreType.DMA((2,))],
        ),
        compiler_params=_cparams(("arbitrary",), 40),
        name="combine_residual",
    )(dest, ys, x1, mod, g.reshape(1, d), mod_next)


def _tile_tables(counts, n_tiles, tm):
    tiles_per_class = (counts[:N_CLASSES] + tm - 1) // tm
    tile_end = jnp.cumsum(tiles_per_class)
    total = tile_end[-1]
    idx = jnp.arange(n_tiles, dtype=I32)
    valid = idx < total
    blk = jnp.where(valid, idx, total - 1)
    cls = jnp.minimum(jnp.sum((tile_end[None, :] <= blk[:, None]).astype(I32), axis=1), N_CLASSES - 1)
    pair = jnp.asarray(PAIRS, I32)
    grp = cls // len(PAIRS)
    e1 = grp * EXPERTS_PER_GROUP + pair[cls % len(PAIRS), 0]
    e2 = grp * EXPERTS_PER_GROUP + pair[cls % len(PAIRS), 1]
    return e1.astype(I32), e2.astype(I32), valid.astype(I32)


def grouped_moe_residual(x1, h2_rows, cls, mod, router_w, w_gate, w_up, w_down, layer, g, mod_next,
                         rows_per_mod, final):
    m, d = x1.shape
    n_tiles = m // MOE_TM + N_CLASSES
    dest, counts = sort_tokens(cls, MOE_TM)
    dest = dest.reshape(m)
    e1, e2, valid = _tile_tables(counts, n_tiles, MOE_TM)
    src = inverse_map(dest, n_tiles * MOE_TM)
    ys = moe_ffn(src, e1, e2, valid, h2_rows, w_gate, cast_experts_bf16(w_up, layer), w_down, layer,
                 router_w.T.astype(F32), d, MOE_TM)
    return combine_residual(dest, ys, x1, mod, g, mod_next, rows_per_mod, GATHER_TM, final)


def _forward(dims, x, c, ctx, c_ctx, ada_w, ada_b, mix_norm_g, ffn_norm_g, ev_w_in, ev_rpb, ev_w_out,
             od_w_in, od_b_in, od_dw_w, od_dw_b, od_ln_g, od_ln_b, od_w_out, od_b_out,
             router_w, router_b, moe_w_gate, moe_w_up, moe_w_down, final_norm_g, *, tiles):
    b, seq, d = dims.batch, dims.seq, dims.d
    n = b * seq
    depth = ada_w.shape[0]
    assert depth == 2, "layer 0 is the Fourier/attention mixer, layer 1 the Conformer mixer"
    x2d = x.reshape(n, d)
    ctx2d = ctx.reshape(b * dims.ctx, d)

    cvec = jnp.zeros((16, d), F32).at[:b].set(c).at[b].set(c_ctx)
    mod_all = adaln_mod(cvec, ada_w, ada_b, tiles["mod_tn"]).reshape(depth, 16, N_MOD, d)

    mod0 = mod_all[0, :b]
    modc0 = mod_all[0, b:b + 1]
    w_in = ev_w_in[0].astype(BF16)
    p = norm_mod_matmul(x2d, mix_norm_g[0], mod0, w_in, seq, 0, 1, tiles["in_tm"], w_in.shape[1], 0)
    kvc = norm_mod_matmul(ctx2d, mix_norm_g[0], modc0, w_in, b * dims.ctx, 0, 1,
                          min(tiles["in_tm"], b * dims.ctx), 2 * dims.naw, 1)
    fo = fourier_mix(p, dims, tiles["four_tm"])
    ao = neighbourhood_attention(p, kvc, ev_rpb[0], dims)
    mod1 = mod_all[1, :b]
    mix_w = ev_w_out.shape[1]
    x1, h2_rows, cls = outproj_residual_route(
        fo, 0, ao, 0, jnp.ones((mix_w,), F32), jnp.zeros((mix_w,), F32), ev_w_out[0].astype(BF16),
        jnp.zeros((d,), F32), x2d, mod0, ffn_norm_g[0], router_w, router_b, seq, tiles["out_tm"], False)
    x2, hmix1 = grouped_moe_residual(x1, h2_rows, cls, mod0, router_w, moe_w_gate, moe_w_up, moe_w_down, 0,
                                     mix_norm_g[1], mod1, seq, False)

    yc = glu_conv(hmix1, od_w_in[0].astype(BF16), od_b_in[0], od_dw_w[0], od_dw_b[0], dims, tiles["glu_tn"])
    x3, h4_rows, cls1 = outproj_residual_route(
        yc, 0, yc, 1, od_ln_g[0], od_ln_b[0], od_w_out[0].astype(BF16), od_b_out[0], x2, mod1, ffn_norm_g[1],
        router_w, router_b, seq, tiles["out_tm"], True)
    out = grouped_moe_residual(x3, h4_rows, cls1, mod1, router_w, moe_w_gate, moe_w_up, moe_w_down, 1,
                               final_norm_g, mod1, seq, True)
    return out.reshape(b, seq, d)


TILES = dict(mod_tn=1024, in_tm=256, glu_tn=256, four_tm=512, out_tm=256)


def kernel(x, c, ctx, c_ctx, ada_w, ada_b, mix_norm_g, ffn_norm_g, ev_w_in, ev_rpb, ev_w_out, od_w_in, od_b_in,
           od_dw_w, od_dw_b, od_ln_g, od_ln_b, od_w_out, od_b_out, router_w, router_b, moe_w_gate, moe_w_up,
           moe_w_down, final_norm_g):
    b, seq, d = x.shape
    dims = Dims(batch=b, seq=seq, d=d, ctx=ctx.shape[1], fw=ev_w_out.shape[1] - NA_HEADS * NA_HEAD_DIM,
                naw=NA_HEADS * NA_HEAD_DIM, fe=moe_w_gate.shape[3])
    return _forward(dims, x, c, ctx, c_ctx, ada_w, ada_b, mix_norm_g, ffn_norm_g, ev_w_in, ev_rpb, ev_w_out,
                    od_w_in, od_b_in, od_dw_w, od_dw_b, od_ln_g, od_ln_b, od_w_out, od_b_out,
                    router_w, router_b, moe_w_gate, moe_w_up, moe_w_down, final_norm_g, tiles=TILES)
```

```python
import functools
import math
from typing import NamedTuple

import jax
import jax.numpy as jnp
import numpy as np
from jax import lax
from jax.experimental import pallas as pl
from jax.experimental.pallas import tpu as pltpu

F32 = jnp.float32
BF16 = jnp.bfloat16
I32 = jnp.int32

LANES = 128
NEG_BIG = -1e30
EPS = 1e-6
LOG2E = math.log2(math.e)

N_MOD = 6
N_FOURIER_GROUPS = 4
NA_HEADS = 16
NA_HEAD_DIM = 64
NA_ROWS = 8
NA_COLS = 16
GRID_W = 64
CONV_K = 31
N_EXPERTS = 16
N_GROUPS = 4
EXPERTS_PER_GROUP = 4
PAIRS = ((0, 1), (0, 2), (0, 3), (1, 3), (1, 2), (2, 3))
N_CLASSES = N_GROUPS * len(PAIRS)
CLASS_PAD = 32
MOE_TM = 256
SORT_BLK = 512
GATHER_TM = 512
SCALAR_UNROLL = 8


class Dims(NamedTuple):
    batch: int
    seq: int
    d: int
    ctx: int
    fw: int
    naw: int
    fe: int


def _cparams(sem, vmem_mb):
    return pltpu.CompilerParams(dimension_semantics=sem, vmem_limit_bytes=vmem_mb << 20)


def _sigmoid(x):
    return 1.0 / (1.0 + jnp.exp(-x))


def _rms_mod(x, g, scale, shift):
    ms = jnp.mean(x * x, axis=-1, keepdims=True)
    return (x * lax.rsqrt(ms + EPS) * g) * (1.0 + scale) + shift


def _mod_kernel(c_ref, w_ref, b_ref, o_ref):
    c = c_ref[...]
    s = (c * _sigmoid(c)).astype(BF16)
    o_ref[0] = jnp.dot(s, w_ref[0].astype(BF16), preferred_element_type=F32) + b_ref[0]


def adaln_mod(cvec, ada_w, ada_b, tn):
    depth, d, n = ada_w.shape
    rows = cvec.shape[0]
    return pl.pallas_call(
        _mod_kernel,
        out_shape=jax.ShapeDtypeStruct((depth, rows, n), F32),
        grid=(depth, n // tn),
        in_specs=[
            pl.BlockSpec((rows, d), lambda l, j: (0, 0)),
            pl.BlockSpec((1, d, tn), lambda l, j: (l, 0, j)),
            pl.BlockSpec((1, 1, tn), lambda l, j: (l, 0, j)),
        ],
        out_specs=pl.BlockSpec((1, rows, tn), lambda l, j: (l, 0, j)),
        compiler_params=_cparams(("parallel", "parallel"), 40),
        name="adaln_mod",
    )(cvec, ada_w, ada_b.reshape(depth, 1, n))


def _inproj_kernel(x0_ref, xn_ref, g_ref, mod0_ref, modn_ref, w_ref, o_ref, h_scr, *, shift_idx, scale_idx):
    i = pl.program_id(0)

    def norm_into(x_ref, mod_ref, slot):
        h = _rms_mod(x_ref[...], g_ref[...], mod_ref[0, scale_idx:scale_idx + 1, :],
                     mod_ref[0, shift_idx:shift_idx + 1, :])
        h_scr[slot] = h.astype(BF16)

    @pl.when(i == 0)
    def _():
        norm_into(x0_ref, mod0_ref, 0)

    def step(cur, nxt):
        o_ref[...] = jnp.dot(h_scr[cur], w_ref[...], preferred_element_type=F32).astype(o_ref.dtype)
        norm_into(xn_ref, modn_ref, nxt)

    @pl.when(i % 2 == 0)
    def _():
        step(0, 1)

    @pl.when(i % 2 == 1)
    def _():
        step(1, 0)


def norm_mod_matmul(x2d, g, mod, w, rows_per_mod, shift_idx, scale_idx, tm, n, col_blk):
    m, d = x2d.shape
    last = m // tm - 1
    kern = functools.partial(_inproj_kernel, shift_idx=shift_idx, scale_idx=scale_idx)

    def nxt(i):
        return jnp.minimum(i + 1, last)

    return pl.pallas_call(
        kern,
        out_shape=jax.ShapeDtypeStruct((m, n), BF16),
        grid=(m // tm,),
        in_specs=[
            pl.BlockSpec((tm, d), lambda i: (0, 0)),
            pl.BlockSpec((tm, d), lambda i: (nxt(i), 0)),
            pl.BlockSpec((1, d), lambda i: (0, 0)),
            pl.BlockSpec((1, N_MOD, d), lambda i: (0, 0, 0)),
            pl.BlockSpec((1, N_MOD, d), lambda i: ((nxt(i) * tm) // rows_per_mod, 0, 0)),
            pl.BlockSpec((d, n), lambda i: (0, col_blk), pipeline_mode=pl.Buffered(1)),
        ],
        out_specs=pl.BlockSpec((tm, n), lambda i: (i, 0)),
        scratch_shapes=[pltpu.VMEM((2, tm, d), BF16)],
        compiler_params=_cparams(("arbitrary",), 48),
        name="norm_mod_matmul",
    )(x2d, x2d, g.reshape(1, d), mod, mod, w)


def _fourier_kernel(u_ref, cs_ref, cl_ref, sl_ref, o_ref, xc_scr, xs_scr, *, gc):
    @pl.when(pl.program_id(1) == 0)
    def _():
        for grp in range(N_FOURIER_GROUPS):
            r = jnp.dot(u_ref[:, grp * gc:(grp + 1) * gc], cs_ref[...], preferred_element_type=F32)
            xc_scr[:, grp * gc:(grp + 1) * gc] = r[:, :gc].astype(BF16)
            xs_scr[:, grp * gc:(grp + 1) * gc] = r[:, gc:].astype(BF16)

    acc = jnp.dot(cl_ref[...], xc_scr[...], preferred_element_type=F32)
    acc = acc - jnp.dot(sl_ref[...], xs_scr[...], preferred_element_type=F32)
    o_ref[...] = acc.astype(o_ref.dtype)


def _dft_tables(length, scale):
    kn = np.outer(np.arange(length), np.arange(length)) % length
    ang = kn.astype(np.float64) * (2.0 * math.pi / length)
    return (np.cos(ang) * scale).astype(np.float32), (np.sin(ang) * scale).astype(np.float32)


def fourier_mix(p, dims, tm):
    gc = dims.fw // N_FOURIER_GROUPS
    cc, sc = _dft_tables(gc, 1.0 / math.sqrt(dims.seq * gc))
    cs = jnp.asarray(np.concatenate([cc, sc], axis=1).astype(BF16))
    cl, sl = (jnp.asarray(t.astype(BF16)) for t in _dft_tables(dims.seq, 1.0))
    kern = functools.partial(_fourier_kernel, gc=gc)
    return pl.pallas_call(
        kern,
        out_shape=jax.ShapeDtypeStruct((dims.batch * dims.seq, dims.fw), BF16),
        grid=(dims.batch, dims.seq // tm),
        in_specs=[
            pl.BlockSpec((dims.seq, dims.fw), lambda b, m: (b, 0)),
            pl.BlockSpec((gc, 2 * gc), lambda b, m: (0, 0)),
            pl.BlockSpec((tm, dims.seq), lambda b, m: (m, 0)),
            pl.BlockSpec((tm, dims.seq), lambda b, m: (m, 0)),
        ],
        out_specs=pl.BlockSpec((tm, dims.fw), lambda b, m: (b * (dims.seq // tm) + m, 0)),
        scratch_shapes=[pltpu.VMEM((dims.seq, dims.fw), BF16), pltpu.VMEM((dims.seq, dims.fw), BF16)],
        compiler_params=_cparams(("parallel", "arbitrary"), 48),
        name="fourier_mix",
    )(p, cs, cl, sl)


NA_QROWS = 2


def _na_geometry(n_rows):
    kr = min(NA_ROWS, n_rows)
    n_union = kr + NA_QROWS - 1
    steps = []
    for rp in range(n_rows // NA_QROWS):
        rows = [rp * NA_QROWS + a for a in range(NA_QROWS)]
        starts = [min(max(r - kr // 2, 0), n_rows - kr) for r in rows]
        u = min(starts[0], n_rows - n_union)
        steps.append((u, tuple((r - u, rs - u) for r, rs in zip(rows, starts))))
    variants = sorted(set(v for _, v in steps))
    return kr, n_union, [u for u, _ in steps], [variants.index(v) for _, v in steps], variants


def _na_kernel(dr_ref, ustart_ref, q_ref, k_ref, v_ref, kc_ref, vc_ref, t_ref, o_ref, s_scr, *, n_union):
    rp = pl.program_id(1)
    row0 = pl.multiple_of(ustart_ref[rp] * GRID_W, GRID_W)
    nloc = n_union * GRID_W
    nctx = kc_ref.shape[0]
    low = lax.broadcasted_iota(I32, (GRID_W, LANES), 1) < NA_HEAD_DIM
    nt = (((1,), (1,)), ((), ()))
    tn = (((0,), (0,)), ((), ()))
    scale = NA_HEAD_DIM ** -0.5
    n_pairs = NA_HEADS // 2

    def scores(pr):
        cols = pl.ds(pr * LANES, LANES)
        q2 = (q_ref[:, cols].astype(F32) * (scale * LOG2E)).astype(BF16)
        parts = []
        for a in range(NA_QROWS):
            qa = q2[a * GRID_W:(a + 1) * GRID_W]
            zero = jnp.zeros_like(qa)
            parts += [jnp.where(low, qa, zero), jnp.where(low, zero, qa)]
        qm = jnp.concatenate(parts, axis=0)
        kp = k_ref[pl.ds(row0, nloc), cols]
        loc = lax.dot_general(kp, qm, nt, preferred_element_type=F32)
        for jj in range(n_union):
            for a in range(NA_QROWS):
                bias = t_ref[pr, dr_ref[(rp * NA_QROWS + a) * n_union + jj]].astype(F32)
                rows = slice(jj * GRID_W, (jj + 1) * GRID_W)
                lanes = slice(a * 2 * GRID_W, (a + 1) * 2 * GRID_W)
                s_scr[pr % 2, rows, lanes] = loc[rows, lanes] + bias
        s_scr[pr % 2, nloc:nloc + nctx, :] = lax.dot_general(kc_ref[:, cols], qm, nt, preferred_element_type=F32)

    def attend(pr):
        cols = pl.ds(pr * LANES, LANES)
        s = s_scr[pr % 2]
        m = jnp.max(s, axis=0, keepdims=True)
        e = jnp.exp2(s - m)
        den = jnp.sum(e, axis=0, keepdims=True)
        eb = e.astype(BF16)
        vp = v_ref[pl.ds(row0, nloc), cols]
        ot = lax.dot_general(vp, eb[0:nloc], tn, preferred_element_type=F32)
        ot = ot + lax.dot_general(vc_ref[:, cols], eb[nloc:], tn, preferred_element_type=F32)
        o = (ot / den).T
        for a in range(NA_QROWS):
            base = a * 2 * GRID_W
            o_ref[a * GRID_W:(a + 1) * GRID_W, cols] = jnp.where(
                low, o[base:base + GRID_W], o[base + GRID_W:base + 2 * GRID_W]).astype(o_ref.dtype)

    always = pl.program_id(0) >= 0
    scores(0)
    for pr in range(n_pairs):
        @pl.when(always)
        def _(pr=pr):
            if pr + 1 < n_pairs:
                scores(pr + 1)
            attend(pr)


N_DR = 2 * NA_ROWS - 1


def _colbias_kernel(rpb_ref, onehot_ref, neg_ref, o_ref):
    o_ref[...] = (jnp.dot(rpb_ref[...], onehot_ref[...], preferred_element_type=F32) + neg_ref[...]).astype(o_ref.dtype)


def _na_bias_blocks(rpb):
    col = np.arange(GRID_W)
    col_start = np.clip(col - NA_COLS // 2, 0, GRID_W - NA_COLS)
    col_mask = (col[None, :] >= col_start[:, None]) & (col[None, :] < col_start[:, None] + NA_COLS)
    dc_idx = np.clip(col[None, :] - col[:, None] + NA_COLS - 1, 0, 2 * NA_COLS - 2)
    n_dc = 2 * NA_COLS - 1
    onehot = np.zeros((LANES, GRID_W, GRID_W), np.float32)
    kk, qq = np.meshgrid(col, col, indexing="ij")
    onehot[dc_idx[qq, kk], kk, qq] = col_mask[qq, kk]
    neg = np.where(col_mask.T, 0.0, NEG_BIG).astype(np.float32).reshape(1, GRID_W * GRID_W)
    rows = NA_HEADS * N_DR
    rows_pad = -(-rows // LANES) * LANES
    rpb2d = jnp.zeros((rows_pad, LANES), F32).at[:rows, :n_dc].set(rpb.astype(F32).reshape(rows, n_dc) * LOG2E)
    colbias = pl.pallas_call(
        _colbias_kernel,
        out_shape=jax.ShapeDtypeStruct((rows_pad, GRID_W * GRID_W), BF16),
        name="na_colbias",
    )(rpb2d, jnp.asarray(onehot.reshape(LANES, GRID_W * GRID_W), BF16), jnp.asarray(neg))
    colbias = colbias[:rows].reshape(NA_HEADS, N_DR, GRID_W, GRID_W)
    masked = jnp.full((NA_HEADS, 1, GRID_W, GRID_W), NEG_BIG, BF16)
    t = jnp.concatenate([colbias, masked], axis=1).reshape(NA_HEADS // 2, 2, N_DR + 1, GRID_W, GRID_W)
    return t.transpose(0, 2, 3, 1, 4).reshape(NA_HEADS // 2, N_DR + 1, GRID_W, 2 * GRID_W)


def neighbourhood_attention(p, kvc, rpb, dims):
    n_rows = dims.seq // GRID_W
    kr, n_union, ustarts, var_of_step, variants = _na_geometry(n_rows)
    n_steps = n_rows // NA_QROWS
    dr_idx = [[(jj - r_off + NA_ROWS - 1) if rs_off <= jj < rs_off + kr else N_DR
               for jj in range(n_union)] for var in (variants[v] for v in var_of_step) for r_off, rs_off in var]
    blocks = _na_bias_blocks(rpb)
    naw = dims.naw
    qb = dims.fw // naw
    tq = NA_QROWS * GRID_W
    kern = functools.partial(_na_kernel, n_union=n_union)
    return pl.pallas_call(
        kern,
        out_shape=jax.ShapeDtypeStruct((dims.batch * dims.seq, naw), BF16),
        grid_spec=pltpu.PrefetchScalarGridSpec(
            num_scalar_prefetch=2,
            grid=(dims.batch, n_steps),
            in_specs=[
                pl.BlockSpec((tq, naw), lambda b, r, dr, us: (b * n_steps + r, qb)),
                pl.BlockSpec((dims.seq, naw), lambda b, r, dr, us: (b, qb + 1)),
                pl.BlockSpec((dims.seq, naw), lambda b, r, dr, us: (b, qb + 2)),
                pl.BlockSpec((dims.ctx, naw), lambda b, r, dr, us: (b, 0)),
                pl.BlockSpec((dims.ctx, naw), lambda b, r, dr, us: (b, 1)),
                pl.BlockSpec(blocks.shape, lambda b, r, dr, us: (0, 0, 0, 0)),
            ],
            out_specs=pl.BlockSpec((tq, naw), lambda b, r, dr, us: (b * n_steps + r, 0)),
            scratch_shapes=[pltpu.VMEM((2, n_union * GRID_W + dims.ctx, 2 * tq), F32)],
        ),
        compiler_params=_cparams(("parallel", "arbitrary"), 48),
        name="neighbourhood_attention",
    )(jnp.asarray(np.asarray(dr_idx, np.int32).reshape(-1)), jnp.asarray(ustarts, I32), p, p, p, kvc, kvc, blocks)


CONV_HALO = 16
CONV_RB = 128
CONV_MM = 256
SUBLANES = 8


def _glu_conv_kernel(h_ref, wa_ref, wg_ref, ba_ref, bg_ref, dw_ref, db_ref, o_ref, scr, *, seq):
    tn = o_ref.shape[1]
    scr[0:CONV_HALO, :] = jnp.zeros((CONV_HALO, tn), F32)
    scr[CONV_HALO + seq:2 * CONV_HALO + seq, :] = jnp.zeros((CONV_HALO, tn), F32)
    base = CONV_HALO - CONV_K // 2
    n_shift = (CONV_K + base + SUBLANES - 1) // SUBLANES

    def project(c):
        row0 = c * CONV_MM
        h = h_ref[pl.ds(row0, CONV_MM), :]
        a = jnp.dot(h, wa_ref[...], preferred_element_type=F32) + ba_ref[...]
        gt = jnp.dot(h, wg_ref[...], preferred_element_type=F32) + bg_ref[...]
        scr[pl.ds(row0 + CONV_HALO, CONV_MM), :] = a * _sigmoid(gt)

    def conv_block(rb):
        row0 = rb * CONV_RB
        for cb in range(tn // LANES):
            cols = pl.ds(cb * LANES, LANES)
            acc = db_ref[:, cols]
            for ph in range(SUBLANES):
                part = None
                for st in range(n_shift):
                    k = st * SUBLANES + ph - base
                    if 0 <= k < CONV_K:
                        rows = pl.ds(row0 + st * SUBLANES, CONV_RB + SUBLANES)
                        term = scr[rows, cols] * dw_ref[k:k + 1, cols]
                        part = term if part is None else part + term
                acc = acc + part[ph:ph + CONV_RB]
            o_ref[pl.ds(row0, CONV_RB), cols] = acc.astype(o_ref.dtype)

    n_mm = seq // CONV_MM
    n_rb = seq // CONV_RB
    always = pl.program_id(0) >= 0
    project(0)
    done = 0
    for c in range(n_mm):
        ready = ((c + 1) * CONV_MM - CONV_K // 2) // CONV_RB if c + 1 < n_mm else n_rb

        @pl.when(always)
        def _(c=c, done=done, ready=ready):
            if c + 1 < n_mm:
                project(c + 1)
            for rb in range(done, ready):
                conv_block(rb)

        done = ready


def glu_conv(h, w, b, dw_w, dw_b, dims, tn):
    d = h.shape[1]
    half = w.shape[1] // 2
    nblk = half // tn
    b2 = b.reshape(1, 2 * half)
    kern = functools.partial(_glu_conv_kernel, seq=dims.seq)
    return pl.pallas_call(
        kern,
        out_shape=jax.ShapeDtypeStruct((dims.batch * dims.seq, half), BF16),
        grid=(dims.batch, nblk),
        in_specs=[
            pl.BlockSpec((dims.seq, d), lambda s, j: (s, 0)),
            pl.BlockSpec((d, tn), lambda s, j: (0, j)),
            pl.BlockSpec((d, tn), lambda s, j: (0, j + nblk)),
            pl.BlockSpec((1, tn), lambda s, j: (0, j)),
            pl.BlockSpec((1, tn), lambda s, j: (0, j + nblk)),
            pl.BlockSpec((CONV_K, tn), lambda s, j: (0, j)),
            pl.BlockSpec((1, tn), lambda s, j: (0, j)),
        ],
        out_specs=pl.BlockSpec((dims.seq, tn), lambda s, j: (s, j)),
        scratch_shapes=[pltpu.VMEM((dims.seq + 2 * CONV_HALO, tn), F32)],
        compiler_params=_cparams(("parallel", "arbitrary"), 48),
        name="glu_conv",
    )(h, w, w, b2, b2, dw_w, dw_b.reshape(1, half))


def _route(logits_t, rb):
    sel = [_sigmoid(logits_t[e:e + 1, :]) + rb[e:e + 1, :] for e in range(N_EXPERTS)]
    gs = []
    for g in range(N_GROUPS):
        v0, v1, v2, v3 = sel[4 * g:4 * g + 4]
        hi1, lo1 = jnp.maximum(v0, v1), jnp.minimum(v0, v1)
        hi2, lo2 = jnp.maximum(v2, v3), jnp.minimum(v2, v3)
        gs.append(jnp.maximum(hi1, hi2) + jnp.maximum(jnp.minimum(hi1, hi2), jnp.maximum(lo1, lo2)))
    best = gs[0]
    bg = jnp.zeros(best.shape, I32)
    for g in range(1, N_GROUPS):
        upd = gs[g] > best
        bg = jnp.where(upd, g, bg)
        best = jnp.where(upd, gs[g], best)
    v = []
    for i in range(EXPERTS_PER_GROUP):
        vi = sel[i]
        for g in range(1, N_GROUPS):
            vi = jnp.where(bg == g, sel[4 * g + i], vi)
        v.append(vi)
    picked = []
    for i in range(EXPERTS_PER_GROUP):
        rank = jnp.zeros(best.shape, I32)
        for j in range(EXPERTS_PER_GROUP):
            if j == i:
                continue
            ahead = (v[j] > v[i]) | ((v[j] == v[i]) & (j < i))
            rank = rank + ahead.astype(I32)
        picked.append(rank < 2)
    code = jnp.full(best.shape, len(PAIRS) - 1, I32)
    for idx in range(len(PAIRS) - 2, -1, -1):
        a, b = PAIRS[idx]
        code = jnp.where(picked[a] & picked[b], idx, code)
    return bg * len(PAIRS) + code


def _outproj_kernel(a_ref, b_ref, lng_ref, lnb_ref, w_ref, bias_ref, x_ref, mod_ref, g_ref, rw1_ref, rw2_ref,
                    rb_ref, x1_ref, h2_ref, cls_ref, *, ln_swish):
    ka = a_ref.shape[1]
    tm = a_ref.shape[0]
    if ln_swish:
        t = jnp.concatenate([a_ref[...], b_ref[...]], axis=-1).astype(F32)
        mu = jnp.mean(t, axis=-1, keepdims=True)
        tc = t - mu
        var = jnp.mean(tc * tc, axis=-1, keepdims=True)
        z = tc * lax.rsqrt(var + EPS) * lng_ref[...] + lnb_ref[...]
        z = (z * _sigmoid(z)).astype(BF16)
        y = jnp.dot(z, w_ref[...], preferred_element_type=F32) + bias_ref[...]
    else:
        y = jnp.dot(a_ref[...], w_ref[0:ka, :], preferred_element_type=F32)
        y = y + jnp.dot(b_ref[...], w_ref[ka:, :], preferred_element_type=F32) + bias_ref[...]
    x1 = x_ref[...] + mod_ref[0, 2:3, :] * y
    x1_ref[...] = x1
    h2 = _rms_mod(x1, g_ref[...], mod_ref[0, 4:5, :], mod_ref[0, 3:4, :])
    _store_packed_rows(h2_ref, h2, tm)
    hi = h2.astype(BF16)
    lo = (h2 - hi.astype(F32)).astype(BF16)
    s = jnp.dot(hi, rw1_ref[...], preferred_element_type=F32) + jnp.dot(lo, rw2_ref[...], preferred_element_type=F32)
    st = s.T
    logits_t = st[0:N_EXPERTS, :] + st[N_EXPERTS:2 * N_EXPERTS, :]
    cls_ref[...] = _route(logits_t, rb_ref[...])


def outproj_residual_route(a, a_col, b, b_col, ln_g, ln_b, w, bias, x2d, mod, ffn_g, router_w, router_b,
                           rows_per_mod, tm, ln_swish):
    m, d = x2d.shape
    ka = kb = w.shape[0] // 2
    kin = ka + kb
    n_slab = d // (2 * LANES)
    rw_hi = router_w.astype(BF16)
    rw_lo = (router_w - rw_hi.astype(F32)).astype(BF16)
    pad = jnp.zeros((d, LANES - 2 * N_EXPERTS), BF16)
    rw1 = jnp.concatenate([rw_hi, rw_lo, pad], axis=1)
    rw2 = jnp.concatenate([rw_hi, jnp.zeros((d, N_EXPERTS), BF16), pad], axis=1)
    kern = functools.partial(_outproj_kernel, ln_swish=ln_swish)
    return pl.pallas_call(
        kern,
        out_shape=(
            jax.ShapeDtypeStruct((m, d), F32),
            jax.ShapeDtypeStruct((m * n_slab, LANES), I32),
            jax.ShapeDtypeStruct((1, m), I32),
        ),
        grid=(m // tm,),
        in_specs=[
            pl.BlockSpec((tm, ka), lambda i: (i, a_col)),
            pl.BlockSpec((tm, kb), lambda i: (i, b_col)),
            pl.BlockSpec((1, kin), lambda i: (0, 0)),
            pl.BlockSpec((1, kin), lambda i: (0, 0)),
            pl.BlockSpec((kin, d), lambda i: (0, 0), pipeline_mode=pl.Buffered(1)),
            pl.BlockSpec((1, d), lambda i: (0, 0)),
            pl.BlockSpec((tm, d), lambda i: (i, 0)),
            pl.BlockSpec((1, N_MOD, d), lambda i: ((i * tm) // rows_per_mod, 0, 0)),
            pl.BlockSpec((1, d), lambda i: (0, 0)),
            pl.BlockSpec((d, LANES), lambda i: (0, 0)),
            pl.BlockSpec((d, LANES), lambda i: (0, 0)),
            pl.BlockSpec((N_EXPERTS, 1), lambda i: (0, 0)),
        ],
        out_specs=(
            pl.BlockSpec((tm, d), lambda i: (i, 0)),
            pl.BlockSpec((tm * n_slab, LANES), lambda i: (i, 0)),
            pl.BlockSpec((1, tm), lambda i: (0, i)),
        ),
        compiler_params=_cparams(("parallel",), 56),
        name="outproj_residual_route",
    )(a, b, ln_g.reshape(1, kin), ln_b.reshape(1, kin), w, bias.reshape(1, d), x2d, mod, ffn_g.reshape(1, d),
      rw1, rw2, router_b.astype(F32).reshape(N_EXPERTS, 1))


def _class_onehot(cls_ref):
    blk = cls_ref.shape[1]
    return (lax.broadcasted_iota(I32, (CLASS_PAD, blk), 0) == cls_ref[...]).astype(F32)


def _count_kernel(cls_ref, cnt_ref):
    @pl.when(pl.program_id(0) == 0)
    def _():
        cnt_ref[...] = jnp.zeros_like(cnt_ref)

    cnt_ref[...] += jnp.sum(_class_onehot(cls_ref), axis=1, keepdims=True)


def _dest_kernel(cls_ref, start_ref, dest_ref, carry_scr):
    @pl.when(pl.program_id(0) == 0)
    def _():
        carry_scr[...] = jnp.zeros_like(carry_scr)

    blk = cls_ref.shape[1]
    onehot = _class_onehot(cls_ref)
    tri = (lax.broadcasted_iota(I32, (blk, blk), 0) <= lax.broadcasted_iota(I32, (blk, blk), 1)).astype(BF16)
    cum = jnp.dot(onehot.astype(BF16), tri, preferred_element_type=F32)
    pos = cum - 1.0 + carry_scr[...] + start_ref[...]
    dest_ref[...] = jnp.sum(onehot * pos, axis=0, keepdims=True).astype(I32)
    carry_scr[...] += jnp.sum(onehot, axis=1, keepdims=True)


def sort_tokens(cls, tile):
    n = cls.shape[1]
    nb = n // SORT_BLK
    counts = pl.pallas_call(
        _count_kernel,
        out_shape=jax.ShapeDtypeStruct((CLASS_PAD, 1), F32),
        grid=(nb,),
        in_specs=[pl.BlockSpec((1, SORT_BLK), lambda j: (0, j))],
        out_specs=pl.BlockSpec((CLASS_PAD, 1), lambda j: (0, 0)),
        compiler_params=_cparams(("arbitrary",), 32),
        name="class_counts",
    )(cls)
    counts = counts.reshape(CLASS_PAD).astype(I32)
    padded = ((counts + tile - 1) // tile) * tile
    starts = (jnp.cumsum(padded) - padded).astype(F32).reshape(CLASS_PAD, 1)
    dest = pl.pallas_call(
        _dest_kernel,
        out_shape=jax.ShapeDtypeStruct((1, n), I32),
        grid=(nb,),
        in_specs=[pl.BlockSpec((1, SORT_BLK), lambda j: (0, j)),
                  pl.BlockSpec((CLASS_PAD, 1), lambda j: (0, 0))],
        out_specs=pl.BlockSpec((1, SORT_BLK), lambda j: (0, j)),
        scratch_shapes=[pltpu.VMEM((CLASS_PAD, 1), F32)],
        compiler_params=_cparams(("arbitrary",), 32),
        name="sorted_positions",
    )(cls, starts)
    return dest, counts


def _inverse_kernel(dest_ref, src_ref, *, n_tok, n_rows):
    def clear(k, carry):
        for u in range(SCALAR_UNROLL):
            src_ref[k * SCALAR_UNROLL + u] = 0
        return carry

    lax.fori_loop(0, n_rows // SCALAR_UNROLL, clear, 0)

    def scatter(k, carry):
        for u in range(SCALAR_UNROLL):
            t = k * SCALAR_UNROLL + u
            src_ref[dest_ref[t]] = t
        return carry

    lax.fori_loop(0, n_tok // SCALAR_UNROLL, scatter, 0)


def inverse_map(dest, n_rows):
    n_tok = dest.shape[0]
    kern = functools.partial(_inverse_kernel, n_tok=n_tok, n_rows=n_rows)
    return pl.pallas_call(
        kern,
        out_shape=jax.ShapeDtypeStruct((n_rows,), I32),
        in_specs=[pl.BlockSpec(memory_space=pltpu.SMEM)],
        out_specs=pl.BlockSpec(memory_space=pltpu.SMEM),
        name="inverse_map",
    )(dest)


def _start_row_gather(idx_ref, base, src_hbm, buf, sem, slot, tm, s):
    def body(k, carry):
        for u in range(SCALAR_UNROLL):
            r = k * SCALAR_UNROLL + u
            pltpu.make_async_copy(
                src_hbm.at[pl.ds(pl.multiple_of(idx_ref[base + r] * s, s), s), :],
                buf.at[slot, pl.ds(pl.multiple_of(r * s, s), s), :],
                sem.at[slot]).start(priority=u % 2)
        return carry

    lax.fori_loop(0, tm // SCALAR_UNROLL, body, 0)


def _wait_row_gather(src_hbm, buf, sem, slot, tm, s):
    pltpu.make_async_copy(src_hbm.at[pl.ds(0, tm * s), :], buf.at[slot], sem.at[slot]).wait()


HI_HALF = -65536


def _store_packed_rows(ref, v, tm):
    half = v.shape[1] // 2
    n_slab = half // LANES
    for c in range(n_slab):
        a = v[:, c * LANES:(c + 1) * LANES].astype(BF16).astype(F32)
        b = v[:, half + c * LANES:half + (c + 1) * LANES].astype(BF16).astype(F32)
        word = lax.bitcast_convert_type(a, I32) | lax.shift_right_logical(lax.bitcast_convert_type(b, I32), 16)
        ref[pl.ds(c, tm, stride=n_slab), :] = word


def _load_packed_rows(ref, tm, n_slab):
    words = [ref[pl.ds(c, tm, stride=n_slab), :] for c in range(n_slab)]
    hi = [lax.bitcast_convert_type(w & HI_HALF, F32) for w in words]
    lo = [lax.bitcast_convert_type(lax.shift_left(w, 16), F32) for w in words]
    return jnp.concatenate(hi + lo, axis=-1)


def _moe_kernel(src_ref, e1_ref, e2_ref, valid_ref, h_hbm, wg1_ref, wg2_ref, wu1_ref, wu2_ref,
                wd1_ref, wd2_ref, rwt_ref, ys_ref, buf, sem, *, tm, n_slab, n_tiles):
    i = pl.program_id(0)
    slot = i % 2

    @pl.when(i == 0)
    def _():
        _start_row_gather(src_ref, 0, h_hbm, buf, sem, 0, tm, n_slab)

    nxt = jnp.minimum(i + 1, n_tiles - 1)

    @pl.when((i + 1 < n_tiles) & (valid_ref[nxt] == 1))
    def _():
        _start_row_gather(src_ref, nxt * tm, h_hbm, buf, sem, 1 - slot, tm, n_slab)

    @pl.when(valid_ref[i] == 0)
    def _():
        ys_ref[...] = jnp.zeros_like(ys_ref)

    @pl.when(valid_ref[i] == 1)
    def _():
        _wait_row_gather(h_hbm, buf, sem, slot, tm, n_slab)
        xr = _load_packed_rows(buf.at[slot], tm, n_slab)
        l1 = jnp.sum(xr * rwt_ref[pl.ds(e1_ref[i], 1), :], axis=-1, keepdims=True)
        l2 = jnp.sum(xr * rwt_ref[pl.ds(e2_ref[i], 1), :], axis=-1, keepdims=True)
        s1 = _sigmoid(l1)
        s2 = _sigmoid(l2)
        tot = s1 + s2

        def expert(wg_ref, wu_ref, weight):
            gt = jnp.dot(xr, wg_ref[0], preferred_element_type=F32)
            up = jnp.dot(xr, wu_ref[0], preferred_element_type=F32)
            return gt * _sigmoid(gt) * up * weight

        a1 = expert(wg1_ref, wu1_ref, s1 / tot)
        a2 = expert(wg2_ref, wu2_ref, s2 / tot)
        y = jnp.dot(a1, wd1_ref[0], preferred_element_type=F32)
        y = y + jnp.dot(a2, wd2_ref[0], preferred_element_type=F32)
        _store_packed_rows(ys_ref, y, tm)


def moe_ffn(src, tile_e1, tile_e2, tile_valid, h_rows, w_gate, w_up, w_down, layer, router_w_t, d, tm):
    n_tiles = tile_e1.shape[0]
    n_slab = d // (2 * LANES)
    fe = w_gate.shape[3]
    kern = functools.partial(_moe_kernel, tm=tm, n_slab=n_slab, n_tiles=n_tiles)

    def w_in_spec(which):
        return pl.BlockSpec((None, 1, d, fe), lambda i, src, e1, e2, valid: (layer, (e1, e2)[which][i], 0, 0))

    def w_out_spec(which):
        return pl.BlockSpec((None, 1, fe, d), lambda i, src, e1, e2, valid: (layer, (e1, e2)[which][i], 0, 0))

    return pl.pallas_call(
        kern,
        out_shape=jax.ShapeDtypeStruct((n_tiles * tm * n_slab, LANES), I32),
        grid_spec=pltpu.PrefetchScalarGridSpec(
            num_scalar_prefetch=4,
            grid=(n_tiles,),
            in_specs=[
                pl.BlockSpec(memory_space=pl.ANY),
                w_in_spec(0), w_in_spec(1), w_in_spec(0), w_in_spec(1), w_out_spec(0), w_out_spec(1),
                pl.BlockSpec((N_EXPERTS, d), lambda i, src, e1, e2, valid: (0, 0)),
            ],
            out_specs=pl.BlockSpec((tm * n_slab, LANES), lambda i, src, e1, e2, valid: (i, 0)),
            scratch_shapes=[pltpu.VMEM((2, tm * n_slab, LANES), I32), pltpu.SemaphoreType.DMA((2,))],
        ),
        compiler_params=_cparams(("arbitrary",), 62),
        name="moe_ffn",
    )(src, tile_e1, tile_e2, tile_valid, h_rows, w_gate, w_gate, w_up, w_up, w_down, w_down, router_w_t)


def _combine_kernel(dest_ref, ys_hbm, x1_ref, mod_ref, g_ref, modn_ref, *rest, tm, s, n_tiles, final):
    if final:
        o_ref, buf, sem = rest
    else:
        o_ref, hn_ref, buf, sem = rest
    i = pl.program_id(0)
    slot = i % 2

    @pl.when(i == 0)
    def _():
        _start_row_gather(dest_ref, 0, ys_hbm, buf, sem, 0, tm, s)

    @pl.when(i + 1 < n_tiles)
    def _():
        _start_row_gather(dest_ref, (i + 1) * tm, ys_hbm, buf, sem, 1 - slot, tm, s)

    _wait_row_gather(ys_hbm, buf, sem, slot, tm, s)
    y = _load_packed_rows(buf.at[slot], tm, s)
    x2 = x1_ref[...] + mod_ref[0, 5:6, :] * y
    if final:
        ms = jnp.mean(x2 * x2, axis=-1, keepdims=True)
        o_ref[...] = x2 * lax.rsqrt(ms + EPS) * g_ref[...]
    else:
        o_ref[...] = x2
        hn_ref[...] = _rms_mod(x2, g_ref[...], modn_ref[0, 1:2, :], modn_ref[0, 0:1, :]).astype(hn_ref.dtype)


def combine_residual(dest, ys, x1, mod, g, mod_next, rows_per_mod, tm, final):
    m, d = x1.shape
    s = d // (2 * LANES)
    n_tiles = m // tm
    kern = functools.partial(_combine_kernel, tm=tm, s=s, n_tiles=n_tiles, final=final)
    row_spec = pl.BlockSpec((tm, d), lambda i, dest: (i, 0))
    mod_spec = pl.BlockSpec((1, N_MOD, d), lambda i, dest: ((i * tm) // rows_per_mod, 0, 0))
    if final:
        out_shape, out_specs = jax.ShapeDtypeStruct((m, d), F32), row_spec
    else:
        out_shape = (jax.ShapeDtypeStruct((m, d), F32), jax.ShapeDtypeStruct((m, d), BF16))
        out_specs = (row_spec, row_spec)
    return pl.pallas_call(
        kern,
        out_shape=out_shape,
        grid_spec=pltpu.PrefetchScalarGridSpec(
            num_scalar_prefetch=1,
            grid=(n_tiles,),
            in_specs=[
                pl.BlockSpec(memory_space=pl.ANY),
                row_spec,
                mod_spec,
                pl.BlockSpec((1, d), lambda i, dest: (0, 0)),
                mod_spec,
            ],
            out_specs=out_specs,
            scratch_shapes=[pltpu.VMEM((2, tm * s, LANES), I32), pltpu.SemaphoreType.DMA((2,))],
        ),
        compiler_params=_cparams(("arbitrary",), 40),
        name="combine_residual",
    )(dest, ys, x1, mod, g.reshape(1, d), mod_next)


def _tile_tables(counts, n_tiles, tm):
    tiles_per_class = (counts[:N_CLASSES] + tm - 1) // tm
    tile_end = jnp.cumsum(tiles_per_class)
    total = tile_end[-1]
    idx = jnp.arange(n_tiles, dtype=I32)
    valid = idx < total
    blk = jnp.where(valid, idx, total - 1)
    cls = jnp.minimum(jnp.sum((tile_end[None, :] <= blk[:, None]).astype(I32), axis=1), N_CLASSES - 1)
    pair = jnp.asarray(PAIRS, I32)
    grp = cls // len(PAIRS)
    e1 = grp * EXPERTS_PER_GROUP + pair[cls % len(PAIRS), 0]
    e2 = grp * EXPERTS_PER_GROUP + pair[cls % len(PAIRS), 1]
    return e1.astype(I32), e2.astype(I32), valid.astype(I32)


def grouped_moe_residual(x1, h2_rows, cls, mod, router_w, w_gate, w_up, w_down, layer, g, mod_next,
                         rows_per_mod, final):
    m, d = x1.shape
    n_tiles = m // MOE_TM + N_CLASSES
    dest, counts = sort_tokens(cls, MOE_TM)
    dest = dest.reshape(m)
    e1, e2, valid = _tile_tables(counts, n_tiles, MOE_TM)
    src = inverse_map(dest, n_tiles * MOE_TM)
    ys = moe_ffn(src, e1, e2, valid, h2_rows, w_gate, w_up, w_down, layer, router_w.T.astype(F32), d, MOE_TM)
    return combine_residual(dest, ys, x1, mod, g, mod_next, rows_per_mod, GATHER_TM, final)


def _forward(dims, x, c, ctx, c_ctx, ada_w, ada_b, mix_norm_g, ffn_norm_g, ev_w_in, ev_rpb, ev_w_out,
             od_w_in, od_b_in, od_dw_w, od_dw_b, od_ln_g, od_ln_b, od_w_out, od_b_out,
             router_w, router_b, moe_w_gate, moe_w_up, moe_w_down, final_norm_g, *, tiles):
    b, seq, d = dims.batch, dims.seq, dims.d
    n = b * seq
    depth = ada_w.shape[0]
    assert depth == 2, "layer 0 is the Fourier/attention mixer, layer 1 the Conformer mixer"
    x2d = x.reshape(n, d)
    ctx2d = ctx.reshape(b * dims.ctx, d)

    cvec = jnp.zeros((16, d), F32).at[:b].set(c).at[b].set(c_ctx)
    mod_all = adaln_mod(cvec, ada_w, ada_b, tiles["mod_tn"]).reshape(depth, 16, N_MOD, d)

    mod0 = mod_all[0, :b]
    modc0 = mod_all[0, b:b + 1]
    w_in = ev_w_in[0].astype(BF16)
    p = norm_mod_matmul(x2d, mix_norm_g[0], mod0, w_in, seq, 0, 1, tiles["in_tm"], w_in.shape[1], 0)
    kvc = norm_mod_matmul(ctx2d, mix_norm_g[0], modc0, w_in, b * dims.ctx, 0, 1,
                          min(tiles["in_tm"], b * dims.ctx), 2 * dims.naw, 1)
    fo = fourier_mix(p, dims, tiles["four_tm"])
    ao = neighbourhood_attention(p, kvc, ev_rpb[0], dims)
    mod1 = mod_all[1, :b]
    mix_w = ev_w_out.shape[1]
    x1, h2_rows, cls = outproj_residual_route(
        fo, 0, ao, 0, jnp.ones((mix_w,), F32), jnp.zeros((mix_w,), F32), ev_w_out[0].astype(BF16),
        jnp.zeros((d,), F32), x2d, mod0, ffn_norm_g[0], router_w, router_b, seq, tiles["out_tm"], False)
    x2, hmix1 = grouped_moe_residual(x1, h2_rows, cls, mod0, router_w, moe_w_gate, moe_w_up, moe_w_down, 0,
                                     mix_norm_g[1], mod1, seq, False)

    yc = glu_conv(hmix1, od_w_in[0].astype(BF16), od_b_in[0], od_dw_w[0], od_dw_b[0], dims, tiles["glu_tn"])
    x3, h4_rows, cls1 = outproj_residual_route(
        yc, 0, yc, 1, od_ln_g[0], od_ln_b[0], od_w_out[0].astype(BF16), od_b_out[0], x2, mod1, ffn_norm_g[1],
        router_w, router_b, seq, tiles["out_tm"], True)
    out = grouped_moe_residual(x3, h4_rows, cls1, mod1, router_w, moe_w_gate, moe_w_up, moe_w_down, 1,
                               final_norm_g, mod1, seq, True)
    return out.reshape(b, seq, d)


TILES = dict(mod_tn=1024, in_tm=256, glu_tn=256, four_tm=512, out_tm=256)


def kernel(x, c, ctx, c_ctx, ada_w, ada_b, mix_norm_g, ffn_norm_g, ev_w_in, ev_rpb, ev_w_out, od_w_in, od_b_in,
           od_dw_w, od_dw_b, od_ln_g, od_ln_b, od_w_out, od_b_out, router_w, router_b, moe_w_gate, moe_w_up,
           moe_w_down, final_norm_g):
    b, seq, d = x.shape
    dims = Dims(batch=b, seq=seq, d=d, ctx=ctx.shape[1], fw=ev_w_out.shape[1] - NA_HEADS * NA_HEAD_DIM,
                naw=NA_HEADS * NA_HEAD_DIM, fe=moe_w_gate.shape[3])
    return _forward(dims, x, c, ctx, c_ctx, ada_w, ada_b, mix_norm_g, ffn_norm_g, ev_w_in, ev_rpb, ev_w_out,
                    od_w_in, od_b_in, od_dw_w, od_dw_b, od_ln_g, od_ln_b, od_w_out, od_b_out,
                    router_w, router_b, moe_w_gate, moe_w_up, moe_w_down, final_norm_g, tiles=TILES)
```
